```python
import math
import jax, jax.numpy as jnp
from jax import lax
import numpy as np

D_MODEL = 4096
BATCH = 2
SEQ = 8192
DEPTH = 1
DEC_BATCH = 8
DEC_SEQ = 64
PAST_LEN = 1024

CHUNK = 64
Q_BLOCK = 128
N_HEADS = 16
N_KV_HEADS = 4
HEAD_DIM = 128
N_IDX_HEADS = 16
IDX_DIM = 64
TOP_K_MAX = 256
N_BUCKETS = 32
MAX_DISTANCE = 128
SSD_HEADS = 32
SSD_HEAD_DIM = 64
D_SSD = SSD_HEADS * SSD_HEAD_DIM
SSD_GROUPS = 4
D_STATE = 128
SSD_CONV = 4
SSD_CONV_DIM = D_SSD + 2 * SSD_GROUPS * D_STATE
D_FF = 11008
FFN_CONV = 3
EPS = 1e-6
IN_SIZES = (N_HEADS * HEAD_DIM, N_KV_HEADS * HEAD_DIM, N_KV_HEADS * HEAD_DIM, N_IDX_HEADS * IDX_DIM, IDX_DIM, N_IDX_HEADS, D_SSD, SSD_CONV_DIM, SSD_HEADS, D_MODEL, D_MODEL)
D_IN = sum(IN_SIZES)

kernel_name = "hybrid_dsa_ssd_convffn_stream_step"

F32 = jnp.float32


def rmsnorm(x, g):
    xf = x.astype(F32)
    y = xf * lax.rsqrt(jnp.mean(xf * xf, axis=-1, keepdims=True) + EPS)
    return (y * g.astype(F32)).astype(x.dtype)


def split_cols(z, sizes):
    offs = np.cumsum(np.array(sizes))[:-1].tolist()
    return jnp.split(z, offs, axis=-1)


def causal_dwconv(x, hist, w, b):
    width = w.shape[0]
    xp = jnp.concatenate([hist.astype(x.dtype), x], axis=1)
    y = lax.conv_general_dilated(xp, w[:, None, :].astype(x.dtype), window_strides=(1,), padding="VALID",
                                 dimension_numbers=("NWC", "WIO", "NWC"), feature_group_count=x.shape[-1])
    return y + b.astype(x.dtype), xp[:, xp.shape[1] - (width - 1):]


def t5_bucket(rel):
    nb = N_BUCKETS // 2
    max_exact = nb // 2
    ret = jnp.where(rel > 0, nb, 0)
    n = jnp.abs(rel)
    nf = jnp.maximum(n, 1).astype(F32)
    large = max_exact + (jnp.log(nf / max_exact) / math.log(MAX_DISTANCE / max_exact) * (nb - max_exact)).astype(jnp.int32)
    large = jnp.minimum(large, nb - 1)
    return ret + jnp.where(n < max_exact, n, large)


def dsa_block(q, qi, wi, qpos, k, v, ki, rel_bias, n_select):
    bsz, tq = q.shape[0], q.shape[1]
    n_keys = k.shape[1]
    grp = N_HEADS // N_KV_HEADS
    kpos = jnp.arange(n_keys)
    idx_logits = jnp.einsum('bqhd,bsd->bqhs', qi.astype(F32), ki.astype(F32)) * (IDX_DIM ** -0.5)
    score = jnp.einsum('bqhs,bqh->bqs', jax.nn.relu(idx_logits), wi.astype(F32)) * (N_IDX_HEADS ** -0.5)
    admissible = (kpos[None, :] // CHUNK) <= (qpos[:, None] // CHUNK)
    score = jnp.where(admissible[None], score, -jnp.inf)
    top_val, top_idx = lax.top_k(score, n_select)
    valid = jnp.isfinite(top_val)
    gather = jax.vmap(lambda arr, ix: arr[ix])
    kg = gather(k, top_idx)
    vg = gather(v, top_idx)
    qg = q.reshape(bsz, tq, N_KV_HEADS, grp, HEAD_DIM)
    s = jnp.einsum('bqngd,bqknd->bqngk', qg, kg, preferred_element_type=F32) * (HEAD_DIM ** -0.5)
    bucket = t5_bucket(top_idx - qpos[None, :, None])
    bias = rel_bias[bucket].astype(F32).reshape(bsz, tq, n_select, N_KV_HEADS, grp).transpose(0, 1, 3, 4, 2)
    s = jnp.where(valid[:, :, None, None, :], s + bias, -jnp.inf)
    p = jax.nn.softmax(s, axis=-1).astype(v.dtype)
    o = jnp.einsum('bqngk,bqknd->bqngd', p, vg)
    return o.reshape(bsz, tq, N_HEADS * HEAD_DIM)


def segsum(a):
    t = a.shape[-1]
    x = jnp.broadcast_to(a[..., :, None], a.shape + (t,))
    x = jnp.where(jnp.tril(jnp.ones((t, t), bool), -1), x, 0.0)
    s = jnp.cumsum(x, axis=-2)
    return jnp.where(jnp.tril(jnp.ones((t, t), bool)), s, -jnp.inf)


def ssd_scan(x, dt, a_neg, bm, cm, h0, chunk):
    bsz, t, nh, hp = x.shape
    g, n = bm.shape[2], bm.shape[3]
    r = nh // g
    nc = t // chunk
    a = (dt * a_neg).reshape(bsz, nc, chunk, nh).transpose(0, 3, 1, 2)
    xd = (x * dt[..., None]).reshape(bsz, nc, chunk, g, r, hp)
    bc = bm.reshape(bsz, nc, chunk, g, n)
    cc = cm.reshape(bsz, nc, chunk, g, n)
    a_cum = jnp.cumsum(a, axis=-1)
    lmat = jnp.exp(segsum(a)).reshape(bsz, g, r, nc, chunk, chunk)
    cb = jnp.einsum('bclgn,bcsgn->bgcls', cc, bc)
    y_diag = jnp.einsum('bgrcls,bcsgrp->bclgrp', cb[:, :, None] * lmat, xd)
    decay_states = jnp.exp(a_cum[..., -1:] - a_cum).reshape(bsz, g, r, nc, chunk).transpose(0, 3, 4, 1, 2)
    states = jnp.einsum('bclgn,bclgrp->bcgrpn', bc, xd * decay_states[..., None])
    states = jnp.concatenate([h0.reshape(bsz, 1, g, r, hp, n), states], axis=1)
    chunk_a = jnp.pad(a_cum[..., -1], ((0, 0), (0, 0), (1, 0)))
    decay_chunk = jnp.exp(segsum(chunk_a)).reshape(bsz, g, r, nc + 1, nc + 1)
    new_states = jnp.einsum('bgrzc,bcgrpn->bzgrpn', decay_chunk, states)
    states_in, h_final = new_states[:, :-1], new_states[:, -1]
    decay_out = jnp.exp(a_cum).reshape(bsz, g, r, nc, chunk).transpose(0, 3, 4, 1, 2)
    y_off = jnp.einsum('bclgn,bcgrpn->bclgrp', cc, states_in) * decay_out[..., None]
    return (y_diag + y_off).reshape(bsz, t, nh, hp), h_final.reshape(bsz, nh, hp, n)


def ssd_branch(z, xbc, dt_raw, conv_hist, ssm0, conv_w, conv_b, dt_bias, a_log, d_skip, norm_g, chunk):
    bsz, t = z.shape[0], z.shape[1]
    xbc_c, conv_new = causal_dwconv(xbc, conv_hist, conv_w, conv_b)
    xbc_c = jax.nn.silu(xbc_c)
    xs, bm, cm = split_cols(xbc_c, (D_SSD, SSD_GROUPS * D_STATE, SSD_GROUPS * D_STATE))
    dt = jax.nn.softplus(dt_raw.astype(F32) + dt_bias.astype(F32))
    a_neg = -jnp.exp(a_log.astype(F32))
    xh = xs.reshape(bsz, t, SSD_HEADS, SSD_HEAD_DIM).astype(F32)
    y, h_new = ssd_scan(xh, dt, a_neg,
                        bm.reshape(bsz, t, SSD_GROUPS, D_STATE).astype(F32),
                        cm.reshape(bsz, t, SSD_GROUPS, D_STATE).astype(F32),
                        ssm0.astype(F32), chunk)
    y = y + d_skip.astype(F32)[:, None] * xh
    y = y.reshape(bsz, t, D_SSD) * jax.nn.silu(z.astype(F32))
    yg = y.reshape(bsz, t, SSD_GROUPS, D_SSD // SSD_GROUPS)
    yg = yg * lax.rsqrt(jnp.mean(yg * yg, axis=-1, keepdims=True) + EPS)
    y = yg.reshape(bsz, t, D_SSD) * norm_g.astype(F32)
    return y.astype(z.dtype), h_new.astype(ssm0.dtype), conv_new


def trunk_layer(x, c, past_k, past_v, past_ik, ssm0, ssd_conv0, ffn_conv0, rel_bias,
                w_ada, b_ada, norm_mix_g, w_in, ssd_conv_w, ssd_conv_b, dt_bias, a_log, d_skip, ssd_norm_g,
                w_attn_o, w_ssd_o, w_out, norm_ffn_g, w_up, ffn_conv_w, ffn_conv_b, w_down):
    bsz, t, _ = x.shape
    past = past_k.shape[1]
    mod = jax.nn.silu(c) @ w_ada + b_ada
    sh1, sc1, g1, sh2, sc2, g2 = [m[:, None, :] for m in jnp.split(mod, 6, axis=-1)]
    u = rmsnorm(x, norm_mix_g) * (1.0 + sc1) + sh1
    proj = u @ w_in
    q, k, v, qi, ki, wi, z, xbc, dt_raw, ga, gs = split_cols(proj, IN_SIZES)
    q = q.reshape(bsz, t, N_HEADS, HEAD_DIM)
    k = k.reshape(bsz, t, N_KV_HEADS, HEAD_DIM)
    v = v.reshape(bsz, t, N_KV_HEADS, HEAD_DIM)
    qi = qi.reshape(bsz, t, N_IDX_HEADS, IDX_DIM)
    k_all = jnp.concatenate([past_k.astype(k.dtype), k], axis=1)
    v_all = jnp.concatenate([past_v.astype(v.dtype), v], axis=1)
    ki_all = jnp.concatenate([past_ik.astype(ki.dtype), ki], axis=1)
    n_select = min(TOP_K_MAX, (past + t) // 4)
    qb = min(Q_BLOCK, t)
    nb = t // qb
    qpos = past + jnp.arange(t)

    def to_blocks(a):
        return a.reshape((bsz, nb, qb) + a.shape[2:]).swapaxes(0, 1)

    def attend(args):
        q_b, qi_b, w_b, pos_b = args
        return dsa_block(q_b, qi_b, w_b, pos_b, k_all, v_all, ki_all, rel_bias, n_select)

    attn = lax.map(attend, (to_blocks(q), to_blocks(qi), to_blocks(wi), qpos.reshape(nb, qb)))
    attn = attn.swapaxes(0, 1).reshape(bsz, t, N_HEADS * HEAD_DIM)
    ssd_out, h_new, ssd_conv_new = ssd_branch(z, xbc, dt_raw, ssd_conv0, ssm0, ssd_conv_w, ssd_conv_b,
                                              dt_bias, a_log, d_skip, ssd_norm_g, min(CHUNK, t))
    merged = jax.nn.sigmoid(ga) * (attn @ w_attn_o) + jax.nn.sigmoid(gs) * (ssd_out @ w_ssd_o)
    x = x + g1 * (merged @ w_out)
    u2 = rmsnorm(x, norm_ffn_g) * (1.0 + sc2) + sh2
    hc, ffn_conv_new = causal_dwconv(u2 @ w_up, ffn_conv0, ffn_conv_w, ffn_conv_b)
    gate, val = jnp.split(hc, 2, axis=-1)
    x = x + g2 * ((jax.nn.silu(gate) * val) @ w_down)
    return x, (k, v, ki, h_new, ssd_conv_new, ffn_conv_new)


def stack_layers(states, i):
    return jnp.stack([s[i] for s in states], axis=0)


def setup_inputs(seed: int = 0) -> dict:
    key = jax.random.key(seed)
    ks = iter(jax.random.split(key, 48))

    def nrm(shape, scale):
        return jax.random.normal(next(ks), shape, F32) * scale

    def unif(shape, lo, hi):
        return jax.random.uniform(next(ks), shape, F32, lo, hi)

    dt0 = jnp.exp(unif((DEPTH, SSD_HEADS), math.log(1e-3), math.log(1e-1)))
    return {
        "x_prompt": nrm((BATCH, SEQ, D_MODEL), 1.0),
        "x_sample": nrm((DEC_BATCH, DEC_SEQ, D_MODEL), 1.0),
        "c_prompt": nrm((BATCH, D_MODEL), 1.0),
        "c_sample": nrm((DEC_BATCH, D_MODEL), 1.0),
        "cache_k": nrm((DEPTH, DEC_BATCH, PAST_LEN, N_KV_HEADS, HEAD_DIM), 1.0),
        "cache_v": nrm((DEPTH, DEC_BATCH, PAST_LEN, N_KV_HEADS, HEAD_DIM), 1.0),
        "cache_idx_k": nrm((DEPTH, DEC_BATCH, PAST_LEN, IDX_DIM), 1.0),
        "state_ssm": nrm((DEPTH, DEC_BATCH, SSD_HEADS, SSD_HEAD_DIM, D_STATE), 0.1),
        "state_ssd_conv": nrm((DEPTH, DEC_BATCH, SSD_CONV - 1, SSD_CONV_DIM), 1.0),
        "state_ffn_conv": nrm((DEPTH, DEC_BATCH, FFN_CONV - 1, 2 * D_FF), 1.0),
        "rel_bias": nrm((N_BUCKETS, N_HEADS), 0.5),
        "w_ada": nrm((DEPTH, D_MODEL, 6 * D_MODEL), 0.5 * D_MODEL ** -0.5),
        "b_ada": nrm((DEPTH, 6 * D_MODEL), 0.02),
        "norm_mix_g": 1.0 + nrm((DEPTH, D_MODEL), 0.05),
        "w_in": nrm((DEPTH, D_MODEL, D_IN), D_MODEL ** -0.5),
        "ssd_conv_w": nrm((DEPTH, SSD_CONV, SSD_CONV_DIM), SSD_CONV ** -0.5),
        "ssd_conv_b": nrm((DEPTH, SSD_CONV_DIM), 0.02),
        "dt_bias": dt0 + jnp.log(-jnp.expm1(-dt0)),
        "a_log": jnp.log(unif((DEPTH, SSD_HEADS), 1.0, 16.0)),
        "d_skip": 1.0 + nrm((DEPTH, SSD_HEADS), 0.1),
        "ssd_norm_g": 1.0 + nrm((DEPTH, D_SSD), 0.05),
        "w_attn_o": nrm((DEPTH, N_HEADS * HEAD_DIM, D_MODEL), (N_HEADS * HEAD_DIM) ** -0.5),
        "w_ssd_o": nrm((DEPTH, D_SSD, D_MODEL), D_SSD ** -0.5),
        "w_out": nrm((DEPTH, D_MODEL, D_MODEL), D_MODEL ** -0.5),
        "norm_ffn_g": 1.0 + nrm((DEPTH, D_MODEL), 0.05),
        "w_up": nrm((DEPTH, D_MODEL, 2 * D_FF), D_MODEL ** -0.5),
        "ffn_conv_w": nrm((DEPTH, FFN_CONV, 2 * D_FF), FFN_CONV ** -0.5),
        "ffn_conv_b": nrm((DEPTH, 2 * D_FF), 0.02),
        "w_down": nrm((DEPTH, D_FF, D_MODEL), D_FF ** -0.5),
        "final_norm_g": 1.0 + nrm((D_MODEL,), 0.05),
    }


def reference(x_prompt, x_sample, c_prompt, c_sample, cache_k, cache_v, cache_idx_k, state_ssm, state_ssd_conv,
              state_ffn_conv, rel_bias, w_ada, b_ada, norm_mix_g, w_in, ssd_conv_w, ssd_conv_b, dt_bias, a_log,
              d_skip, ssd_norm_g, w_attn_o, w_ssd_o, w_out, norm_ffn_g, w_up, ffn_conv_w, ffn_conv_b, w_down,
              final_norm_g):
    dt_ = x_prompt.dtype
    hp, hs = x_prompt, x_sample
    prompt_states, sample_states = [], []
    for l in range(DEPTH):
        layer_w = (w_ada[l], b_ada[l], norm_mix_g[l], w_in[l], ssd_conv_w[l], ssd_conv_b[l], dt_bias[l], a_log[l],
                   d_skip[l], ssd_norm_g[l], w_attn_o[l], w_ssd_o[l], w_out[l], norm_ffn_g[l], w_up[l],
                   ffn_conv_w[l], ffn_conv_b[l], w_down[l])
        hp, st_p = trunk_layer(hp, c_prompt,
                               jnp.zeros((BATCH, 0, N_KV_HEADS, HEAD_DIM), dt_),
                               jnp.zeros((BATCH, 0, N_KV_HEADS, HEAD_DIM), dt_),
                               jnp.zeros((BATCH, 0, IDX_DIM), dt_),
                               jnp.zeros((BATCH, SSD_HEADS, SSD_HEAD_DIM, D_STATE), state_ssm.dtype),
                               jnp.zeros((BATCH, SSD_CONV - 1, SSD_CONV_DIM), dt_),
                               jnp.zeros((BATCH, FFN_CONV - 1, 2 * D_FF), dt_),
                               rel_bias, *layer_w)
        hs, st_s = trunk_layer(hs, c_sample, cache_k[l], cache_v[l], cache_idx_k[l], state_ssm[l],
                               state_ssd_conv[l], state_ffn_conv[l], rel_bias, *layer_w)
        prompt_states.append(st_p)
        sample_states.append(st_s)
    y_prompt = rmsnorm(hp, final_norm_g)
    y_sample = rmsnorm(hs, final_norm_g)
    new_k_prompt = stack_layers(prompt_states, 0)
    new_v_prompt = stack_layers(prompt_states, 1)
    new_idx_k_prompt = stack_layers(prompt_states, 2)
    ssm_prompt = stack_layers(prompt_states, 3)
    ssd_conv_prompt = stack_layers(prompt_states, 4)
    ffn_conv_prompt = stack_layers(prompt_states, 5)
    new_k_sample = stack_layers(sample_states, 0)
    new_v_sample = stack_layers(sample_states, 1)
    new_idx_k_sample = stack_layers(sample_states, 2)
    ssm_sample = stack_layers(sample_states, 3)
    ssd_conv_sample = stack_layers(sample_states, 4)
    ffn_conv_sample = stack_layers(sample_states, 5)
    return (y_prompt, y_sample, new_k_prompt, new_v_prompt, new_idx_k_prompt, ssm_prompt, ssd_conv_prompt,
            ffn_conv_prompt, new_k_sample, new_v_sample, new_idx_k_sample, ssm_sample, ssd_conv_sample,
            ffn_conv_sample)
```

```python
import functools
import math

import numpy as np
import jax
import jax.numpy as jnp
from jax import lax
from jax.experimental import pallas as pl
from jax.experimental.pallas import tpu as pltpu

F32 = jnp.float32
BF16 = jnp.bfloat16
I32 = jnp.int32

LANES = 128
V7X_VMEM_BYTES = 64 * 1024 * 1024
VMEM_LIMIT = V7X_VMEM_BYTES - 8 * 1024 * 1024
INT_MIN = -(2 ** 31)
NEG_BIG = -1e30
HIGHEST = lax.Precision.HIGHEST


class Cfg:
    def __init__(self, **kw):
        self.d_model = 4096
        self.chunk = 64
        self.n_heads = 16
        self.n_kv = 4
        self.head_dim = 128
        self.n_idx_heads = 16
        self.idx_dim = 64
        self.top_k_max = 256
        self.n_buckets = 32
        self.max_distance = 128
        self.ssd_heads = 32
        self.ssd_head_dim = 64
        self.ssd_groups = 4
        self.d_state = 128
        self.ssd_conv = 4
        self.d_ff = 11008
        self.ffn_conv = 3
        self.eps = 1e-6
        self.tm = 1024
        self.tn_in = 768
        self.tn = 512
        self.tn_ff = 256
        self.tm_down = 512
        self.tn_down = 256
        self.tr = 256
        self.tq = 128
        self.tk_far = 512
        self.ssd_chunk = 128
        self.tn_mod = 512
        for k, v in kw.items():
            assert hasattr(self, k), k
            setattr(self, k, v)
        self.d_ssd = self.ssd_heads * self.ssd_head_dim
        self.conv_dim = self.d_ssd + 2 * self.ssd_groups * self.d_state
        self.hq = self.n_heads * self.head_dim
        self.hkv = self.n_kv * self.head_dim
        self.hidx = self.n_idx_heads * self.idx_dim
        self.in_sizes = (self.hq, self.hkv, self.hkv, self.hidx, self.idx_dim, self.n_idx_heads, self.d_ssd,
                         self.conv_dim, self.ssd_heads, self.d_model, self.d_model)
        segs = [("q", self.hq), ("k", self.hkv), ("v", self.hkv), ("qi", self.hidx), ("z", self.d_ssd),
                ("xbc", self.conv_dim), ("ga", self.d_model), ("gs", self.d_model), ("kiw", LANES), ("dt", LANES)]
        off, self.off = 0, {}
        for name, width in segs:
            assert width % LANES == 0
            self.off[name] = off
            off += width
        self.n_packed = -(-off // self.tn_in) * self.tn_in
        assert self.idx_dim + self.n_idx_heads <= LANES and self.ssd_heads <= LANES


def _cparams(sem):
    return pltpu.CompilerParams(dimension_semantics=sem, vmem_limit_bytes=VMEM_LIMIT)


def _tile(n, pref):
    t = min(n, pref)
    assert n % t == 0, (n, pref)
    return t


def _seq_map(t_seq, tm):
    if t_seq >= tm:
        assert t_seq % tm == 0
        per = t_seq // tm
        return tm, 1, (lambda i: i // per)
    assert tm % t_seq == 0
    return t_seq, tm // t_seq, (lambda i: i)


def _mod_body(c_ref, w_ref, b_ref, o_ref):
    c = c_ref[...]
    a = (c * jax.nn.sigmoid(c)).astype(BF16)
    o_ref[...] = jnp.dot(a, w_ref[...].astype(BF16), preferred_element_type=F32) + b_ref[...]


def _modulation(cfg, c, w_ada, b_ada):
    rows, d = c.shape
    n = w_ada.shape[1]
    tn = _tile(n, cfg.tn_mod)
    return pl.pallas_call(
        _mod_body,
        grid=(n // tn,),
        in_specs=[pl.BlockSpec((rows, d), lambda j: (0, 0)),
                  pl.BlockSpec((d, tn), lambda j: (0, j)),
                  pl.BlockSpec((1, tn), lambda j: (0, j))],
        out_specs=pl.BlockSpec((rows, tn), lambda j: (0, j)),
        out_shape=jax.ShapeDtypeStruct((rows, n), F32),
        compiler_params=_cparams(("parallel",)),
        name="adaln_mod",
    )(c, w_ada, b_ada.reshape(1, n))


def _norm_mod_body(eps, sh_row, sc_row, x_ref, g_ref, mod_ref, o_ref):
    x = x_ref[...]
    y = x * lax.rsqrt(jnp.mean(x * x, axis=-1, keepdims=True) + eps) * g_ref[...]
    y = y * (1.0 + mod_ref[sc_row:sc_row + 1, :]) + mod_ref[sh_row:sh_row + 1, :]
    o_ref[...] = y.astype(o_ref.dtype)


def _norm_mod(cfg, x, g, mod, t_seq, sh_row, sc_row):
    m, d = x.shape
    tr = _tile(t_seq, cfg.tr)
    per = t_seq // tr
    return pl.pallas_call(
        functools.partial(_norm_mod_body, cfg.eps, sh_row, sc_row),
        grid=(m // tr,),
        in_specs=[pl.BlockSpec((tr, d), lambda i: (i, 0)),
                  pl.BlockSpec((1, d), lambda i: (0, 0)),
                  pl.BlockSpec((None, 6, d), lambda i: (i // per, 0, 0))],
        out_specs=pl.BlockSpec((tr, d), lambda i: (i, 0)),
        out_shape=jax.ShapeDtypeStruct((m, d), BF16),
        compiler_params=_cparams(("parallel",)),
        name="rmsnorm_mod",
    )(x, g.reshape(1, d), mod)


def _norm_body(eps, x_ref, g_ref, o_ref):
    x = x_ref[...]
    o_ref[...] = x * lax.rsqrt(jnp.mean(x * x, axis=-1, keepdims=True) + eps) * g_ref[...]


def _final_norm(cfg, x, g):
    m, d = x.shape
    tr = _tile(m, cfg.tr)
    return pl.pallas_call(
        functools.partial(_norm_body, cfg.eps),
        grid=(m // tr,),
        in_specs=[pl.BlockSpec((tr, d), lambda i: (i, 0)), pl.BlockSpec((1, d), lambda i: (0, 0))],
        out_specs=pl.BlockSpec((tr, d), lambda i: (i, 0)),
        out_shape=jax.ShapeDtypeStruct((m, d), F32),
        compiler_params=_cparams(("parallel",)),
        name="final_rmsnorm",
    )(x, g.reshape(1, d))


def _mm_body(a_ref, w_ref, o_ref):
    o_ref[...] = jnp.dot(a_ref[...], w_ref[...], preferred_element_type=F32).astype(o_ref.dtype)


def _matmul(a, w, tm, tn, out_dtype, name):
    m, k = a.shape
    n = w.shape[1]
    tm, tn = _tile(m, tm), _tile(n, tn)
    return pl.pallas_call(
        _mm_body,
        grid=(m // tm, n // tn),
        in_specs=[pl.BlockSpec((tm, k), lambda i, j: (i, 0)), pl.BlockSpec((k, tn), lambda i, j: (0, j))],
        out_specs=pl.BlockSpec((tm, tn), lambda i, j: (i, j)),
        out_shape=jax.ShapeDtypeStruct((m, n), out_dtype),
        compiler_params=_cparams(("parallel", "parallel")),
        name=name,
    )(a, w)


def _merge_body(attn_ref, ssd_ref, wa_ref, ws_ref, ga_ref, gs_ref, o_ref):
    a = jnp.dot(attn_ref[...], wa_ref[...], preferred_element_type=F32)
    s = jnp.dot(ssd_ref[...], ws_ref[...], preferred_element_type=F32)
    o_ref[...] = (jax.nn.sigmoid(ga_ref[...]) * a + jax.nn.sigmoid(gs_ref[...]) * s).astype(o_ref.dtype)


def _merge(cfg, attn, ssd, w_attn_o, w_ssd_o, proj):
    m = attn.shape[0]
    d = cfg.d_model
    tm = _tile(m, cfg.tm)
    tn = math.gcd(math.gcd(cfg.off["ga"], cfg.off["gs"]), _tile(d, cfg.tn))
    ga0, gs0 = cfg.off["ga"] // tn, cfg.off["gs"] // tn
    return pl.pallas_call(
        _merge_body,
        grid=(m // tm, d // tn),
        in_specs=[pl.BlockSpec((tm, attn.shape[1]), lambda i, j: (i, 0)),
                  pl.BlockSpec((tm, ssd.shape[1]), lambda i, j: (i, 0)),
                  pl.BlockSpec((attn.shape[1], tn), lambda i, j: (0, j)),
                  pl.BlockSpec((ssd.shape[1], tn), lambda i, j: (0, j)),
                  pl.BlockSpec((tm, tn), lambda i, j: (i, ga0 + j)),
                  pl.BlockSpec((tm, tn), lambda i, j: (i, gs0 + j))],
        out_specs=pl.BlockSpec((tm, tn), lambda i, j: (i, j)),
        out_shape=jax.ShapeDtypeStruct((m, d), BF16),
        compiler_params=_cparams(("parallel", "parallel")),
        name="branch_merge",
    )(attn, ssd, w_attn_o, w_ssd_o, proj, proj)


def _resid_body(rows, groups, gate_row, a_ref, w_ref, x_ref, mod_ref, o_ref):
    acc = jnp.dot(a_ref[...], w_ref[...], preferred_element_type=F32)
    for g in range(groups):
        sl = slice(g * rows, (g + 1) * rows)
        o_ref[sl, :] = x_ref[sl, :] + mod_ref[g, gate_row:gate_row + 1, :] * acc[sl, :]


def _gated_residual(a, w, x, mod, t_seq, gate_row, tm, tn, name):
    m, k = a.shape
    n = w.shape[1]
    tm, tn = _tile(m, tm), _tile(n, tn)
    rows, groups, seq_of = _seq_map(t_seq, tm)
    return pl.pallas_call(
        functools.partial(_resid_body, rows, groups, gate_row),
        grid=(m // tm, n // tn),
        in_specs=[pl.BlockSpec((tm, k), lambda i, j: (i, 0)),
                  pl.BlockSpec((k, tn), lambda i, j: (0, j)),
                  pl.BlockSpec((tm, tn), lambda i, j: (i, j)),
                  pl.BlockSpec((groups, 6, tn), lambda i, j: (seq_of(i), 0, j))],
        out_specs=pl.BlockSpec((tm, tn), lambda i, j: (i, j)),
        out_shape=jax.ShapeDtypeStruct((m, n), F32),
        compiler_params=_cparams(("parallel", "parallel")),
        name=name,
    )(a, w, x, mod)


def _up_body(rows, groups, tiles_per_seq, a_ref, wg_ref, wv_ref, cwg_ref, cwv_ref, cbg_ref, cbv_ref, hg_ref, hv_ref,
             h_ref, ng_ref, nv_ref, carry_ref):
    i, j = pl.program_id(0), pl.program_id(1)
    a = a_ref[...]
    rid = lax.broadcasted_iota(I32, (rows, wg_ref.shape[1]), 0)
    halves = []
    for half, (w_ref, cw_ref, cb_ref, hist_ref, new_ref) in enumerate(
            ((wg_ref, cwg_ref, cbg_ref, hg_ref, ng_ref), (wv_ref, cwv_ref, cbv_ref, hv_ref, nv_ref))):
        up = jnp.dot(a, w_ref[...], preferred_element_type=F32)
        outs = []
        for g in range(groups):
            u = up[g * rows:(g + 1) * rows, :]
            prev = hist_ref[g]
            if groups == 1 and tiles_per_seq > 1:
                prev = jnp.where(i % tiles_per_seq == 0, prev, carry_ref[j, half])
            s1 = jnp.where(rid == 0, prev[1:2, :], pltpu.roll(u, 1, 0))
            s2 = jnp.where(rid == 0, prev[0:1, :], jnp.where(rid == 1, prev[1:2, :], pltpu.roll(u, 2, 0)))
            outs.append(cw_ref[0:1, :] * s2 + cw_ref[1:2, :] * s1 + cw_ref[2:3, :] * u + cb_ref[...])
            last = u[rows - 2:rows, :]
            new_ref[g] = last
            if groups == 1 and tiles_per_seq > 1:
                carry_ref[j, half] = last
        halves.append(outs)
    for g in range(groups):
        gate, val = halves[0][g], halves[1][g]
        h_ref[g * rows:(g + 1) * rows, :] = (gate * jax.nn.sigmoid(gate) * val).astype(h_ref.dtype)


def _ffn_up(cfg, u2, w_up, conv_w, conv_b, hist, t_seq):
    m, d = u2.shape
    nf = cfg.d_ff
    n_seq = m // t_seq
    tm, tn = _tile(m, cfg.tm), _tile(nf, cfg.tn_ff)
    rows, groups, seq_of = _seq_map(t_seq, tm)
    nj = nf // tn
    wspec = lambda half: pl.BlockSpec((d, tn), lambda i, j: (0, j + half * nj))
    cspec = lambda r, half: pl.BlockSpec((r, tn), lambda i, j: (0, j + half * nj))
    hspec = lambda half: pl.BlockSpec((groups, 2, tn), lambda i, j: (seq_of(i), 0, j + half * nj))
    ospec = pl.BlockSpec((groups, 2, tn), lambda i, j: (i, 0, j))
    h, new_g, new_v = pl.pallas_call(
        functools.partial(_up_body, rows, groups, max(1, t_seq // tm)),
        grid=(m // tm, nj),
        in_specs=[pl.BlockSpec((tm, d), lambda i, j: (i, 0)), wspec(0), wspec(1),
                  cspec(3, 0), cspec(3, 1), cspec(1, 0), cspec(1, 1), hspec(0), hspec(1)],
        out_specs=[pl.BlockSpec((tm, tn), lambda i, j: (i, j)), ospec, ospec],
        out_shape=[jax.ShapeDtypeStruct((m, nf), BF16),
                   jax.ShapeDtypeStruct((m // rows, 2, nf), F32),
                   jax.ShapeDtypeStruct((m // rows, 2, nf), F32)],
        scratch_shapes=[pltpu.VMEM((nj, 2, 2, tn), F32)],
        compiler_params=_cparams(("arbitrary", "arbitrary")),
        name="ffn_up_conv_swiglu",
    )(u2, w_up, w_up, conv_w, conv_w, conv_b.reshape(1, -1), conv_b.reshape(1, -1), hist, hist)
    per_seq = t_seq // rows
    return h, jnp.concatenate([new_g[per_seq - 1::per_seq], new_v[per_seq - 1::per_seq]], axis=-1)


def _ssd_body(cfg, lc, z_ref, xbc_ref, dt_ref, hist_ref, s0_ref, cw_ref, cb_ref, dtb_ref, alog_ref, dskip_ref,
              ng_ref, expand_ref, y_ref, state_ref, cnew_ref, xcarry_ref, ydiag_ref):
    nh, hp, ng, ns = cfg.ssd_heads, cfg.ssd_head_dim, cfg.ssd_groups, cfg.d_state
    ds = cfg.d_ssd
    per_group = nh // ng
    c = pl.program_id(1)

    @pl.when(c == 0)
    def _():
        state_ref[...] = s0_ref[...]
        xcarry_ref[0:3, :] = hist_ref[...]

    x = xbc_ref[...]
    prev = xcarry_ref[0:3, :]
    rid = lax.broadcasted_iota(I32, x.shape, 0)
    s1 = jnp.where(rid == 0, prev[2:3, :], pltpu.roll(x, 1, 0))
    s2 = jnp.where(rid == 0, prev[1:2, :], jnp.where(rid == 1, prev[2:3, :], pltpu.roll(x, 2, 0)))
    s3 = jnp.where(rid == 0, prev[0:1, :],
                   jnp.where(rid == 1, prev[1:2, :], jnp.where(rid == 2, prev[2:3, :], pltpu.roll(x, 3, 0))))
    xc = cw_ref[0:1, :] * s3 + cw_ref[1:2, :] * s2 + cw_ref[2:3, :] * s1 + cw_ref[3:4, :] * x + cb_ref[...]
    xc = xc * jax.nn.sigmoid(xc)
    last = x[lc - 3:lc, :]
    xcarry_ref[0:3, :] = last
    cnew_ref[...] = last
    xs = xc[:, :ds]
    bm = xc[:, ds:ds + ng * ns].astype(BF16)
    cm = xc[:, ds + ng * ns:].astype(BF16)

    raw = dt_ref[...] + dtb_ref[...]
    dt = jnp.maximum(raw, 0.0) + jnp.log1p(jnp.exp(-jnp.abs(raw)))
    a = dt * (-jnp.exp(alog_ref[...]))
    ri = lax.broadcasted_iota(I32, (lc, lc), 0)
    ci = lax.broadcasted_iota(I32, (lc, lc), 1)
    causal = ri >= ci
    a_cum = jnp.dot(causal.astype(F32), a, precision=HIGHEST, preferred_element_type=F32)
    eye = (lax.broadcasted_iota(I32, (LANES, LANES), 0) == lax.broadcasted_iota(I32, (LANES, LANES), 1))
    nt = (((1,), (1,)), ((), ()))
    a_cum_t = lax.dot_general(eye.astype(F32), a_cum, nt, precision=HIGHEST, preferred_element_type=F32)
    expand = expand_ref[...]
    dt_x = jnp.dot(dt, expand, precision=HIGHEST, preferred_element_type=F32)
    a_x = jnp.dot(a_cum, expand, precision=HIGHEST, preferred_element_type=F32)
    a_end = a_x[lc - 1:lc, :]
    xd = xs * dt_x

    for g in range(ng):
        cb = lax.dot_general(cm[:, g * ns:(g + 1) * ns], bm[:, g * ns:(g + 1) * ns], nt, preferred_element_type=F32)
        for r in range(per_group):
            h = g * per_group + r
            seg = a_cum[:, h:h + 1] - a_cum_t[h:h + 1, :]
            lmat = jnp.exp(jnp.where(causal, seg, -jnp.inf))
            ydiag_ref[:, h * hp:(h + 1) * hp] = jnp.dot((cb * lmat).astype(BF16), xd[:, h * hp:(h + 1) * hp].astype(BF16),
                                                        preferred_element_type=F32)

    state = state_ref[...]
    xdd = (xd * jnp.exp(a_end - a_x)).astype(BF16)
    w = per_group * hp
    y_off, new_cols = [], []
    for g in range(ng):
        y_off.append(jnp.dot(cm[:, g * ns:(g + 1) * ns], state[:, g * w:(g + 1) * w].astype(BF16),
                             preferred_element_type=F32))
        b_t = lax.dot_general(eye.astype(BF16), bm[:, g * ns:(g + 1) * ns], nt, preferred_element_type=F32).astype(BF16)
        new_cols.append(jnp.dot(b_t, xdd[:, g * w:(g + 1) * w], preferred_element_type=F32))
    state_ref[...] = state * jnp.exp(a_end) + jnp.concatenate(new_cols, axis=1)
    y = ydiag_ref[...] + jnp.concatenate(y_off, axis=1) * jnp.exp(a_x) + dskip_ref[...] * xs

    z = z_ref[...]
    y = y * (z * jax.nn.sigmoid(z))
    gw = ds // ng
    for g in range(ng):
        yg = y[:, g * gw:(g + 1) * gw]
        yg = yg * lax.rsqrt(jnp.mean(yg * yg, axis=-1, keepdims=True) + cfg.eps)
        y_ref[:, g * gw:(g + 1) * gw] = (yg * ng_ref[:, g * gw:(g + 1) * gw]).astype(y_ref.dtype)


def _ssd(cfg, proj, t_seq, hist, state0_t, conv_w, conv_b, dt_bias, a_log, d_skip, norm_g):
    m = proj.shape[0]
    n_seq = m // t_seq
    lc = _tile(t_seq, cfg.ssd_chunk)
    nc = t_seq // lc
    ds, cd, ns = cfg.d_ssd, cfg.conv_dim, cfg.d_state
    assert cfg.off["z"] % ds == 0 and cfg.off["xbc"] % cd == 0
    zb, xb, db = cfg.off["z"] // ds, cfg.off["xbc"] // cd, cfg.off["dt"] // LANES
    pad = LANES - cfg.ssd_heads
    lane_row = lambda v: jnp.pad(v.astype(F32), (0, pad)).reshape(1, LANES)
    expand = (jnp.arange(LANES)[:, None] == (jnp.arange(ds) // cfg.ssd_head_dim)[None, :]).astype(F32)
    const = lambda shape: pl.BlockSpec(shape, lambda b, c: (0,) * len(shape))
    return pl.pallas_call(
        functools.partial(_ssd_body, cfg, lc),
        grid=(n_seq, nc),
        in_specs=[pl.BlockSpec((lc, ds), lambda b, c: (b * nc + c, zb)),
                  pl.BlockSpec((lc, cd), lambda b, c: (b * nc + c, xb)),
                  pl.BlockSpec((lc, LANES), lambda b, c: (b * nc + c, db)),
                  pl.BlockSpec((None, 3, cd), lambda b, c: (b, 0, 0)),
                  pl.BlockSpec((None, ns, ds), lambda b, c: (b, 0, 0)),
                  const((cfg.ssd_conv, cd)), const((1, cd)), const((1, LANES)), const((1, LANES)), const((1, ds)),
                  const((1, ds)), const((LANES, ds))],
        out_specs=[pl.BlockSpec((lc, ds), lambda b, c: (b * nc + c, 0)),
                   pl.BlockSpec((None, ns, ds), lambda b, c: (b, 0, 0)),
                   pl.BlockSpec((None, 3, cd), lambda b, c: (b, 0, 0))],
        out_shape=[jax.ShapeDtypeStruct((m, ds), BF16),
                   jax.ShapeDtypeStruct((n_seq, ns, ds), F32),
                   jax.ShapeDtypeStruct((n_seq, 3, cd), F32)],
        scratch_shapes=[pltpu.VMEM((8, cd), F32), pltpu.VMEM((lc, ds), F32)],
        compiler_params=_cparams(("parallel", "arbitrary")),
        name="ssd_scan",
    )(proj, proj, proj, hist, state0_t, conv_w, conv_b.reshape(1, cd), lane_row(dt_bias), lane_row(a_log),
      jnp.repeat(d_skip.astype(F32), cfg.ssd_head_dim).reshape(1, ds), norm_g.reshape(1, ds), expand)


NEAR_COLS = 640
NEAR_BACK = 512
TK_NEAR = 128
TK_IDX = 512


def _t5_bucket(cfg, rel):
    nb = cfg.n_buckets // 2
    max_exact = nb // 2
    ret = jnp.where(rel > 0, nb, 0)
    n = jnp.abs(rel)
    nf = jnp.maximum(n, 1).astype(F32)
    large = max_exact + (jnp.log(nf / max_exact) / math.log(cfg.max_distance / max_exact) * (nb - max_exact)).astype(I32)
    large = jnp.minimum(large, nb - 1)
    return ret + jnp.where(n < max_exact, n, large)


def _near_bias(cfg, rel_bias, tq):
    assert cfg.max_distance <= LANES
    rel = jnp.arange(NEAR_COLS, dtype=I32)[None, :] - NEAR_BACK - jnp.arange(tq, dtype=I32)[:, None]
    far = rel_bias[_t5_bucket(cfg, jnp.asarray(-cfg.max_distance, I32))].astype(F32)
    return jnp.transpose(rel_bias[_t5_bucket(cfg, rel)].astype(F32) - far, (2, 0, 1))


def _dsa_body(cfg, tq, past, n_select, q_ref, qi_ref, kiw_ref, k_ref, v_ref, kidx_ref, bias_ref, o_ref,
              keys_ref, qs_ref, qis_ref, wb_ref, m_ref, l_ref, acc_ref):
    nkv, hd, di, nih = cfg.n_kv, cfg.head_dim, cfg.idx_dim, cfg.n_idx_heads
    grp = cfg.n_heads // nkv
    nt = (((1,), (1,)), ((), ()))
    x0 = past + pl.program_id(1) * tq
    k_end = x0 + tq
    n_idx = (k_end + TK_IDX - 1) // TK_IDX

    q = q_ref[...] * (hd ** -0.5)
    for n in range(nkv):
        for g in range(grp):
            h = n * grp + g
            qs_ref[n, g * tq:(g + 1) * tq, :] = q[:, h * hd:(h + 1) * hd].astype(BF16)
    qi = qi_ref[...]
    kiw = kiw_ref[...]
    w = kiw[:, di:di + nih] * ((di ** -0.5) * (nih ** -0.5))
    for h in range(nih):
        qis_ref[h * tq:(h + 1) * tq, :] = qi[:, h * di:(h + 1) * di].astype(BF16)
        wb_ref[h] = jnp.broadcast_to(w[:, h:h + 1], (tq, LANES))

    row = lax.broadcasted_iota(I32, (tq, LANES), 0)
    limit = (jnp.right_shift(x0 + row, int(math.log2(cfg.chunk))) + 1) * cfg.chunk
    lane = lax.broadcasted_iota(I32, (tq, LANES), 1)

    def idx_tile(t, carry):
        k0 = pl.multiple_of(t * TK_IDX, TK_IDX)
        logits = lax.dot_general(qis_ref[...], kidx_ref[pl.ds(k0, TK_IDX), :], nt, preferred_element_type=F32)
        for cb in range(TK_IDX // LANES):
            sc = jnp.zeros((tq, LANES), F32)
            for h in range(nih):
                sc = sc + wb_ref[h] * jnp.maximum(logits[h * tq:(h + 1) * tq, cb * LANES:(cb + 1) * LANES], 0.0)
            bits = pltpu.bitcast(sc, I32)
            key = jnp.where(bits < 0, bits ^ 0x7FFFFFFF, bits)
            key = jnp.where(k0 + cb * LANES + lane < limit, key, INT_MIN)
            keys_ref[:, pl.ds(pl.multiple_of(k0 + cb * LANES, LANES), LANES)] = key
        return carry

    lax.fori_loop(0, n_idx, idx_tile, 0)

    def count_ge(cand):
        cand_b = jnp.broadcast_to(cand, (tq, LANES))

        def body(t, acc):
            k0 = pl.multiple_of(t * TK_IDX, TK_IDX)
            blk = keys_ref[:, pl.ds(k0, TK_IDX)]
            for cb in range(TK_IDX // LANES):
                acc = acc + jnp.where(blk[:, cb * LANES:(cb + 1) * LANES] >= cand_b, 1.0, 0.0)
            return acc

        acc = lax.fori_loop(0, n_idx, body, jnp.zeros((tq, LANES), F32))
        return jnp.sum(acc, axis=1, keepdims=True)

    def bit_step(s, thr):
        bit = 31 - s
        cand = jnp.where(bit == 31, jnp.zeros_like(thr), thr | jnp.left_shift(jnp.int32(1), bit))
        return jnp.where(count_ge(cand) >= float(n_select), cand, thr)

    thr = lax.fori_loop(0, 32, bit_step, jnp.full((tq, 1), INT_MIN, I32))
    thr = jnp.maximum(thr, INT_MIN + 1)

    far_end = jnp.maximum(x0 - LANES, 0) // cfg.tk_far * cfg.tk_far
    n_far = far_end // cfg.tk_far
    n_near = ((k_end + TK_NEAR - 1) // TK_NEAR * TK_NEAR - far_end) // TK_NEAR

    for n in range(nkv):
        m_ref[...] = jnp.full(m_ref.shape, NEG_BIG, F32)
        l_ref[...] = jnp.zeros(l_ref.shape, F32)
        acc_ref[...] = jnp.zeros(acc_ref.shape, F32)

        def tile(k0, width, bias):
            s = lax.dot_general(qs_ref[n], k_ref[pl.ds(k0, width), n * hd:(n + 1) * hd], nt,
                                preferred_element_type=F32)
            if bias is not None:
                s = s + bias
            sel = keys_ref[:, pl.ds(k0, width)] >= thr
            per_group = lambda f, a: jnp.concatenate([f(a[g * tq:(g + 1) * tq, :]) for g in range(grp)], axis=0)
            s = per_group(lambda a: jnp.where(sel, a, NEG_BIG), s)
            m_prev = m_ref[...]
            m_new = jnp.maximum(m_prev, jnp.max(s, axis=1, keepdims=True))
            alpha = jnp.exp(m_prev - m_new)
            p = per_group(lambda a: jnp.where(sel, a, 0.0), jnp.exp(s - m_new))
            l_ref[...] = alpha * l_ref[...] + jnp.sum(p, axis=1, keepdims=True)
            acc_ref[...] = alpha * acc_ref[...] + jnp.dot(p.astype(BF16), v_ref[pl.ds(k0, width), n * hd:(n + 1) * hd],
                                                          preferred_element_type=F32)
            m_ref[...] = m_new

        def far_tile(t, carry):
            tile(pl.multiple_of(t * cfg.tk_far, cfg.tk_far), cfg.tk_far, None)
            return carry

        def near_tile(u, carry):
            k0 = pl.multiple_of(far_end + u * TK_NEAR, TK_NEAR)
            off = pl.multiple_of(k0 - (x0 - NEAR_BACK), TK_NEAR)
            bias = jnp.concatenate([bias_ref[n * grp + g, :, pl.ds(off, TK_NEAR)] for g in range(grp)], axis=0)
            tile(k0, TK_NEAR, bias)
            return carry

        lax.fori_loop(0, n_far, far_tile, 0)
        lax.fori_loop(0, n_near, near_tile, 0)
        out = acc_ref[...] / l_ref[...]
        for g in range(grp):
            h = n * grp + g
            o_ref[:, h * hd:(h + 1) * hd] = out[g * tq:(g + 1) * tq, :].astype(o_ref.dtype)


def _dsa(cfg, proj, t_seq, past, k_all, v_all, kidx_all, rel_bias):
    m = proj.shape[0]
    n_seq = m // t_seq
    tq = _tile(t_seq, cfg.tq)
    nq = t_seq // tq
    n_keys = past + t_seq
    lp = k_all.shape[1]
    assert tq % cfg.chunk == 0 and past % LANES == 0 and lp % TK_IDX == 0 and lp >= n_keys
    assert cfg.tk_far % TK_NEAR == 0 and NEAR_BACK == cfg.tk_far and NEAR_COLS == NEAR_BACK + LANES and tq <= LANES
    n_select = min(cfg.top_k_max, n_keys // 4)
    grp = cfg.n_heads // cfg.n_kv
    hq, hidx = cfg.hq, cfg.hidx
    assert cfg.off["q"] % hq == 0 and cfg.off["qi"] % hidx == 0
    bias = _near_bias(cfg, rel_bias, tq)
    whole = lambda shape: pl.BlockSpec(shape, lambda b, i: (b,) + (0,) * (len(shape) - 1), pipeline_mode=pl.Buffered(1))
    return pl.pallas_call(
        functools.partial(_dsa_body, cfg, tq, past, n_select),
        grid=(n_seq, nq),
        in_specs=[pl.BlockSpec((tq, hq), lambda b, i: (b * nq + i, cfg.off["q"] // hq)),
                  pl.BlockSpec((tq, hidx), lambda b, i: (b * nq + i, cfg.off["qi"] // hidx)),
                  pl.BlockSpec((tq, LANES), lambda b, i: (b * nq + i, cfg.off["kiw"] // LANES)),
                  whole((None, lp, cfg.hkv)), whole((None, lp, cfg.hkv)), whole((None, lp, cfg.idx_dim)),
                  pl.BlockSpec(bias.shape, lambda b, i: (0, 0, 0), pipeline_mode=pl.Buffered(1))],
        out_specs=pl.BlockSpec((tq, hq), lambda b, i: (b * nq + i, 0)),
        out_shape=jax.ShapeDtypeStruct((m, hq), BF16),
        scratch_shapes=[pltpu.VMEM((tq, lp), I32),
                        pltpu.VMEM((cfg.n_kv, grp * tq, cfg.head_dim), BF16),
                        pltpu.VMEM((cfg.n_idx_heads * tq, cfg.idx_dim), BF16),
                        pltpu.VMEM((cfg.n_idx_heads, tq, LANES), F32),
                        pltpu.VMEM((grp * tq, 1), F32), pltpu.VMEM((grp * tq, 1), F32),
                        pltpu.VMEM((grp * tq, cfg.head_dim), F32)],
        compiler_params=_cparams(("parallel", "arbitrary")),
        name="dsa_attention",
    )(proj, proj, proj, k_all, v_all, kidx_all, bias)


def _pack_w_in(cfg, w_in):
    offs = np.cumsum(np.array(cfg.in_sizes))[:-1].tolist()
    q, k, v, qi, ki, wi, z, xbc, dt, ga, gs = jnp.split(w_in, offs, axis=-1)
    lane_pad = lambda a: jnp.pad(a, ((0, 0), (0, LANES - a.shape[1])))
    packed = jnp.concatenate([q, k, v, qi, z, xbc, ga, gs, lane_pad(jnp.concatenate([ki, wi], axis=1)), lane_pad(dt)],
                             axis=1)
    return jnp.pad(packed, ((0, 0), (0, cfg.n_packed - packed.shape[1]))).astype(BF16)


def _pad_keys(a, lp):
    return jnp.pad(a, ((0, 0), (0, lp - a.shape[1]), (0, 0))).astype(BF16)


def _trunk_layer(cfg, x, mod, past_k, past_v, past_ik, ssm0, ssd_conv0, ffn_conv0, rel_bias, wts):
    bsz, t, d = x.shape
    m = bsz * t
    past = past_k.shape[1]
    x2d = x.reshape(m, d)
    off = cfg.off

    u = _norm_mod(cfg, x2d, wts["norm_mix_g"], mod, t, 0, 1)
    proj = _matmul(u, wts["w_in"], cfg.tm, cfg.tn_in, F32, "in_proj")
    k_new = proj[:, off["k"]:off["k"] + cfg.hkv].reshape(bsz, t, cfg.hkv)
    v_new = proj[:, off["v"]:off["v"] + cfg.hkv].reshape(bsz, t, cfg.hkv)
    ki_new = proj[:, off["kiw"]:off["kiw"] + cfg.idx_dim].reshape(bsz, t, cfg.idx_dim)
    lp = -(-(past + t) // TK_IDX) * TK_IDX
    k_all = _pad_keys(jnp.concatenate([past_k.reshape(bsz, past, cfg.hkv), k_new], axis=1), lp)
    v_all = _pad_keys(jnp.concatenate([past_v.reshape(bsz, past, cfg.hkv), v_new], axis=1), lp)
    ki_all = _pad_keys(jnp.concatenate([past_ik, ki_new], axis=1), lp)
    attn = _dsa(cfg, proj, t, past, k_all, v_all, ki_all, rel_bias)

    state0_t = jnp.transpose(ssm0.astype(F32), (0, 3, 1, 2)).reshape(bsz, cfg.d_state, cfg.d_ssd)
    ssd_out, state_t, ssd_conv_new = _ssd(cfg, proj, t, ssd_conv0, state0_t, wts["ssd_conv_w"], wts["ssd_conv_b"],
                                          wts["dt_bias"], wts["a_log"], wts["d_skip"], wts["ssd_norm_g"])
    h_new = jnp.transpose(state_t.reshape(bsz, cfg.d_state, cfg.ssd_heads, cfg.ssd_head_dim), (0, 2, 3, 1))

    merged = _merge(cfg, attn, ssd_out, wts["w_attn_o"], wts["w_ssd_o"], proj)
    x1 = _gated_residual(merged, wts["w_out"], x2d, mod, t, 2, cfg.tm, cfg.tn, "out_proj_residual")

    u2 = _norm_mod(cfg, x1, wts["norm_ffn_g"], mod, t, 3, 4)
    h, ffn_conv_new = _ffn_up(cfg, u2, wts["w_up"], wts["ffn_conv_w"], wts["ffn_conv_b"], ffn_conv0, t)
    x2 = _gated_residual(h, wts["w_down"], x1, mod, t, 5, cfg.tm_down, cfg.tn_down, "down_proj_residual")
    states = (k_new.reshape(bsz, t, cfg.n_kv, cfg.head_dim), v_new.reshape(bsz, t, cfg.n_kv, cfg.head_dim), ki_new,
              h_new.astype(ssm0.dtype), ssd_conv_new, ffn_conv_new)
    return x2.reshape(bsz, t, d), states


def _forward(cfg, x_prompt, x_sample, c_prompt, c_sample, cache_k, cache_v, cache_idx_k, state_ssm, state_ssd_conv,
             state_ffn_conv, rel_bias, w_ada, b_ada, norm_mix_g, w_in, ssd_conv_w, ssd_conv_b, dt_bias, a_log, d_skip,
             ssd_norm_g, w_attn_o, w_ssd_o, w_out, norm_ffn_g, w_up, ffn_conv_w, ffn_conv_b, w_down, final_norm_g):
    depth = w_in.shape[0]
    bp, tp, d = x_prompt.shape
    bs, ts, _ = x_sample.shape
    dt_ = x_prompt.dtype
    hp, hs = x_prompt, x_sample
    c_all = jnp.concatenate([c_prompt, c_sample], axis=0)
    c_all = jnp.pad(c_all, ((0, -(bp + bs) % 8), (0, 0)))
    prompt_states, sample_states = [], []
    for l in range(depth):
        mod = _modulation(cfg, c_all, w_ada[l], b_ada[l]).reshape(c_all.shape[0], 6, d)
        wts = dict(norm_mix_g=norm_mix_g[l], w_in=_pack_w_in(cfg, w_in[l]), ssd_conv_w=ssd_conv_w[l],
                   ssd_conv_b=ssd_conv_b[l], dt_bias=dt_bias[l], a_log=a_log[l], d_skip=d_skip[l],
                   ssd_norm_g=ssd_norm_g[l], w_attn_o=w_attn_o[l].astype(BF16), w_ssd_o=w_ssd_o[l].astype(BF16),
                   w_out=w_out[l].astype(BF16), norm_ffn_g=norm_ffn_g[l], w_up=w_up[l].astype(BF16),
                   ffn_conv_w=ffn_conv_w[l], ffn_conv_b=ffn_conv_b[l], w_down=w_down[l].astype(BF16))
        hp, st_p = _trunk_layer(cfg, hp, mod[:bp],
                                jnp.zeros((bp, 0, cfg.n_kv, cfg.head_dim), dt_),
                                jnp.zeros((bp, 0, cfg.n_kv, cfg.head_dim), dt_),
                                jnp.zeros((bp, 0, cfg.idx_dim), dt_),
                                jnp.zeros((bp, cfg.ssd_heads, cfg.ssd_head_dim, cfg.d_state), state_ssm.dtype),
                                jnp.zeros((bp, cfg.ssd_conv - 1, cfg.conv_dim), dt_),
                                jnp.zeros((bp, cfg.ffn_conv - 1, 2 * cfg.d_ff), dt_),
                                rel_bias, wts)
        hs, st_s = _trunk_layer(cfg, hs, mod[bp:bp + bs], cache_k[l], cache_v[l], cache_idx_k[l], state_ssm[l],
                                state_ssd_conv[l], state_ffn_conv[l], rel_bias, wts)
        prompt_states.append(st_p)
        sample_states.append(st_s)
    y_prompt = _final_norm(cfg, hp.reshape(bp * tp, d), final_norm_g).reshape(bp, tp, d)
    y_sample = _final_norm(cfg, hs.reshape(bs * ts, d), final_norm_g).reshape(bs, ts, d)
    stack = lambda states, i: jnp.stack([s[i] for s in states], axis=0)
    return (y_prompt, y_sample) + tuple(stack(prompt_states, i) for i in range(6)) + tuple(
        stack(sample_states, i) for i in range(6))


def kernel(x_prompt, x_sample, c_prompt, c_sample, cache_k, cache_v, cache_idx_k, state_ssm, state_ssd_conv,
           state_ffn_conv, rel_bias, w_ada, b_ada, norm_mix_g, w_in, ssd_conv_w, ssd_conv_b, dt_bias, a_log, d_skip,
           ssd_norm_g, w_attn_o, w_ssd_o, w_out, norm_ffn_g, w_up, ffn_conv_w, ffn_conv_b, w_down, final_norm_g):
    return _forward(Cfg(), x_prompt, x_sample, c_prompt, c_sample, cache_k, cache_v, cache_idx_k, state_ssm,
                    state_ssd_conv, state_ffn_conv, rel_bias, w_ada, b_ada, norm_mix_g, w_in, ssd_conv_w, ssd_conv_b,
                    dt_bias, a_log, d_skip, ssd_norm_g, w_attn_o, w_ssd_o, w_out, norm_ffn_g, w_up, ffn_conv_w,
                    ffn_conv_b, w_down, final_norm_g)
```

```python
import functools
import math

import numpy as np
import jax
import jax.numpy as jnp
from jax import lax
from jax.experimental import pallas as pl
from jax.experimental.pallas import tpu as pltpu

F32 = jnp.float32
BF16 = jnp.bfloat16
I32 = jnp.int32

LANES = 128
V7X_VMEM_BYTES = 64 * 1024 * 1024
VMEM_LIMIT = V7X_VMEM_BYTES - 8 * 1024 * 1024
INT_MIN = -(2 ** 31)
NEG_BIG = -1e30
M_INIT = -1e29
LOG2E = math.log2(math.e)
HIGHEST = lax.Precision.HIGHEST


class Cfg:
    def __init__(self, **kw):
        self.d_model = 4096
        self.chunk = 64
        self.n_heads = 16
        self.n_kv = 4
        self.head_dim = 128
        self.n_idx_heads = 16
        self.idx_dim = 64
        self.top_k_max = 256
        self.n_buckets = 32
        self.max_distance = 128
        self.ssd_heads = 32
        self.ssd_head_dim = 64
        self.ssd_groups = 4
        self.d_state = 128
        self.ssd_conv = 4
        self.d_ff = 11008
        self.ffn_conv = 3
        self.eps = 1e-6
        self.tm = 1024
        self.tn_in = 768
        self.tn = 512
        self.tn_ff = 256
        self.tm_down = 512
        self.tn_down = 256
        self.tr = 256
        self.tq = 128
        self.tk_far = 512
        self.ssd_chunk = 128
        self.tn_mod = 512
        for k, v in kw.items():
            assert hasattr(self, k), k
            setattr(self, k, v)
        self.d_ssd = self.ssd_heads * self.ssd_head_dim
        self.conv_dim = self.d_ssd + 2 * self.ssd_groups * self.d_state
        self.hq = self.n_heads * self.head_dim
        self.hkv = self.n_kv * self.head_dim
        self.hidx = self.n_idx_heads * self.idx_dim
        self.in_sizes = (self.hq, self.hkv, self.hkv, self.hidx, self.idx_dim, self.n_idx_heads, self.d_ssd,
                         self.conv_dim, self.ssd_heads, self.d_model, self.d_model)
        segs = [("q", self.hq), ("k", self.hkv), ("v", self.hkv), ("qi", self.hidx), ("z", self.d_ssd),
                ("xbc", self.conv_dim), ("ga", self.d_model), ("gs", self.d_model), ("kiw", LANES), ("dt", LANES)]
        off, self.off = 0, {}
        for name, width in segs:
            assert width % LANES == 0
            self.off[name] = off
            off += width
        self.n_packed = -(-off // self.tn_in) * self.tn_in
        assert self.idx_dim + self.n_idx_heads <= LANES and self.ssd_heads <= LANES


def _cparams(sem):
    return pltpu.CompilerParams(dimension_semantics=sem, vmem_limit_bytes=VMEM_LIMIT)


def _tile(n, pref):
    t = min(n, pref)
    assert n % t == 0, (n, pref)
    return t


def _seq_map(t_seq, tm):
    if t_seq >= tm:
        assert t_seq % tm == 0
        per = t_seq // tm
        return tm, 1, (lambda i: i // per)
    assert tm % t_seq == 0
    return t_seq, tm // t_seq, (lambda i: i)


def _mod_body(c_ref, w_ref, b_ref, o_ref):
    c = c_ref[...]
    a = (c * jax.nn.sigmoid(c)).astype(BF16)
    o_ref[...] = jnp.dot(a, w_ref[...].astype(BF16), preferred_element_type=F32) + b_ref[...]


def _modulation(cfg, c, w_ada, b_ada):
    rows, d = c.shape
    n = w_ada.shape[1]
    tn = _tile(n, cfg.tn_mod)
    return pl.pallas_call(
        _mod_body,
        grid=(n // tn,),
        in_specs=[pl.BlockSpec((rows, d), lambda j: (0, 0)),
                  pl.BlockSpec((d, tn), lambda j: (0, j)),
                  pl.BlockSpec((1, tn), lambda j: (0, j))],
        out_specs=pl.BlockSpec((rows, tn), lambda j: (0, j)),
        out_shape=jax.ShapeDtypeStruct((rows, n), F32),
        compiler_params=_cparams(("parallel",)),
        name="adaln_mod",
    )(c, w_ada, b_ada.reshape(1, n))


def _norm_mod_body(eps, sh_row, sc_row, x_ref, g_ref, mod_ref, o_ref):
    x = x_ref[...]
    y = x * lax.rsqrt(jnp.mean(x * x, axis=-1, keepdims=True) + eps) * g_ref[...]
    y = y * (1.0 + mod_ref[sc_row:sc_row + 1, :]) + mod_ref[sh_row:sh_row + 1, :]
    o_ref[...] = y.astype(o_ref.dtype)


def _norm_mod(cfg, x, g, mod, t_seq, sh_row, sc_row):
    m, d = x.shape
    tr = _tile(t_seq, cfg.tr)
    per = t_seq // tr
    return pl.pallas_call(
        functools.partial(_norm_mod_body, cfg.eps, sh_row, sc_row),
        grid=(m // tr,),
        in_specs=[pl.BlockSpec((tr, d), lambda i: (i, 0)),
                  pl.BlockSpec((1, d), lambda i: (0, 0)),
                  pl.BlockSpec((None, 6, d), lambda i: (i // per, 0, 0))],
        out_specs=pl.BlockSpec((tr, d), lambda i: (i, 0)),
        out_shape=jax.ShapeDtypeStruct((m, d), BF16),
        compiler_params=_cparams(("parallel",)),
        name="rmsnorm_mod",
    )(x, g.reshape(1, d), mod)


def _norm_body(eps, x_ref, g_ref, o_ref):
    x = x_ref[...]
    o_ref[...] = x * lax.rsqrt(jnp.mean(x * x, axis=-1, keepdims=True) + eps) * g_ref[...]


def _final_norm(cfg, x, g):
    m, d = x.shape
    tr = _tile(m, cfg.tr)
    return pl.pallas_call(
        functools.partial(_norm_body, cfg.eps),
        grid=(m // tr,),
        in_specs=[pl.BlockSpec((tr, d), lambda i: (i, 0)), pl.BlockSpec((1, d), lambda i: (0, 0))],
        out_specs=pl.BlockSpec((tr, d), lambda i: (i, 0)),
        out_shape=jax.ShapeDtypeStruct((m, d), F32),
        compiler_params=_cparams(("parallel",)),
        name="final_rmsnorm",
    )(x, g.reshape(1, d))


def _mm_body(a_ref, w_ref, o_ref):
    o_ref[...] = jnp.dot(a_ref[...], w_ref[...], preferred_element_type=F32).astype(o_ref.dtype)


def _matmul(a, w, tm, tn, out_dtype, name):
    m, k = a.shape
    n = w.shape[1]
    tm, tn = _tile(m, tm), _tile(n, tn)
    return pl.pallas_call(
        _mm_body,
        grid=(m // tm, n // tn),
        in_specs=[pl.BlockSpec((tm, k), lambda i, j: (i, 0)), pl.BlockSpec((k, tn), lambda i, j: (0, j))],
        out_specs=pl.BlockSpec((tm, tn), lambda i, j: (i, j)),
        out_shape=jax.ShapeDtypeStruct((m, n), out_dtype),
        compiler_params=_cparams(("parallel", "parallel")),
        name=name,
    )(a, w)


def _merge_body(attn_ref, ssd_ref, wa_ref, ws_ref, ga_ref, gs_ref, o_ref):
    a = jnp.dot(attn_ref[...], wa_ref[...], preferred_element_type=F32)
    s = jnp.dot(ssd_ref[...], ws_ref[...], preferred_element_type=F32)
    o_ref[...] = (jax.nn.sigmoid(ga_ref[...]) * a + jax.nn.sigmoid(gs_ref[...]) * s).astype(o_ref.dtype)


def _merge(cfg, attn, ssd, w_attn_o, w_ssd_o, proj):
    m = attn.shape[0]
    d = cfg.d_model
    tm = _tile(m, cfg.tm)
    tn = math.gcd(math.gcd(cfg.off["ga"], cfg.off["gs"]), _tile(d, cfg.tn))
    ga0, gs0 = cfg.off["ga"] // tn, cfg.off["gs"] // tn
    return pl.pallas_call(
        _merge_body,
        grid=(m // tm, d // tn),
        in_specs=[pl.BlockSpec((tm, attn.shape[1]), lambda i, j: (i, 0)),
                  pl.BlockSpec((tm, ssd.shape[1]), lambda i, j: (i, 0)),
                  pl.BlockSpec((attn.shape[1], tn), lambda i, j: (0, j)),
                  pl.BlockSpec((ssd.shape[1], tn), lambda i, j: (0, j)),
                  pl.BlockSpec((tm, tn), lambda i, j: (i, ga0 + j)),
                  pl.BlockSpec((tm, tn), lambda i, j: (i, gs0 + j))],
        out_specs=pl.BlockSpec((tm, tn), lambda i, j: (i, j)),
        out_shape=jax.ShapeDtypeStruct((m, d), BF16),
        compiler_params=_cparams(("parallel", "parallel")),
        name="branch_merge",
    )(attn, ssd, w_attn_o, w_ssd_o, proj, proj)


def _resid_body(rows, groups, gate_row, a_ref, w_ref, x_ref, mod_ref, o_ref):
    acc = jnp.dot(a_ref[...], w_ref[...], preferred_element_type=F32)
    for g in range(groups):
        sl = slice(g * rows, (g + 1) * rows)
        o_ref[sl, :] = x_ref[sl, :] + mod_ref[g, gate_row:gate_row + 1, :] * acc[sl, :]


def _gated_residual(a, w, x, mod, t_seq, gate_row, tm, tn, name):
    m, k = a.shape
    n = w.shape[1]
    tm, tn = _tile(m, tm), _tile(n, tn)
    rows, groups, seq_of = _seq_map(t_seq, tm)
    return pl.pallas_call(
        functools.partial(_resid_body, rows, groups, gate_row),
        grid=(m // tm, n // tn),
        in_specs=[pl.BlockSpec((tm, k), lambda i, j: (i, 0)),
                  pl.BlockSpec((k, tn), lambda i, j: (0, j)),
                  pl.BlockSpec((tm, tn), lambda i, j: (i, j)),
                  pl.BlockSpec((groups, 6, tn), lambda i, j: (seq_of(i), 0, j))],
        out_specs=pl.BlockSpec((tm, tn), lambda i, j: (i, j)),
        out_shape=jax.ShapeDtypeStruct((m, n), F32),
        compiler_params=_cparams(("parallel", "parallel")),
        name=name,
    )(a, w, x, mod)


def _up_body(rows, groups, tiles_per_seq, a_ref, wg_ref, wv_ref, cwg_ref, cwv_ref, cbg_ref, cbv_ref, hg_ref, hv_ref,
             h_ref, ng_ref, nv_ref, carry_ref):
    i, j = pl.program_id(0), pl.program_id(1)
    a = a_ref[...]
    rid = lax.broadcasted_iota(I32, (rows, wg_ref.shape[1]), 0)
    halves = []
    for half, (w_ref, cw_ref, cb_ref, hist_ref, new_ref) in enumerate(
            ((wg_ref, cwg_ref, cbg_ref, hg_ref, ng_ref), (wv_ref, cwv_ref, cbv_ref, hv_ref, nv_ref))):
        up = jnp.dot(a, w_ref[...], preferred_element_type=F32)
        outs = []
        for g in range(groups):
            u = up[g * rows:(g + 1) * rows, :]
            prev = hist_ref[g]
            if groups == 1 and tiles_per_seq > 1:
                prev = jnp.where(i % tiles_per_seq == 0, prev, carry_ref[j, half])
            s1 = jnp.where(rid == 0, prev[1:2, :], pltpu.roll(u, 1, 0))
            s2 = jnp.where(rid == 0, prev[0:1, :], jnp.where(rid == 1, prev[1:2, :], pltpu.roll(u, 2, 0)))
            outs.append(cw_ref[0:1, :] * s2 + cw_ref[1:2, :] * s1 + cw_ref[2:3, :] * u + cb_ref[...])
            last = u[rows - 2:rows, :]
            new_ref[g] = last
            if groups == 1 and tiles_per_seq > 1:
                carry_ref[j, half] = last
        halves.append(outs)
    for g in range(groups):
        gate, val = halves[0][g], halves[1][g]
        h_ref[g * rows:(g + 1) * rows, :] = (gate * jax.nn.sigmoid(gate) * val).astype(h_ref.dtype)


def _ffn_up(cfg, u2, w_up, conv_w, conv_b, hist, t_seq):
    m, d = u2.shape
    nf = cfg.d_ff
    n_seq = m // t_seq
    tm, tn = _tile(m, cfg.tm), _tile(nf, cfg.tn_ff)
    rows, groups, seq_of = _seq_map(t_seq, tm)
    nj = nf // tn
    wspec = lambda half: pl.BlockSpec((d, tn), lambda i, j: (0, j + half * nj))
    cspec = lambda r, half: pl.BlockSpec((r, tn), lambda i, j: (0, j + half * nj))
    hspec = lambda half: pl.BlockSpec((groups, 2, tn), lambda i, j: (seq_of(i), 0, j + half * nj))
    ospec = pl.BlockSpec((groups, 2, tn), lambda i, j: (i, 0, j))
    h, new_g, new_v = pl.pallas_call(
        functools.partial(_up_body, rows, groups, max(1, t_seq // tm)),
        grid=(m // tm, nj),
        in_specs=[pl.BlockSpec((tm, d), lambda i, j: (i, 0)), wspec(0), wspec(1),
                  cspec(3, 0), cspec(3, 1), cspec(1, 0), cspec(1, 1), hspec(0), hspec(1)],
        out_specs=[pl.BlockSpec((tm, tn), lambda i, j: (i, j)), ospec, ospec],
        out_shape=[jax.ShapeDtypeStruct((m, nf), BF16),
                   jax.ShapeDtypeStruct((m // rows, 2, nf), F32),
                   jax.ShapeDtypeStruct((m // rows, 2, nf), F32)],
        scratch_shapes=[pltpu.VMEM((nj, 2, 2, tn), F32)],
        compiler_params=_cparams(("arbitrary", "arbitrary")),
        name="ffn_up_conv_swiglu",
    )(u2, w_up, w_up, conv_w, conv_w, conv_b.reshape(1, -1), conv_b.reshape(1, -1), hist, hist)
    per_seq = t_seq // rows
    return h, jnp.concatenate([new_g[per_seq - 1::per_seq], new_v[per_seq - 1::per_seq]], axis=-1)


def _ssd_body(cfg, lc, z_ref, xbc_ref, dt_ref, hist_ref, s0_ref, cw_ref, cb_ref, dtb_ref, alog_ref, dskip_ref,
              ng_ref, expand_ref, y_ref, state_ref, cnew_ref, xcarry_ref, ydiag_ref):
    nh, hp, ng, ns = cfg.ssd_heads, cfg.ssd_head_dim, cfg.ssd_groups, cfg.d_state
    ds = cfg.d_ssd
    per_group = nh // ng
    c = pl.program_id(1)

    @pl.when(c == 0)
    def _():
        state_ref[...] = s0_ref[...]
        xcarry_ref[0:3, :] = hist_ref[...]

    x = xbc_ref[...]
    prev = xcarry_ref[0:3, :]
    rid = lax.broadcasted_iota(I32, x.shape, 0)
    s1 = jnp.where(rid == 0, prev[2:3, :], pltpu.roll(x, 1, 0))
    s2 = jnp.where(rid == 0, prev[1:2, :], jnp.where(rid == 1, prev[2:3, :], pltpu.roll(x, 2, 0)))
    s3 = jnp.where(rid == 0, prev[0:1, :],
                   jnp.where(rid == 1, prev[1:2, :], jnp.where(rid == 2, prev[2:3, :], pltpu.roll(x, 3, 0))))
    xc = cw_ref[0:1, :] * s3 + cw_ref[1:2, :] * s2 + cw_ref[2:3, :] * s1 + cw_ref[3:4, :] * x + cb_ref[...]
    xc = xc * jax.nn.sigmoid(xc)
    last = x[lc - 3:lc, :]
    xcarry_ref[0:3, :] = last
    cnew_ref[...] = last
    xs = xc[:, :ds]
    bm = xc[:, ds:ds + ng * ns].astype(BF16)
    cm = xc[:, ds + ng * ns:].astype(BF16)

    raw = dt_ref[...] + dtb_ref[...]
    dt = jnp.maximum(raw, 0.0) + jnp.log1p(jnp.exp(-jnp.abs(raw)))
    a = dt * (-jnp.exp(alog_ref[...]))
    ri = lax.broadcasted_iota(I32, (lc, lc), 0)
    ci = lax.broadcasted_iota(I32, (lc, lc), 1)
    causal = ri >= ci
    a_cum = jnp.dot(causal.astype(F32), a, precision=HIGHEST, preferred_element_type=F32)
    eye = (lax.broadcasted_iota(I32, (LANES, LANES), 0) == lax.broadcasted_iota(I32, (LANES, LANES), 1))
    nt = (((1,), (1,)), ((), ()))
    a_cum_t = lax.dot_general(eye.astype(F32), a_cum, nt, precision=HIGHEST, preferred_element_type=F32)
    expand = expand_ref[...]
    dt_x = jnp.dot(dt, expand, precision=HIGHEST, preferred_element_type=F32)
    a_x = jnp.dot(a_cum, expand, precision=HIGHEST, preferred_element_type=F32)
    a_end = a_x[lc - 1:lc, :]
    xd = xs * dt_x

    for g in range(ng):
        cb = lax.dot_general(cm[:, g * ns:(g + 1) * ns], bm[:, g * ns:(g + 1) * ns], nt, preferred_element_type=F32)
        for r in range(per_group):
            h = g * per_group + r
            seg = a_cum[:, h:h + 1] - a_cum_t[h:h + 1, :]
            lmat = jnp.exp(jnp.where(causal, seg, -jnp.inf))
            ydiag_ref[:, h * hp:(h + 1) * hp] = jnp.dot((cb * lmat).astype(BF16), xd[:, h * hp:(h + 1) * hp].astype(BF16),
                                                        preferred_element_type=F32)

    state = state_ref[...]
    xdd = (xd * jnp.exp(a_end - a_x)).astype(BF16)
    w = per_group * hp
    y_off, new_cols = [], []
    for g in range(ng):
        y_off.append(jnp.dot(cm[:, g * ns:(g + 1) * ns], state[:, g * w:(g + 1) * w].astype(BF16),
                             preferred_element_type=F32))
        b_t = lax.dot_general(eye.astype(BF16), bm[:, g * ns:(g + 1) * ns], nt, preferred_element_type=F32).astype(BF16)
        new_cols.append(jnp.dot(b_t, xdd[:, g * w:(g + 1) * w], preferred_element_type=F32))
    state_ref[...] = state * jnp.exp(a_end) + jnp.concatenate(new_cols, axis=1)
    y = ydiag_ref[...] + jnp.concatenate(y_off, axis=1) * jnp.exp(a_x) + dskip_ref[...] * xs

    z = z_ref[...]
    y = y * (z * jax.nn.sigmoid(z))
    gw = ds // ng
    for g in range(ng):
        yg = y[:, g * gw:(g + 1) * gw]
        yg = yg * lax.rsqrt(jnp.mean(yg * yg, axis=-1, keepdims=True) + cfg.eps)
        y_ref[:, g * gw:(g + 1) * gw] = (yg * ng_ref[:, g * gw:(g + 1) * gw]).astype(y_ref.dtype)


def _ssd(cfg, proj, t_seq, hist, state0_t, conv_w, conv_b, dt_bias, a_log, d_skip, norm_g):
    m = proj.shape[0]
    n_seq = m // t_seq
    lc = _tile(t_seq, cfg.ssd_chunk)
    nc = t_seq // lc
    ds, cd, ns = cfg.d_ssd, cfg.conv_dim, cfg.d_state
    assert cfg.off["z"] % ds == 0 and cfg.off["xbc"] % cd == 0
    zb, xb, db = cfg.off["z"] // ds, cfg.off["xbc"] // cd, cfg.off["dt"] // LANES
    pad = LANES - cfg.ssd_heads
    lane_row = lambda v: jnp.pad(v.astype(F32), (0, pad)).reshape(1, LANES)
    expand = (jnp.arange(LANES)[:, None] == (jnp.arange(ds) // cfg.ssd_head_dim)[None, :]).astype(F32)
    const = lambda shape: pl.BlockSpec(shape, lambda b, c: (0,) * len(shape))
    return pl.pallas_call(
        functools.partial(_ssd_body, cfg, lc),
        grid=(n_seq, nc),
        in_specs=[pl.BlockSpec((lc, ds), lambda b, c: (b * nc + c, zb)),
                  pl.BlockSpec((lc, cd), lambda b, c: (b * nc + c, xb)),
                  pl.BlockSpec((lc, LANES), lambda b, c: (b * nc + c, db)),
                  pl.BlockSpec((None, 3, cd), lambda b, c: (b, 0, 0)),
                  pl.BlockSpec((None, ns, ds), lambda b, c: (b, 0, 0)),
                  const((cfg.ssd_conv, cd)), const((1, cd)), const((1, LANES)), const((1, LANES)), const((1, ds)),
                  const((1, ds)), const((LANES, ds))],
        out_specs=[pl.BlockSpec((lc, ds), lambda b, c: (b * nc + c, 0)),
                   pl.BlockSpec((None, ns, ds), lambda b, c: (b, 0, 0)),
                   pl.BlockSpec((None, 3, cd), lambda b, c: (b, 0, 0))],
        out_shape=[jax.ShapeDtypeStruct((m, ds), BF16),
                   jax.ShapeDtypeStruct((n_seq, ns, ds), F32),
                   jax.ShapeDtypeStruct((n_seq, 3, cd), F32)],
        scratch_shapes=[pltpu.VMEM((8, cd), F32), pltpu.VMEM((lc, ds), F32)],
        compiler_params=_cparams(("parallel", "arbitrary")),
        name="ssd_scan",
    )(proj, proj, proj, hist, state0_t, conv_w, conv_b.reshape(1, cd), lane_row(dt_bias), lane_row(a_log),
      jnp.repeat(d_skip.astype(F32), cfg.ssd_head_dim).reshape(1, ds), norm_g.reshape(1, ds), expand)


NEAR_COLS = 640
NEAR_BACK = 512
TK_NEAR = 128
TK_IDX = 512


def _t5_bucket(cfg, rel):
    nb = cfg.n_buckets // 2
    max_exact = nb // 2
    ret = jnp.where(rel > 0, nb, 0)
    n = jnp.abs(rel)
    nf = jnp.maximum(n, 1).astype(F32)
    large = max_exact + (jnp.log(nf / max_exact) / math.log(cfg.max_distance / max_exact) * (nb - max_exact)).astype(I32)
    large = jnp.minimum(large, nb - 1)
    return ret + jnp.where(n < max_exact, n, large)


def _near_bias(cfg, rel_bias, tq):
    assert cfg.max_distance <= LANES
    rel = jnp.arange(NEAR_COLS, dtype=I32)[:, None] - NEAR_BACK - jnp.arange(tq, dtype=I32)[None, :]
    far = rel_bias[_t5_bucket(cfg, jnp.asarray(-cfg.max_distance, I32))].astype(F32)
    tab = (rel_bias[_t5_bucket(cfg, rel)].astype(F32) - far) * LOG2E
    grp = cfg.n_heads // cfg.n_kv
    tab = tab.reshape(NEAR_COLS, tq, cfg.n_kv, grp)
    return jnp.transpose(tab, (2, 0, 3, 1)).reshape(cfg.n_kv, NEAR_COLS, grp * tq)


def _dsa_body(cfg, tq, past, n_select, q_ref, qi_ref, kiw_ref, k_ref, v_ref, kidx_ref, bias_ref, o_ref,
              keys_ref, qs_ref, qis_ref, w_ref, m_ref, l_ref, acc_ref, s_ref):
    nkv, hd, di, nih = cfg.n_kv, cfg.head_dim, cfg.idx_dim, cfg.n_idx_heads
    grp = cfg.n_heads // nkv
    gw = grp * tq
    nt = (((1,), (1,)), ((), ()))
    tn = (((0,), (0,)), ((), ()))
    eye = lax.broadcasted_iota(I32, (LANES, LANES), 0) == lax.broadcasted_iota(I32, (LANES, LANES), 1)
    eye_bf = eye.astype(BF16)
    x0 = past + pl.program_id(1) * tq
    k_end = x0 + tq
    n_idx = (k_end + TK_IDX - 1) // TK_IDX

    q = (q_ref[...] * ((hd ** -0.5) * LOG2E)).astype(BF16)
    for n in range(nkv):
        for g in range(grp):
            h = n * grp + g
            qs_ref[n, :, g * tq:(g + 1) * tq] = lax.dot_general(
                eye_bf, q[:, h * hd:(h + 1) * hd], nt, preferred_element_type=F32).astype(BF16)
    qi = qi_ref[...].astype(BF16)
    per_blk = LANES // di
    for j in range(nih // per_blk):
        t_blk = lax.dot_general(eye_bf, qi[:, j * LANES:(j + 1) * LANES], nt, preferred_element_type=F32)
        for r in range(per_blk):
            h = j * per_blk + r
            qis_ref[:, h * tq:(h + 1) * tq] = t_blk[r * di:(r + 1) * di, :].astype(BF16)
    kiw_t = lax.dot_general(eye.astype(F32), kiw_ref[...], nt, precision=HIGHEST, preferred_element_type=F32)
    w_ref[...] = kiw_t[di:di + nih, :] * ((di ** -0.5) * (nih ** -0.5))

    qpos = x0 + lax.broadcasted_iota(I32, (1, tq), 1)
    limit = (jnp.right_shift(qpos, int(math.log2(cfg.chunk))) + 1) * cfg.chunk
    krow = lax.broadcasted_iota(I32, (TK_IDX, tq), 0)

    def idx_tile(t, carry):
        k0 = pl.multiple_of(t * TK_IDX, TK_IDX)
        logits = jnp.dot(kidx_ref[pl.ds(k0, TK_IDX), :], qis_ref[...], preferred_element_type=F32)
        sc = jnp.zeros((TK_IDX, tq), F32)
        for h in range(nih):
            sc = sc + w_ref[h:h + 1, :] * jnp.maximum(logits[:, h * tq:(h + 1) * tq], 0.0)
        bits = pltpu.bitcast(sc, I32)
        key = jnp.where(bits < 0, bits ^ 0x7FFFFFFF, bits)
        keys_ref[pl.ds(k0, TK_IDX), :] = jnp.where(k0 + krow < limit, key, INT_MIN)
        return carry

    lax.fori_loop(0, n_idx, idx_tile, 0)

    def count_ge(cand):
        cand8 = jnp.broadcast_to(cand, (8, tq))

        def body(t, accs):
            k0 = pl.multiple_of(t * TK_IDX, TK_IDX)
            blk = keys_ref[pl.ds(k0, TK_IDX), :]
            accs = list(accs)
            for r in range(TK_IDX // 8):
                accs[r % len(accs)] = accs[r % len(accs)] + jnp.where(blk[r * 8:(r + 1) * 8, :] >= cand8, 1.0, 0.0)
            return tuple(accs)

        accs = lax.fori_loop(0, n_idx, body, (jnp.zeros((8, tq), F32),) * 8)
        return jnp.sum(functools.reduce(lambda a, b: a + b, accs), axis=0, keepdims=True)

    def bit_step(s, thr):
        bit = 31 - s
        cand = jnp.where(bit == 31, jnp.zeros_like(thr), thr | jnp.left_shift(jnp.int32(1), bit))
        return jnp.where(count_ge(cand) >= float(n_select), cand, thr)

    thr = lax.fori_loop(0, 32, bit_step, jnp.full((1, tq), INT_MIN, I32))
    thr = jnp.maximum(thr, INT_MIN + 1)

    far_end = jnp.maximum(x0 - LANES, 0) // cfg.tk_far * cfg.tk_far
    n_far = far_end // cfg.tk_far
    n_near = ((k_end + TK_NEAR - 1) // TK_NEAR * TK_NEAR - far_end) // TK_NEAR
    m_ref[...] = jnp.full(m_ref.shape, M_INIT, F32)
    l_ref[...] = jnp.zeros(l_ref.shape, F32)
    acc_ref[...] = jnp.zeros(acc_ref.shape, F32)

    def tile(k0, width, bias_off):
        sel = keys_ref[pl.ds(k0, width), :] >= thr

        def logits(n):
            s = jnp.dot(k_ref[pl.ds(k0, width), n * hd:(n + 1) * hd], qs_ref[n], preferred_element_type=F32)
            if bias_off is not None:
                s = s + bias_ref[n, pl.ds(bias_off, width), :]
            for g in range(grp):
                s_ref[n % 2, 0:width, g * tq:(g + 1) * tq] = jnp.where(sel, s[:, g * tq:(g + 1) * tq], NEG_BIG)

        logits(0)
        for n in range(nkv):
            if n + 1 < nkv:
                logits(n + 1)
            s = s_ref[n % 2, 0:width, :]
            m_prev = m_ref[n]
            m_new = jnp.maximum(m_prev, jnp.max(s, axis=0, keepdims=True))
            alpha = jnp.exp2(m_prev - m_new)
            p = jnp.exp2(s - m_new)
            l_ref[n] = alpha * l_ref[n] + jnp.sum(p, axis=0, keepdims=True)
            acc_ref[n] = alpha * acc_ref[n] + lax.dot_general(
                v_ref[pl.ds(k0, width), n * hd:(n + 1) * hd], p.astype(BF16), tn, preferred_element_type=F32)
            m_ref[n] = m_new

    def far_tile(t, carry):
        tile(pl.multiple_of(t * cfg.tk_far, cfg.tk_far), cfg.tk_far, None)
        return carry

    def near_tile(u, carry):
        k0 = pl.multiple_of(far_end + u * TK_NEAR, TK_NEAR)
        tile(k0, TK_NEAR, pl.multiple_of(k0 - (x0 - NEAR_BACK), TK_NEAR))
        return carry

    lax.fori_loop(0, n_far, far_tile, 0)
    lax.fori_loop(0, n_near, near_tile, 0)
    for n in range(nkv):
        out_t = (acc_ref[n] / l_ref[n]).astype(BF16)
        for g in range(grp):
            h = n * grp + g
            o_ref[:, h * hd:(h + 1) * hd] = lax.dot_general(
                out_t[:, g * tq:(g + 1) * tq], eye_bf, tn, preferred_element_type=F32).astype(o_ref.dtype)


def _dsa(cfg, proj, t_seq, past, k_all, v_all, kidx_all, rel_bias):
    m = proj.shape[0]
    n_seq = m // t_seq
    tq = _tile(t_seq, cfg.tq)
    nq = t_seq // tq
    n_keys = past + t_seq
    lp = k_all.shape[1]
    assert tq % cfg.chunk == 0 and past % LANES == 0 and lp % TK_IDX == 0 and lp >= n_keys
    assert cfg.tk_far % TK_NEAR == 0 and NEAR_BACK == cfg.tk_far and NEAR_COLS == NEAR_BACK + LANES and tq <= LANES
    n_select = min(cfg.top_k_max, n_keys // 4)
    grp = cfg.n_heads // cfg.n_kv
    hq, hidx = cfg.hq, cfg.hidx
    assert cfg.off["q"] % hq == 0 and cfg.off["qi"] % hidx == 0
    assert cfg.head_dim == LANES and LANES % cfg.idx_dim == 0 and cfg.n_idx_heads % (LANES // cfg.idx_dim) == 0
    bias = _near_bias(cfg, rel_bias, tq)
    whole = lambda shape: pl.BlockSpec(shape, lambda b, i: (b,) + (0,) * (len(shape) - 1), pipeline_mode=pl.Buffered(1))
    return pl.pallas_call(
        functools.partial(_dsa_body, cfg, tq, past, n_select),
        grid=(n_seq, nq),
        in_specs=[pl.BlockSpec((tq, hq), lambda b, i: (b * nq + i, cfg.off["q"] // hq)),
                  pl.BlockSpec((tq, hidx), lambda b, i: (b * nq + i, cfg.off["qi"] // hidx)),
                  pl.BlockSpec((tq, LANES), lambda b, i: (b * nq + i, cfg.off["kiw"] // LANES)),
                  whole((None, lp, cfg.hkv)), whole((None, lp, cfg.hkv)), whole((None, lp, cfg.idx_dim)),
                  pl.BlockSpec(bias.shape, lambda b, i: (0, 0, 0), pipeline_mode=pl.Buffered(1))],
        out_specs=pl.BlockSpec((tq, hq), lambda b, i: (b * nq + i, 0)),
        out_shape=jax.ShapeDtypeStruct((m, hq), BF16),
        scratch_shapes=[pltpu.VMEM((lp, tq), I32),
                        pltpu.VMEM((cfg.n_kv, cfg.head_dim, grp * tq), BF16),
                        pltpu.VMEM((cfg.idx_dim, cfg.n_idx_heads * tq), BF16),
                        pltpu.VMEM((cfg.n_idx_heads, tq), F32),
                        pltpu.VMEM((cfg.n_kv, 1, grp * tq), F32), pltpu.VMEM((cfg.n_kv, 1, grp * tq), F32),
                        pltpu.VMEM((cfg.n_kv, cfg.head_dim, grp * tq), F32),
                        pltpu.VMEM((2, cfg.tk_far, grp * tq), F32)],
        compiler_params=_cparams(("parallel", "arbitrary")),
        name="dsa_attention",
    )(proj, proj, proj, k_all, v_all, kidx_all, bias)


def _pack_w_in(cfg, w_in):
    offs = np.cumsum(np.array(cfg.in_sizes))[:-1].tolist()
    q, k, v, qi, ki, wi, z, xbc, dt, ga, gs = jnp.split(w_in, offs, axis=-1)
    lane_pad = lambda a: jnp.pad(a, ((0, 0), (0, LANES - a.shape[1])))
    packed = jnp.concatenate([q, k, v, qi, z, xbc, ga, gs, lane_pad(jnp.concatenate([ki, wi], axis=1)), lane_pad(dt)],
                             axis=1)
    return jnp.pad(packed, ((0, 0), (0, cfg.n_packed - packed.shape[1]))).astype(BF16)


def _pad_keys(a, lp):
    return jnp.pad(a, ((0, 0), (0, lp - a.shape[1]), (0, 0))).astype(BF16)


def _trunk_layer(cfg, x, mod, past_k, past_v, past_ik, ssm0, ssd_conv0, ffn_conv0, rel_bias, wts):
    bsz, t, d = x.shape
    m = bsz * t
    past = past_k.shape[1]
    x2d = x.reshape(m, d)
    off = cfg.off

    u = _norm_mod(cfg, x2d, wts["norm_mix_g"], mod, t, 0, 1)
    proj = _matmul(u, wts["w_in"], cfg.tm, cfg.tn_in, F32, "in_proj")
    k_new = proj[:, off["k"]:off["k"] + cfg.hkv].reshape(bsz, t, cfg.hkv)
    v_new = proj[:, off["v"]:off["v"] + cfg.hkv].reshape(bsz, t, cfg.hkv)
    ki_new = proj[:, off["kiw"]:off["kiw"] + cfg.idx_dim].reshape(bsz, t, cfg.idx_dim)
    lp = -(-(past + t) // TK_IDX) * TK_IDX
    k_all = _pad_keys(jnp.concatenate([past_k.reshape(bsz, past, cfg.hkv), k_new], axis=1), lp)
    v_all = _pad_keys(jnp.concatenate([past_v.reshape(bsz, past, cfg.hkv), v_new], axis=1), lp)
    ki_all = _pad_keys(jnp.concatenate([past_ik, ki_new], axis=1), lp)
    attn = _dsa(cfg, proj, t, past, k_all, v_all, ki_all, rel_bias)

    state0_t = jnp.transpose(ssm0.astype(F32), (0, 3, 1, 2)).reshape(bsz, cfg.d_state, cfg.d_ssd)
    ssd_out, state_t, ssd_conv_new = _ssd(cfg, proj, t, ssd_conv0, state0_t, wts["ssd_conv_w"], wts["ssd_conv_b"],
                                          wts["dt_bias"], wts["a_log"], wts["d_skip"], wts["ssd_norm_g"])
    h_new = jnp.transpose(state_t.reshape(bsz, cfg.d_state, cfg.ssd_heads, cfg.ssd_head_dim), (0, 2, 3, 1))

    merged = _merge(cfg, attn, ssd_out, wts["w_attn_o"], wts["w_ssd_o"], proj)
    x1 = _gated_residual(merged, wts["w_out"], x2d, mod, t, 2, cfg.tm, cfg.tn, "out_proj_residual")

    u2 = _norm_mod(cfg, x1, wts["norm_ffn_g"], mod, t, 3, 4)
    h, ffn_conv_new = _ffn_up(cfg, u2, wts["w_up"], wts["ffn_conv_w"], wts["ffn_conv_b"], ffn_conv0, t)
    x2 = _gated_residual(h, wts["w_down"], x1, mod, t, 5, cfg.tm_down, cfg.tn_down, "down_proj_residual")
    states = (k_new.reshape(bsz, t, cfg.n_kv, cfg.head_dim), v_new.reshape(bsz, t, cfg.n_kv, cfg.head_dim), ki_new,
              h_new.astype(ssm0.dtype), ssd_conv_new, ffn_conv_new)
    return x2.reshape(bsz, t, d), states


def _forward(cfg, x_prompt, x_sample, c_prompt, c_sample, cache_k, cache_v, cache_idx_k, state_ssm, state_ssd_conv,
             state_ffn_conv, rel_bias, w_ada, b_ada, norm_mix_g, w_in, ssd_conv_w, ssd_conv_b, dt_bias, a_log, d_skip,
             ssd_norm_g, w_attn_o, w_ssd_o, w_out, norm_ffn_g, w_up, ffn_conv_w, ffn_conv_b, w_down, final_norm_g):
    depth = w_in.shape[0]
    bp, tp, d = x_prompt.shape
    bs, ts, _ = x_sample.shape
    dt_ = x_prompt.dtype
    hp, hs = x_prompt, x_sample
    c_all = jnp.concatenate([c_prompt, c_sample], axis=0)
    c_all = jnp.pad(c_all, ((0, -(bp + bs) % 8), (0, 0)))
    prompt_states, sample_states = [], []
    for l in range(depth):
        mod = _modulation(cfg, c_all, w_ada[l], b_ada[l]).reshape(c_all.shape[0], 6, d)
        wts = dict(norm_mix_g=norm_mix_g[l], w_in=_pack_w_in(cfg, w_in[l]), ssd_conv_w=ssd_conv_w[l],
                   ssd_conv_b=ssd_conv_b[l], dt_bias=dt_bias[l], a_log=a_log[l], d_skip=d_skip[l],
                   ssd_norm_g=ssd_norm_g[l], w_attn_o=w_attn_o[l].astype(BF16), w_ssd_o=w_ssd_o[l].astype(BF16),
                   w_out=w_out[l].astype(BF16), norm_ffn_g=norm_ffn_g[l], w_up=w_up[l].astype(BF16),
                   ffn_conv_w=ffn_conv_w[l], ffn_conv_b=ffn_conv_b[l], w_down=w_down[l].astype(BF16))
        hp, st_p = _trunk_layer(cfg, hp, mod[:bp],
                                jnp.zeros((bp, 0, cfg.n_kv, cfg.head_dim), dt_),
                                jnp.zeros((bp, 0, cfg.n_kv, cfg.head_dim), dt_),
                                jnp.zeros((bp, 0, cfg.idx_dim), dt_),
                                jnp.zeros((bp, cfg.ssd_heads, cfg.ssd_head_dim, cfg.d_state), state_ssm.dtype),
                                jnp.zeros((bp, cfg.ssd_conv - 1, cfg.conv_dim), dt_),
                                jnp.zeros((bp, cfg.ffn_conv - 1, 2 * cfg.d_ff), dt_),
                                rel_bias, wts)
        hs, st_s = _trunk_layer(cfg, hs, mod[bp:bp + bs], cache_k[l], cache_v[l], cache_idx_k[l], state_ssm[l],
                                state_ssd_conv[l], state_ffn_conv[l], rel_bias, wts)
        prompt_states.append(st_p)
        sample_states.append(st_s)
    y_prompt = _final_norm(cfg, hp.reshape(bp * tp, d), final_norm_g).reshape(bp, tp, d)
    y_sample = _final_norm(cfg, hs.reshape(bs * ts, d), final_norm_g).reshape(bs, ts, d)
    stack = lambda states, i: jnp.stack([s[i] for s in states], axis=0)
    return (y_prompt, y_sample) + tuple(stack(prompt_states, i) for i in range(6)) + tuple(
        stack(sample_states, i) for i in range(6))


def kernel(x_prompt, x_sample, c_prompt, c_sample, cache_k, cache_v, cache_idx_k, state_ssm, state_ssd_conv,
           state_ffn_conv, rel_bias, w_ada, b_ada, norm_mix_g, w_in, ssd_conv_w, ssd_conv_b, dt_bias, a_log, d_skip,
           ssd_norm_g, w_attn_o, w_ssd_o, w_out, norm_ffn_g, w_up, ffn_conv_w, ffn_conv_b, w_down, final_norm_g):
    return _forward(Cfg(), x_prompt, x_sample, c_prompt, c_sample, cache_k, cache_v, cache_idx_k, state_ssm,
                    state_ssd_conv, state_ffn_conv, rel_bias, w_ada, b_ada, norm_mix_g, w_in, ssd_conv_w, ssd_conv_b,
                    dt_bias, a_log, d_skip, ssd_norm_g, w_attn_o, w_ssd_o, w_out, norm_ffn_g, w_up, ffn_conv_w,
                    ffn_conv_b, w_down, final_norm_g)
```

```python
import functools
import math

import numpy as np
import jax
import jax.numpy as jnp
from jax import lax
from jax.experimental import pallas as pl
from jax.experimental.pallas import tpu as pltpu

F32 = jnp.float32
BF16 = jnp.bfloat16
I32 = jnp.int32

LANES = 128
V7X_VMEM_BYTES = 64 * 1024 * 1024
VMEM_LIMIT = V7X_VMEM_BYTES - 8 * 1024 * 1024
INT_MIN = -(2 ** 31)
NEG_BIG = -1e30
M_INIT = -1e29
LOG2E = math.log2(math.e)
HIGHEST = lax.Precision.HIGHEST


class Cfg:
    def __init__(self, **kw):
        self.d_model = 4096
        self.chunk = 64
        self.n_heads = 16
        self.n_kv = 4
        self.head_dim = 128
        self.n_idx_heads = 16
        self.idx_dim = 64
        self.top_k_max = 256
        self.n_buckets = 32
        self.max_distance = 128
        self.ssd_heads = 32
        self.ssd_head_dim = 64
        self.ssd_groups = 4
        self.d_state = 128
        self.ssd_conv = 4
        self.d_ff = 11008
        self.ffn_conv = 3
        self.eps = 1e-6
        self.tm = 1024
        self.tn_in = 768
        self.tn = 512
        self.tn_ff = 256
        self.tm_down = 512
        self.tn_down = 512
        self.tr = 256
        self.tq = 128
        self.tk_far = 512
        self.ssd_chunk = 128
        self.tn_mod = 512
        for k, v in kw.items():
            assert hasattr(self, k), k
            setattr(self, k, v)
        self.d_ssd = self.ssd_heads * self.ssd_head_dim
        self.conv_dim = self.d_ssd + 2 * self.ssd_groups * self.d_state
        self.hq = self.n_heads * self.head_dim
        self.hkv = self.n_kv * self.head_dim
        self.hidx = self.n_idx_heads * self.idx_dim
        self.in_sizes = (self.hq, self.hkv, self.hkv, self.hidx, self.idx_dim, self.n_idx_heads, self.d_ssd,
                         self.conv_dim, self.ssd_heads, self.d_model, self.d_model)
        segs = [("q", self.hq), ("k", self.hkv), ("v", self.hkv), ("qi", self.hidx), ("z", self.d_ssd),
                ("xbc", self.conv_dim), ("ga", self.d_model), ("gs", self.d_model), ("kiw", LANES), ("dt", LANES)]
        off, self.off = 0, {}
        for name, width in segs:
            assert width % LANES == 0
            self.off[name] = off
            off += width
        self.n_packed = -(-off // self.tn_in) * self.tn_in
        assert self.idx_dim + self.n_idx_heads <= LANES and self.ssd_heads <= LANES


def _cparams(sem):
    return pltpu.CompilerParams(dimension_semantics=sem, vmem_limit_bytes=VMEM_LIMIT)


def _tile(n, pref):
    t = min(n, pref)
    assert n % t == 0, (n, pref)
    return t


def _seq_map(t_seq, tm):
    if t_seq >= tm:
        assert t_seq % tm == 0
        per = t_seq // tm
        return tm, 1, (lambda i: i // per)
    assert tm % t_seq == 0
    return t_seq, tm // t_seq, (lambda i: i)


def _mod_body(c_ref, w_ref, b_ref, o_ref):
    c = c_ref[...]
    a = (c * jax.nn.sigmoid(c)).astype(BF16)
    o_ref[...] = jnp.dot(a, w_ref[...].astype(BF16), preferred_element_type=F32) + b_ref[...]


def _modulation(cfg, c, w_ada, b_ada):
    rows, d = c.shape
    n = w_ada.shape[1]
    tn = _tile(n, cfg.tn_mod)
    return pl.pallas_call(
        _mod_body,
        grid=(n // tn,),
        in_specs=[pl.BlockSpec((rows, d), lambda j: (0, 0)),
                  pl.BlockSpec((d, tn), lambda j: (0, j)),
                  pl.BlockSpec((1, tn), lambda j: (0, j))],
        out_specs=pl.BlockSpec((rows, tn), lambda j: (0, j)),
        out_shape=jax.ShapeDtypeStruct((rows, n), F32),
        compiler_params=_cparams(("parallel",)),
        name="adaln_mod",
    )(c, w_ada, b_ada.reshape(1, n))


def _norm_mod_body(eps, sh_row, sc_row, x_ref, g_ref, mod_ref, o_ref):
    x = x_ref[...]
    y = x * lax.rsqrt(jnp.mean(x * x, axis=-1, keepdims=True) + eps) * g_ref[...]
    y = y * (1.0 + mod_ref[sc_row:sc_row + 1, :]) + mod_ref[sh_row:sh_row + 1, :]
    o_ref[...] = y.astype(o_ref.dtype)


def _norm_mod(cfg, x, g, mod, t_seq, sh_row, sc_row):
    m, d = x.shape
    tr = _tile(t_seq, cfg.tr)
    per = t_seq // tr
    return pl.pallas_call(
        functools.partial(_norm_mod_body, cfg.eps, sh_row, sc_row),
        grid=(m // tr,),
        in_specs=[pl.BlockSpec((tr, d), lambda i: (i, 0)),
                  pl.BlockSpec((1, d), lambda i: (0, 0)),
                  pl.BlockSpec((None, 6, d), lambda i: (i // per, 0, 0))],
        out_specs=pl.BlockSpec((tr, d), lambda i: (i, 0)),
        out_shape=jax.ShapeDtypeStruct((m, d), BF16),
        compiler_params=_cparams(("parallel",)),
        name="rmsnorm_mod",
    )(x, g.reshape(1, d), mod)


def _norm_body(eps, x_ref, g_ref, o_ref):
    x = x_ref[...]
    o_ref[...] = x * lax.rsqrt(jnp.mean(x * x, axis=-1, keepdims=True) + eps) * g_ref[...]


def _final_norm(cfg, x, g):
    m, d = x.shape
    tr = _tile(m, cfg.tr)
    return pl.pallas_call(
        functools.partial(_norm_body, cfg.eps),
        grid=(m // tr,),
        in_specs=[pl.BlockSpec((tr, d), lambda i: (i, 0)), pl.BlockSpec((1, d), lambda i: (0, 0))],
        out_specs=pl.BlockSpec((tr, d), lambda i: (i, 0)),
        out_shape=jax.ShapeDtypeStruct((m, d), F32),
        compiler_params=_cparams(("parallel",)),
        name="final_rmsnorm",
    )(x, g.reshape(1, d))


def _mm_nt_body(a_ref, wt_ref, o_ref):
    o_ref[...] = lax.dot_general(a_ref[...], wt_ref[...], (((1,), (1,)), ((), ())),
                                 preferred_element_type=F32).astype(o_ref.dtype)


def _matmul_nt(a, w_t, tm, tn, out_dtype, name):
    m, k = a.shape
    n = w_t.shape[0]
    tm, tn = _tile(m, tm), _tile(n, tn)
    return pl.pallas_call(
        _mm_nt_body,
        grid=(m // tm, n // tn),
        in_specs=[pl.BlockSpec((tm, k), lambda i, j: (i, 0)), pl.BlockSpec((tn, k), lambda i, j: (j, 0))],
        out_specs=pl.BlockSpec((tm, tn), lambda i, j: (i, j)),
        out_shape=jax.ShapeDtypeStruct((m, n), out_dtype),
        compiler_params=_cparams(("parallel", "parallel")),
        name=name,
    )(a, w_t)


def _merge_body(attn_ref, ssd_ref, wa_ref, ws_ref, ga_ref, gs_ref, o_ref):
    a = jnp.dot(attn_ref[...], wa_ref[...], preferred_element_type=F32)
    s = jnp.dot(ssd_ref[...], ws_ref[...], preferred_element_type=F32)
    o_ref[...] = (jax.nn.sigmoid(ga_ref[...]) * a + jax.nn.sigmoid(gs_ref[...]) * s).astype(o_ref.dtype)


def _merge(cfg, attn, ssd, w_attn_o, w_ssd_o, proj):
    m = attn.shape[0]
    d = cfg.d_model
    tm = _tile(m, cfg.tm)
    tn = math.gcd(math.gcd(cfg.off["ga"], cfg.off["gs"]), _tile(d, cfg.tn))
    ga0, gs0 = cfg.off["ga"] // tn, cfg.off["gs"] // tn
    return pl.pallas_call(
        _merge_body,
        grid=(m // tm, d // tn),
        in_specs=[pl.BlockSpec((tm, attn.shape[1]), lambda i, j: (i, 0)),
                  pl.BlockSpec((tm, ssd.shape[1]), lambda i, j: (i, 0)),
                  pl.BlockSpec((attn.shape[1], tn), lambda i, j: (0, j)),
                  pl.BlockSpec((ssd.shape[1], tn), lambda i, j: (0, j)),
                  pl.BlockSpec((tm, tn), lambda i, j: (i, ga0 + j)),
                  pl.BlockSpec((tm, tn), lambda i, j: (i, gs0 + j))],
        out_specs=pl.BlockSpec((tm, tn), lambda i, j: (i, j)),
        out_shape=jax.ShapeDtypeStruct((m, d), BF16),
        compiler_params=_cparams(("parallel", "parallel")),
        name="branch_merge",
    )(attn, ssd, w_attn_o, w_ssd_o, proj, proj)


def _resid_body(rows, groups, gate_row, a_ref, w_ref, x_ref, mod_ref, o_ref):
    acc = jnp.dot(a_ref[...], w_ref[...], preferred_element_type=F32)
    for g in range(groups):
        sl = slice(g * rows, (g + 1) * rows)
        o_ref[sl, :] = x_ref[sl, :] + mod_ref[g, gate_row:gate_row + 1, :] * acc[sl, :]


def _gated_residual(a, w, x, mod, t_seq, gate_row, tm, tn, name):
    m, k = a.shape
    n = w.shape[1]
    tm, tn = _tile(m, tm), _tile(n, tn)
    rows, groups, seq_of = _seq_map(t_seq, tm)
    return pl.pallas_call(
        functools.partial(_resid_body, rows, groups, gate_row),
        grid=(m // tm, n // tn),
        in_specs=[pl.BlockSpec((tm, k), lambda i, j: (i, 0)),
                  pl.BlockSpec((k, tn), lambda i, j: (0, j)),
                  pl.BlockSpec((tm, tn), lambda i, j: (i, j)),
                  pl.BlockSpec((groups, 6, tn), lambda i, j: (seq_of(i), 0, j))],
        out_specs=pl.BlockSpec((tm, tn), lambda i, j: (i, j)),
        out_shape=jax.ShapeDtypeStruct((m, n), F32),
        compiler_params=_cparams(("parallel", "parallel")),
        name=name,
    )(a, w, x, mod)


SUBLANES = 8


def _shifted_rows(x, prev, n):
    rid = lax.broadcasted_iota(I32, (SUBLANES, x.shape[1]), 0)
    out = []
    for k in range(1, n + 1):
        rolled = pltpu.roll(x, k, 0)
        head = rolled[0:SUBLANES, :]
        for r in range(k):
            head = jnp.where(rid == r, prev[n - k + r:n - k + r + 1, :], head)
        out.append(jnp.concatenate([head, rolled[SUBLANES:, :]], axis=0))
    return out


def _up_body(rows, groups, tiles_per_seq, a_ref, wg_ref, wv_ref, cwg_ref, cwv_ref, cbg_ref, cbv_ref, hg_ref, hv_ref,
             h_ref, ng_ref, nv_ref, carry_ref):
    i, j = pl.program_id(0), pl.program_id(1)
    a = a_ref[...]
    halves = []
    for half, (w_ref, cw_ref, cb_ref, hist_ref, new_ref) in enumerate(
            ((wg_ref, cwg_ref, cbg_ref, hg_ref, ng_ref), (wv_ref, cwv_ref, cbv_ref, hv_ref, nv_ref))):
        up = jnp.dot(a, w_ref[...], preferred_element_type=F32)
        outs = []
        for g in range(groups):
            u = up[g * rows:(g + 1) * rows, :]
            prev = hist_ref[g]
            if groups == 1 and tiles_per_seq > 1:
                prev = jnp.where(i % tiles_per_seq == 0, prev, carry_ref[j, half])
            s1, s2 = _shifted_rows(u, prev, 2)
            outs.append(cw_ref[0:1, :] * s2 + cw_ref[1:2, :] * s1 + cw_ref[2:3, :] * u + cb_ref[...])
            last = u[rows - 2:rows, :]
            new_ref[g] = last
            if groups == 1 and tiles_per_seq > 1:
                carry_ref[j, half] = last
        halves.append(outs)
    for g in range(groups):
        gate, val = halves[0][g], halves[1][g]
        h_ref[g * rows:(g + 1) * rows, :] = (gate * jax.nn.sigmoid(gate) * val).astype(h_ref.dtype)


def _ffn_up(cfg, u2, w_up, conv_w, conv_b, hist, t_seq):
    m, d = u2.shape
    nf = cfg.d_ff
    n_seq = m // t_seq
    tm, tn = _tile(m, cfg.tm), _tile(nf, cfg.tn_ff)
    rows, groups, seq_of = _seq_map(t_seq, tm)
    nj = nf // tn
    wspec = lambda half: pl.BlockSpec((d, tn), lambda i, j: (0, j + half * nj))
    cspec = lambda r, half: pl.BlockSpec((r, tn), lambda i, j: (0, j + half * nj))
    hspec = lambda half: pl.BlockSpec((groups, 2, tn), lambda i, j: (seq_of(i), 0, j + half * nj))
    ospec = pl.BlockSpec((groups, 2, tn), lambda i, j: (i, 0, j))
    h, new_g, new_v = pl.pallas_call(
        functools.partial(_up_body, rows, groups, max(1, t_seq // tm)),
        grid=(m // tm, nj),
        in_specs=[pl.BlockSpec((tm, d), lambda i, j: (i, 0)), wspec(0), wspec(1),
                  cspec(3, 0), cspec(3, 1), cspec(1, 0), cspec(1, 1), hspec(0), hspec(1)],
        out_specs=[pl.BlockSpec((tm, tn), lambda i, j: (i, j)), ospec, ospec],
        out_shape=[jax.ShapeDtypeStruct((m, nf), BF16),
                   jax.ShapeDtypeStruct((m // rows, 2, nf), F32),
                   jax.ShapeDtypeStruct((m // rows, 2, nf), F32)],
        scratch_shapes=[pltpu.VMEM((nj, 2, 2, tn), F32)],
        compiler_params=_cparams(("arbitrary", "arbitrary")),
        name="ffn_up_conv_swiglu",
    )(u2, w_up, w_up, conv_w, conv_w, conv_b.reshape(1, -1), conv_b.reshape(1, -1), hist, hist)
    per_seq = t_seq // rows
    return h, jnp.concatenate([new_g[per_seq - 1::per_seq], new_v[per_seq - 1::per_seq]], axis=-1)


def _ssd_body(cfg, lc, z_ref, xbc_ref, dt_ref, hist_ref, s0_ref, cw_ref, cb_ref, dtb_ref, alog_ref, dskip_ref,
              ng_ref, expand_ref, y_ref, state_ref, cnew_ref, xcarry_ref, ydiag_ref):
    nh, hp, ng, ns = cfg.ssd_heads, cfg.ssd_head_dim, cfg.ssd_groups, cfg.d_state
    ds = cfg.d_ssd
    per_group = nh // ng
    c = pl.program_id(1)

    @pl.when(c == 0)
    def _():
        state_ref[...] = s0_ref[...]
        xcarry_ref[0:3, :] = hist_ref[...]

    x = xbc_ref[...]
    prev = xcarry_ref[0:3, :]
    s1, s2, s3 = _shifted_rows(x, prev, 3)
    xc =cw_ref[0:1, :] * s3 + cw_ref[1:2, :] * s2 + cw_ref[2:3, :] * s1 + cw_ref[3:4, :] * x + cb_ref[...]
    xc = xc * jax.nn.sigmoid(xc)
    last = x[lc - 3:lc, :]
    xcarry_ref[0:3, :] = last
    cnew_ref[...] = last
    xs = xc[:, :ds]
    bm = xc[:, ds:ds + ng * ns].astype(BF16)
    cm = xc[:, ds + ng * ns:].astype(BF16)

    raw = dt_ref[...] + dtb_ref[...]
    dt = jnp.maximum(raw, 0.0) + jnp.log1p(jnp.exp(-jnp.abs(raw)))
    a = dt * (-jnp.exp(alog_ref[...]))
    ri = lax.broadcasted_iota(I32, (lc, lc), 0)
    ci = lax.broadcasted_iota(I32, (lc, lc), 1)
    causal = ri >= ci
    a_cum = jnp.dot(causal.astype(F32), a, precision=HIGHEST, preferred_element_type=F32)
    eye = (lax.broadcasted_iota(I32, (LANES, LANES), 0) == lax.broadcasted_iota(I32, (LANES, LANES), 1))
    nt = (((1,), (1,)), ((), ()))
    a_cum_t = lax.dot_general(eye.astype(F32), a_cum, nt, precision=HIGHEST, preferred_element_type=F32)
    expand = expand_ref[...]
    dt_x = jnp.dot(dt, expand, precision=HIGHEST, preferred_element_type=F32)
    a_x = jnp.dot(a_cum, expand, precision=HIGHEST, preferred_element_type=F32)
    a_end = a_x[lc - 1:lc, :]
    xd = xs * dt_x

    for g in range(ng):
        cb = lax.dot_general(cm[:, g * ns:(g + 1) * ns], bm[:, g * ns:(g + 1) * ns], nt, preferred_element_type=F32)
        for r in range(per_group):
            h = g * per_group + r
            seg = a_cum[:, h:h + 1] - a_cum_t[h:h + 1, :]
            lmat = jnp.exp(jnp.where(causal, seg, -jnp.inf))
            ydiag_ref[:, h * hp:(h + 1) * hp] = jnp.dot((cb * lmat).astype(BF16), xd[:, h * hp:(h + 1) * hp].astype(BF16),
                                                        preferred_element_type=F32)

    state = state_ref[...]
    xdd = (xd * jnp.exp(a_end - a_x)).astype(BF16)
    w = per_group * hp
    y_off, new_cols = [], []
    for g in range(ng):
        y_off.append(jnp.dot(cm[:, g * ns:(g + 1) * ns], state[:, g * w:(g + 1) * w].astype(BF16),
                             preferred_element_type=F32))
        b_t = lax.dot_general(eye.astype(BF16), bm[:, g * ns:(g + 1) * ns], nt, preferred_element_type=F32).astype(BF16)
        new_cols.append(jnp.dot(b_t, xdd[:, g * w:(g + 1) * w], preferred_element_type=F32))
    state_ref[...] = state * jnp.exp(a_end) + jnp.concatenate(new_cols, axis=1)
    y = ydiag_ref[...] + jnp.concatenate(y_off, axis=1) * jnp.exp(a_x) + dskip_ref[...] * xs

    z = z_ref[...]
    y = y * (z * jax.nn.sigmoid(z))
    gw = ds // ng
    for g in range(ng):
        yg = y[:, g * gw:(g + 1) * gw]
        yg = yg * lax.rsqrt(jnp.mean(yg * yg, axis=-1, keepdims=True) + cfg.eps)
        y_ref[:, g * gw:(g + 1) * gw] = (yg * ng_ref[:, g * gw:(g + 1) * gw]).astype(y_ref.dtype)


def _ssd(cfg, proj, t_seq, hist, state0_t, conv_w, conv_b, dt_bias, a_log, d_skip, norm_g):
    m = proj.shape[0]
    n_seq = m // t_seq
    lc = _tile(t_seq, cfg.ssd_chunk)
    nc = t_seq // lc
    ds, cd, ns = cfg.d_ssd, cfg.conv_dim, cfg.d_state
    assert cfg.off["z"] % ds == 0 and cfg.off["xbc"] % cd == 0
    zb, xb, db = cfg.off["z"] // ds, cfg.off["xbc"] // cd, cfg.off["dt"] // LANES
    pad = LANES - cfg.ssd_heads
    lane_row = lambda v: jnp.pad(v.astype(F32), (0, pad)).reshape(1, LANES)
    expand = (jnp.arange(LANES)[:, None] == (jnp.arange(ds) // cfg.ssd_head_dim)[None, :]).astype(F32)
    const = lambda shape: pl.BlockSpec(shape, lambda b, c: (0,) * len(shape))
    return pl.pallas_call(
        functools.partial(_ssd_body, cfg, lc),
        grid=(n_seq, nc),
        in_specs=[pl.BlockSpec((lc, ds), lambda b, c: (b * nc + c, zb)),
                  pl.BlockSpec((lc, cd), lambda b, c: (b * nc + c, xb)),
                  pl.BlockSpec((lc, LANES), lambda b, c: (b * nc + c, db)),
                  pl.BlockSpec((None, 3, cd), lambda b, c: (b, 0, 0)),
                  pl.BlockSpec((None, ns, ds), lambda b, c: (b, 0, 0)),
                  const((cfg.ssd_conv, cd)), const((1, cd)), const((1, LANES)), const((1, LANES)), const((1, ds)),
                  const((1, ds)), const((LANES, ds))],
        out_specs=[pl.BlockSpec((lc, ds), lambda b, c: (b * nc + c, 0)),
                   pl.BlockSpec((None, ns, ds), lambda b, c: (b, 0, 0)),
                   pl.BlockSpec((None, 3, cd), lambda b, c: (b, 0, 0))],
        out_shape=[jax.ShapeDtypeStruct((m, ds), BF16),
                   jax.ShapeDtypeStruct((n_seq, ns, ds), F32),
                   jax.ShapeDtypeStruct((n_seq, 3, cd), F32)],
        scratch_shapes=[pltpu.VMEM((8, cd), F32), pltpu.VMEM((lc, ds), F32)],
        compiler_params=_cparams(("parallel", "arbitrary")),
        name="ssd_scan",
    )(proj, proj, proj, hist, state0_t, conv_w, conv_b.reshape(1, cd), lane_row(dt_bias), lane_row(a_log),
      jnp.repeat(d_skip.astype(F32), cfg.ssd_head_dim).reshape(1, ds), norm_g.reshape(1, ds), expand)


NEAR_COLS = 640
NEAR_BACK = 512
TK_NEAR = 128
TK_IDX = 512


def _t5_bucket(cfg, rel):
    nb = cfg.n_buckets // 2
    max_exact = nb // 2
    ret = jnp.where(rel > 0, nb, 0)
    n = jnp.abs(rel)
    nf = jnp.maximum(n, 1).astype(F32)
    large = max_exact + (jnp.log(nf / max_exact) / math.log(cfg.max_distance / max_exact) * (nb - max_exact)).astype(I32)
    large = jnp.minimum(large, nb - 1)
    return ret + jnp.where(n < max_exact, n, large)


def _near_bias(cfg, rel_bias, tq):
    assert cfg.max_distance <= LANES
    rel = jnp.arange(NEAR_COLS, dtype=I32)[:, None] - NEAR_BACK - jnp.arange(tq, dtype=I32)[None, :]
    far = rel_bias[_t5_bucket(cfg, jnp.asarray(-cfg.max_distance, I32))].astype(F32)
    onehot = (_t5_bucket(cfg, rel)[..., None] == jnp.arange(cfg.n_buckets, dtype=I32)).astype(F32)
    tab = jnp.einsum("ctb,bh->cth", onehot, rel_bias.astype(F32), precision=HIGHEST)
    tab = (tab - far) * LOG2E
    grp = cfg.n_heads // cfg.n_kv
    tab = tab.reshape(NEAR_COLS, tq, cfg.n_kv, grp)
    return jnp.transpose(tab, (2, 0, 3, 1)).reshape(cfg.n_kv, NEAR_COLS, grp * tq)


def _dsa_body(cfg, tq, past, n_select, q_ref, qi_ref, kiw_ref, k_ref, v_ref, kidx_ref, bias_ref, o_ref,
              keys_ref, qs_ref, qis_ref, w_ref, m_ref, l_ref, acc_ref, s_ref):
    nkv, hd, di, nih = cfg.n_kv, cfg.head_dim, cfg.idx_dim, cfg.n_idx_heads
    grp = cfg.n_heads // nkv
    gw = grp * tq
    nt = (((1,), (1,)), ((), ()))
    tn = (((0,), (0,)), ((), ()))
    eye = lax.broadcasted_iota(I32, (LANES, LANES), 0) == lax.broadcasted_iota(I32, (LANES, LANES), 1)
    eye_bf = eye.astype(BF16)
    x0 = past + pl.program_id(1) * tq
    k_end = x0 + tq
    n_idx = (k_end + TK_IDX - 1) // TK_IDX

    q = (q_ref[...] * ((hd ** -0.5) * LOG2E)).astype(BF16)
    for n in range(nkv):
        for g in range(grp):
            h = n * grp + g
            qs_ref[n, :, g * tq:(g + 1) * tq] = lax.dot_general(
                eye_bf, q[:, h * hd:(h + 1) * hd], nt, preferred_element_type=F32).astype(BF16)
    qi = qi_ref[...].astype(BF16)
    per_blk = LANES // di
    for j in range(nih // per_blk):
        t_blk = lax.dot_general(eye_bf, qi[:, j * LANES:(j + 1) * LANES], nt, preferred_element_type=F32)
        for r in range(per_blk):
            h = j * per_blk + r
            qis_ref[:, h * tq:(h + 1) * tq] = t_blk[r * di:(r + 1) * di, :].astype(BF16)
    kiw_t = lax.dot_general(eye.astype(F32), kiw_ref[...], nt, precision=HIGHEST, preferred_element_type=F32)
    w_ref[...] = kiw_t[di:di + nih, :] * ((di ** -0.5) * (nih ** -0.5))

    qpos = x0 + lax.broadcasted_iota(I32, (1, tq), 1)
    limit = (jnp.right_shift(qpos, int(math.log2(cfg.chunk))) + 1) * cfg.chunk
    krow = lax.broadcasted_iota(I32, (TK_IDX, tq), 0)

    def idx_tile(t, carry):
        k0 = pl.multiple_of(t * TK_IDX, TK_IDX)
        logits = jnp.dot(kidx_ref[pl.ds(k0, TK_IDX), :], qis_ref[...], preferred_element_type=F32)
        sc = jnp.zeros((TK_IDX, tq), F32)
        for h in range(nih):
            sc = sc + w_ref[h:h + 1, :] * jnp.maximum(logits[:, h * tq:(h + 1) * tq], 0.0)
        bits = pltpu.bitcast(jnp.where(sc == 0.0, 0.0, sc), I32)
        key = jnp.where(bits < 0, bits ^ 0x7FFFFFFF, bits)
        keys_ref[pl.ds(k0, TK_IDX), :] = jnp.where(k0 + krow < limit, key, INT_MIN)
        return carry

    lax.fori_loop(0, n_idx, idx_tile, 0)

    def count(pred):
        def body(t, accs):
            k0 = pl.multiple_of(t * TK_IDX, TK_IDX)
            blk = keys_ref[pl.ds(k0, TK_IDX), :]
            accs = list(accs)
            for r in range(TK_IDX // SUBLANES):
                hit = pred(blk[r * SUBLANES:(r + 1) * SUBLANES, :])
                accs[r % len(accs)] = accs[r % len(accs)] + jnp.where(hit, 1.0, 0.0)
            return tuple(accs)

        accs = lax.fori_loop(0, n_idx, body, (jnp.zeros((SUBLANES, tq), F32),) * 8)
        return jnp.sum(functools.reduce(lambda a, b: a + b, accs), axis=0, keepdims=True)

    def bit_step(s, thr):
        bit = 31 - s
        cand = jnp.where(bit == 31, jnp.zeros_like(thr), thr | jnp.left_shift(jnp.int32(1), bit))
        cand8 = jnp.broadcast_to(cand, (SUBLANES, tq))
        return jnp.where(count(lambda b: b >= cand8) >= float(n_select), cand, thr)

    thr = lax.fori_loop(0, 32, bit_step, jnp.full((1, tq), INT_MIN, I32))
    thr = jnp.maximum(thr, INT_MIN + 1)

    thr8 = jnp.broadcast_to(thr, (SUBLANES, tq))
    need = float(n_select) - count(lambda b: b > thr8)
    n_tied = count(lambda b: b == thr8)

    @pl.when(jnp.max(n_tied - need) > 0.0)
    def _():
        tri = (lax.broadcasted_iota(I32, (TK_IDX, TK_IDX), 0) >= lax.broadcasted_iota(I32, (TK_IDX, TK_IDX), 1))
        tri = tri.astype(BF16)

        def drop_late_ties(t, seen):
            k0 = pl.multiple_of(t * TK_IDX, TK_IDX)
            blk = keys_ref[pl.ds(k0, TK_IDX), :]
            tied = blk == thr
            rank = seen + jnp.dot(tri, jnp.where(tied, 1.0, 0.0).astype(BF16), preferred_element_type=F32)
            keys_ref[pl.ds(k0, TK_IDX), :] = jnp.where(tied, jnp.where(rank > need, INT_MIN, blk), blk)
            return rank[TK_IDX - 1:TK_IDX, :]

        lax.fori_loop(0, n_idx, drop_late_ties, jnp.zeros((1, tq), F32))

    far_end = jnp.maximum(x0 - LANES, 0) // cfg.tk_far * cfg.tk_far
    n_far = far_end // cfg.tk_far
    n_near = ((k_end + TK_NEAR - 1) // TK_NEAR * TK_NEAR - far_end) // TK_NEAR
    m_ref[...] = jnp.full(m_ref.shape, M_INIT, F32)
    l_ref[...] = jnp.zeros(l_ref.shape, F32)
    acc_ref[...] = jnp.zeros(acc_ref.shape, F32)

    def tile(k0, width, bias_off):
        sel = keys_ref[pl.ds(k0, width), :] >= thr

        def logits(n):
            s = jnp.dot(k_ref[pl.ds(k0, width), n * hd:(n + 1) * hd], qs_ref[n], preferred_element_type=F32)
            if bias_off is not None:
                s = s + bias_ref[n, pl.ds(bias_off, width), :]
            for g in range(grp):
                s_ref[n % 2, 0:width, g * tq:(g + 1) * tq] = jnp.where(sel, s[:, g * tq:(g + 1) * tq], NEG_BIG)

        logits(0)
        for n in range(nkv):
            if n + 1 < nkv:
                logits(n + 1)
            s = s_ref[n % 2, 0:width, :]
            m_prev = m_ref[n]
            m_new = jnp.maximum(m_prev, jnp.max(s, axis=0, keepdims=True))
            alpha = jnp.exp2(m_prev - m_new)
            p = jnp.exp2(s - m_new)
            l_ref[n] = alpha * l_ref[n] + jnp.sum(p, axis=0, keepdims=True)
            acc_ref[n] = alpha * acc_ref[n] + lax.dot_general(
                v_ref[pl.ds(k0, width), n * hd:(n + 1) * hd], p.astype(BF16), tn, preferred_element_type=F32)
            m_ref[n] = m_new

    def far_tile(t, carry):
        tile(pl.multiple_of(t * cfg.tk_far, cfg.tk_far), cfg.tk_far, None)
        return carry

    def near_tile(u, carry):
        k0 = pl.multiple_of(far_end + u * TK_NEAR, TK_NEAR)
        tile(k0, TK_NEAR, pl.multiple_of(k0 - (x0 - NEAR_BACK), TK_NEAR))
        return carry

    lax.fori_loop(0, n_far, far_tile, 0)
    lax.fori_loop(0, n_near, near_tile, 0)
    for n in range(nkv):
        out_t = (acc_ref[n] / l_ref[n]).astype(BF16)
        for g in range(grp):
            h = n * grp + g
            o_ref[:, h * hd:(h + 1) * hd] = lax.dot_general(
                out_t[:, g * tq:(g + 1) * tq], eye_bf, tn, preferred_element_type=F32).astype(o_ref.dtype)


def _dsa(cfg, proj, t_seq, past, k_all, v_all, kidx_all, rel_bias):
    m = proj.shape[0]
    n_seq = m // t_seq
    tq = _tile(t_seq, cfg.tq)
    nq = t_seq // tq
    n_keys = past + t_seq
    lp = k_all.shape[1]
    assert tq % cfg.chunk == 0 and past % LANES == 0 and lp % TK_IDX == 0 and lp >= n_keys
    assert cfg.tk_far % TK_NEAR == 0 and NEAR_BACK == cfg.tk_far and NEAR_COLS == NEAR_BACK + LANES and tq <= LANES
    n_select = min(cfg.top_k_max, n_keys // 4)
    grp = cfg.n_heads // cfg.n_kv
    hq, hidx = cfg.hq, cfg.hidx
    assert cfg.off["q"] % hq == 0 and cfg.off["qi"] % hidx == 0
    assert cfg.head_dim == LANES and LANES % cfg.idx_dim == 0 and cfg.n_idx_heads % (LANES // cfg.idx_dim) == 0
    bias = _near_bias(cfg, rel_bias, tq)
    whole = lambda shape: pl.BlockSpec(shape, lambda b, i: (b,) + (0,) * (len(shape) - 1), pipeline_mode=pl.Buffered(1))
    return pl.pallas_call(
        functools.partial(_dsa_body, cfg, tq, past, n_select),
        grid=(n_seq, nq),
        in_specs=[pl.BlockSpec((tq, hq), lambda b, i: (b * nq + i, cfg.off["q"] // hq)),
                  pl.BlockSpec((tq, hidx), lambda b, i: (b * nq + i, cfg.off["qi"] // hidx)),
                  pl.BlockSpec((tq, LANES), lambda b, i: (b * nq + i, cfg.off["kiw"] // LANES)),
                  whole((None, lp, cfg.hkv)), whole((None, lp, cfg.hkv)), whole((None, lp, cfg.idx_dim)),
                  pl.BlockSpec(bias.shape, lambda b, i: (0, 0, 0), pipeline_mode=pl.Buffered(1))],
        out_specs=pl.BlockSpec((tq, hq), lambda b, i: (b * nq + i, 0)),
        out_shape=jax.ShapeDtypeStruct((m, hq), BF16),
        scratch_shapes=[pltpu.VMEM((lp, tq), I32),
                        pltpu.VMEM((cfg.n_kv, cfg.head_dim, grp * tq), BF16),
                        pltpu.VMEM((cfg.idx_dim, cfg.n_idx_heads * tq), BF16),
                        pltpu.VMEM((cfg.n_idx_heads, tq), F32),
                        pltpu.VMEM((cfg.n_kv, 1, grp * tq), F32), pltpu.VMEM((cfg.n_kv, 1, grp * tq), F32),
                        pltpu.VMEM((cfg.n_kv, cfg.head_dim, grp * tq), F32),
                        pltpu.VMEM((2, cfg.tk_far, grp * tq), F32)],
        compiler_params=_cparams(("parallel", "arbitrary")),
        name="dsa_attention",
    )(proj, proj, proj, k_all, v_all, kidx_all, bias)


def _pack_w_in(cfg, w_in):
    offs = np.cumsum(np.array(cfg.in_sizes))[:-1].tolist()
    q, k, v, qi, ki, wi, z, xbc, dt, ga, gs = jnp.split(w_in.T, offs, axis=0)
    row_pad = lambda a: jnp.pad(a, ((0, LANES - a.shape[0]), (0, 0)))
    packed = jnp.concatenate([q, k, v, qi, z, xbc, ga, gs, row_pad(jnp.concatenate([ki, wi], axis=0)), row_pad(dt)],
                             axis=0)
    return jnp.pad(packed, ((0, cfg.n_packed - packed.shape[0]), (0, 0))).astype(BF16)


def _pad_keys(a, lp):
    return jnp.pad(a, ((0, 0), (0, lp - a.shape[1]), (0, 0))).astype(BF16)


def _trunk_layer(cfg, x, mod, past_k, past_v, past_ik, ssm0, ssd_conv0, ffn_conv0, rel_bias, wts):
    bsz, t, d = x.shape
    m = bsz * t
    past = past_k.shape[1]
    x2d = x.reshape(m, d)
    off = cfg.off

    u = _norm_mod(cfg, x2d, wts["norm_mix_g"], mod, t, 0, 1)
    proj = _matmul_nt(u, wts["w_in_t"], cfg.tm, cfg.tn_in, F32, "in_proj")
    k_new = proj[:, off["k"]:off["k"] + cfg.hkv].reshape(bsz, t, cfg.hkv)
    v_new = proj[:, off["v"]:off["v"] + cfg.hkv].reshape(bsz, t, cfg.hkv)
    ki_new = proj[:, off["kiw"]:off["kiw"] + cfg.idx_dim].reshape(bsz, t, cfg.idx_dim)
    lp = -(-(past + t) // TK_IDX) * TK_IDX
    k_all = _pad_keys(jnp.concatenate([past_k.reshape(bsz, past, cfg.hkv), k_new], axis=1), lp)
    v_all = _pad_keys(jnp.concatenate([past_v.reshape(bsz, past, cfg.hkv), v_new], axis=1), lp)
    ki_all = _pad_keys(jnp.concatenate([past_ik, ki_new], axis=1), lp)
    attn = _dsa(cfg, proj, t, past, k_all, v_all, ki_all, rel_bias)

    state0_t = jnp.transpose(ssm0.astype(F32), (0, 3, 1, 2)).reshape(bsz, cfg.d_state, cfg.d_ssd)
    ssd_out, state_t, ssd_conv_new = _ssd(cfg, proj, t, ssd_conv0, state0_t, wts["ssd_conv_w"], wts["ssd_conv_b"],
                                          wts["dt_bias"], wts["a_log"], wts["d_skip"], wts["ssd_norm_g"])
    h_new = jnp.transpose(state_t.reshape(bsz, cfg.d_state, cfg.ssd_heads, cfg.ssd_head_dim), (0, 2, 3, 1))

    merged = _merge(cfg, attn, ssd_out, wts["w_attn_o"], wts["w_ssd_o"], proj)
    x1 = _gated_residual(merged, wts["w_out"], x2d, mod, t, 2, cfg.tm, cfg.tn, "out_proj_residual")

    u2 = _norm_mod(cfg, x1, wts["norm_ffn_g"], mod, t, 3, 4)
    h, ffn_conv_new = _ffn_up(cfg, u2, wts["w_up"], wts["ffn_conv_w"], wts["ffn_conv_b"], ffn_conv0, t)
    x2 = _gated_residual(h, wts["w_down"], x1, mod, t, 5, cfg.tm_down, cfg.tn_down, "down_proj_residual")
    states = (k_new.reshape(bsz, t, cfg.n_kv, cfg.head_dim), v_new.reshape(bsz, t, cfg.n_kv, cfg.head_dim), ki_new,
              h_new.astype(ssm0.dtype), ssd_conv_new, ffn_conv_new)
    return x2.reshape(bsz, t, d), states


def _forward(cfg, x_prompt, x_sample, c_prompt, c_sample, cache_k, cache_v, cache_idx_k, state_ssm, state_ssd_conv,
             state_ffn_conv, rel_bias, w_ada, b_ada, norm_mix_g, w_in, ssd_conv_w, ssd_conv_b, dt_bias, a_log, d_skip,
             ssd_norm_g, w_attn_o, w_ssd_o, w_out, norm_ffn_g, w_up, ffn_conv_w, ffn_conv_b, w_down, final_norm_g):
    depth = w_in.shape[0]
    bp, tp, d = x_prompt.shape
    bs, ts, _ = x_sample.shape
    dt_ = x_prompt.dtype
    hp, hs = x_prompt, x_sample
    c_all = jnp.concatenate([c_prompt, c_sample], axis=0)
    c_all = jnp.pad(c_all, ((0, -(bp + bs) % 8), (0, 0)))
    prompt_states, sample_states = [], []
    for l in range(depth):
        mod = _modulation(cfg, c_all, w_ada[l], b_ada[l]).reshape(c_all.shape[0], 6, d)
        wts = dict(norm_mix_g=norm_mix_g[l], w_in_t=_pack_w_in(cfg, w_in[l]), ssd_conv_w=ssd_conv_w[l],
                   ssd_conv_b=ssd_conv_b[l], dt_bias=dt_bias[l], a_log=a_log[l], d_skip=d_skip[l],
                   ssd_norm_g=ssd_norm_g[l], w_attn_o=w_attn_o[l].astype(BF16), w_ssd_o=w_ssd_o[l].astype(BF16),
                   w_out=w_out[l].astype(BF16), norm_ffn_g=norm_ffn_g[l], w_up=w_up[l].astype(BF16),
                   ffn_conv_w=ffn_conv_w[l], ffn_conv_b=ffn_conv_b[l], w_down=w_down[l].astype(BF16))
        hp, st_p = _trunk_layer(cfg, hp, mod[:bp],
                                jnp.zeros((bp, 0, cfg.n_kv, cfg.head_dim), dt_),
                                jnp.zeros((bp, 0, cfg.n_kv, cfg.head_dim), dt_),
                                jnp.zeros((bp, 0, cfg.idx_dim), dt_),
                                jnp.zeros((bp, cfg.ssd_heads, cfg.ssd_head_dim, cfg.d_state), state_ssm.dtype),
                                jnp.zeros((bp, cfg.ssd_conv - 1, cfg.conv_dim), dt_),
                                jnp.zeros((bp, cfg.ffn_conv - 1, 2 * cfg.d_ff), dt_),
                                rel_bias, wts)
        hs, st_s = _trunk_layer(cfg, hs, mod[bp:bp + bs], cache_k[l], cache_v[l], cache_idx_k[l], state_ssm[l],
                                state_ssd_conv[l], state_ffn_conv[l], rel_bias, wts)
        prompt_states.append(st_p)
        sample_states.append(st_s)
    y_prompt = _final_norm(cfg, hp.reshape(bp * tp, d), final_norm_g).reshape(bp, tp, d)
    y_sample = _final_norm(cfg, hs.reshape(bs * ts, d), final_norm_g).reshape(bs, ts, d)
    stack = lambda states, i: jnp.stack([s[i] for s in states], axis=0)
    return (y_prompt, y_sample) + tuple(stack(prompt_states, i) for i in range(6)) + tuple(
        stack(sample_states, i) for i in range(6))


def kernel(x_prompt, x_sample, c_prompt, c_sample, cache_k, cache_v, cache_idx_k, state_ssm, state_ssd_conv,
           state_ffn_conv, rel_bias, w_ada, b_ada, norm_mix_g, w_in, ssd_conv_w, ssd_conv_b, dt_bias, a_log, d_skip,
           ssd_norm_g, w_attn_o, w_ssd_o, w_out, norm_ffn_g, w_up, ffn_conv_w, ffn_conv_b, w_down, final_norm_g):
    return _forward(Cfg(), x_prompt, x_sample, c_prompt, c_sample, cache_k, cache_v, cache_idx_k, state_ssm,
                    state_ssd_conv, state_ffn_conv, rel_bias, w_ada, b_ada, norm_mix_g, w_in, ssd_conv_w, ssd_conv_b,
                    dt_bias, a_log, d_skip, ssd_norm_g, w_attn_o, w_ssd_o, w_out, norm_ffn_g, w_up, ffn_conv_w,
                    ffn_conv_b, w_down, final_norm_g)
```

```python
import functools
import math

import numpy as np
import jax
import jax.numpy as jnp
from jax import lax
from jax.experimental import pallas as pl
from jax.experimental.pallas import tpu as pltpu

F32 = jnp.float32
BF16 = jnp.bfloat16
I32 = jnp.int32

LANES = 128
V7X_VMEM_BYTES = 64 * 1024 * 1024
VMEM_LIMIT = V7X_VMEM_BYTES - 8 * 1024 * 1024
INT_MIN = -(2 ** 31)
NEG_BIG = -1e30
M_INIT = -1e29
LOG2E = math.log2(math.e)
HIGHEST = lax.Precision.HIGHEST


class Cfg:
    def __init__(self, **kw):
        self.d_model = 4096
        self.chunk = 64
        self.n_heads = 16
        self.n_kv = 4
        self.head_dim = 128
        self.n_idx_heads = 16
        self.idx_dim = 64
        self.top_k_max = 256
        self.n_buckets = 32
        self.max_distance = 128
        self.ssd_heads = 32
        self.ssd_head_dim = 64
        self.ssd_groups = 4
        self.d_state = 128
        self.ssd_conv = 4
        self.d_ff = 11008
        self.ffn_conv = 3
        self.eps = 1e-6
        self.tm = 1024
        self.tn_in = 768
        self.tn = 512
        self.tn_ff = 256
        self.tm_down = 512
        self.tn_down = 512
        self.tr = 256
        self.tq = 128
        self.tk_far = 512
        self.ssd_chunk = 128
        self.tn_mod = 512
        for k, v in kw.items():
            assert hasattr(self, k), k
            setattr(self, k, v)
        self.d_ssd = self.ssd_heads * self.ssd_head_dim
        self.conv_dim = self.d_ssd + 2 * self.ssd_groups * self.d_state
        self.hq = self.n_heads * self.head_dim
        self.hkv = self.n_kv * self.head_dim
        self.hidx = self.n_idx_heads * self.idx_dim
        self.in_sizes = (self.hq, self.hkv, self.hkv, self.hidx, self.idx_dim, self.n_idx_heads, self.d_ssd,
                         self.conv_dim, self.ssd_heads, self.d_model, self.d_model)
        segs = [("q", self.hq), ("k", self.hkv), ("v", self.hkv), ("qi", self.hidx), ("z", self.d_ssd),
                ("xbc", self.conv_dim), ("ga", self.d_model), ("gs", self.d_model), ("kiw", LANES), ("dt", LANES)]
        off, self.off = 0, {}
        for name, width in segs:
            assert width % LANES == 0
            self.off[name] = off
            off += width
        self.n_packed = -(-off // self.tn_in) * self.tn_in
        assert self.idx_dim + self.n_idx_heads <= LANES and self.ssd_heads <= LANES


def _cparams(sem):
    return pltpu.CompilerParams(dimension_semantics=sem, vmem_limit_bytes=VMEM_LIMIT)


def _tile(n, pref):
    t = min(n, pref)
    assert n % t == 0, (n, pref)
    return t


def _seq_map(t_seq, tm):
    if t_seq >= tm:
        assert t_seq % tm == 0
        per = t_seq // tm
        return tm, 1, (lambda i: i // per)
    assert tm % t_seq == 0
    return t_seq, tm // t_seq, (lambda i: i)


def _mod_body(c_ref, w_ref, b_ref, o_ref):
    c = c_ref[...]
    a = (c * jax.nn.sigmoid(c)).astype(BF16)
    o_ref[...] = jnp.dot(a, w_ref[...].astype(BF16), preferred_element_type=F32) + b_ref[...]


def _modulation(cfg, c, w_ada, b_ada):
    rows, d = c.shape
    n = w_ada.shape[1]
    tn = _tile(n, cfg.tn_mod)
    return pl.pallas_call(
        _mod_body,
        grid=(n // tn,),
        in_specs=[pl.BlockSpec((rows, d), lambda j: (0, 0)),
                  pl.BlockSpec((d, tn), lambda j: (0, j)),
                  pl.BlockSpec((1, tn), lambda j: (0, j))],
        out_specs=pl.BlockSpec((rows, tn), lambda j: (0, j)),
        out_shape=jax.ShapeDtypeStruct((rows, n), F32),
        compiler_params=_cparams(("parallel",)),
        name="adaln_mod",
    )(c, w_ada, b_ada.reshape(1, n))


def _norm_mod_body(eps, sh_row, sc_row, x_ref, g_ref, mod_ref, o_ref):
    x = x_ref[...]
    y = x * lax.rsqrt(jnp.mean(x * x, axis=-1, keepdims=True) + eps) * g_ref[...]
    y = y * (1.0 + mod_ref[sc_row:sc_row + 1, :]) + mod_ref[sh_row:sh_row + 1, :]
    o_ref[...] = y.astype(o_ref.dtype)


def _norm_mod(cfg, x, g, mod, t_seq, sh_row, sc_row):
    m, d = x.shape
    tr = _tile(t_seq, cfg.tr)
    per = t_seq // tr
    return pl.pallas_call(
        functools.partial(_norm_mod_body, cfg.eps, sh_row, sc_row),
        grid=(m // tr,),
        in_specs=[pl.BlockSpec((tr, d), lambda i: (i, 0)),
                  pl.BlockSpec((1, d), lambda i: (0, 0)),
                  pl.BlockSpec((None, 6, d), lambda i: (i // per, 0, 0))],
        out_specs=pl.BlockSpec((tr, d), lambda i: (i, 0)),
        out_shape=jax.ShapeDtypeStruct((m, d), BF16),
        compiler_params=_cparams(("parallel",)),
        name="rmsnorm_mod",
    )(x, g.reshape(1, d), mod)


def _norm_body(eps, x_ref, g_ref, o_ref):
    x = x_ref[...]
    o_ref[...] = x * lax.rsqrt(jnp.mean(x * x, axis=-1, keepdims=True) + eps) * g_ref[...]


def _final_norm(cfg, x, g):
    m, d = x.shape
    tr = _tile(m, cfg.tr)
    return pl.pallas_call(
        functools.partial(_norm_body, cfg.eps),
        grid=(m // tr,),
        in_specs=[pl.BlockSpec((tr, d), lambda i: (i, 0)), pl.BlockSpec((1, d), lambda i: (0, 0))],
        out_specs=pl.BlockSpec((tr, d), lambda i: (i, 0)),
        out_shape=jax.ShapeDtypeStruct((m, d), F32),
        compiler_params=_cparams(("parallel",)),
        name="final_rmsnorm",
    )(x, g.reshape(1, d))


def _mm_nt_body(a_ref, wt_ref, o_ref):
    o_ref[...] = lax.dot_general(a_ref[...], wt_ref[...], (((1,), (1,)), ((), ())),
                                 preferred_element_type=F32).astype(o_ref.dtype)


def _matmul_nt(a, w_t, tm, tn, out_dtype, name):
    m, k = a.shape
    n = w_t.shape[0]
    tm, tn = _tile(m, tm), _tile(n, tn)
    return pl.pallas_call(
        _mm_nt_body,
        grid=(m // tm, n // tn),
        in_specs=[pl.BlockSpec((tm, k), lambda i, j: (i, 0)), pl.BlockSpec((tn, k), lambda i, j: (j, 0))],
        out_specs=pl.BlockSpec((tm, tn), lambda i, j: (i, j)),
        out_shape=jax.ShapeDtypeStruct((m, n), out_dtype),
        compiler_params=_cparams(("parallel", "parallel")),
        name=name,
    )(a, w_t)


def _merge_body(attn_ref, ssd_ref, wa_ref, ws_ref, ga_ref, gs_ref, o_ref):
    a = jnp.dot(attn_ref[...], wa_ref[...], preferred_element_type=F32)
    s = jnp.dot(ssd_ref[...], ws_ref[...], preferred_element_type=F32)
    o_ref[...] = (jax.nn.sigmoid(ga_ref[...]) * a + jax.nn.sigmoid(gs_ref[...]) * s).astype(o_ref.dtype)


def _merge(cfg, attn, ssd, w_attn_o, w_ssd_o, proj):
    m = attn.shape[0]
    d = cfg.d_model
    tm = _tile(m, cfg.tm)
    tn = math.gcd(math.gcd(cfg.off["ga"], cfg.off["gs"]), _tile(d, cfg.tn))
    ga0, gs0 = cfg.off["ga"] // tn, cfg.off["gs"] // tn
    return pl.pallas_call(
        _merge_body,
        grid=(m // tm, d // tn),
        in_specs=[pl.BlockSpec((tm, attn.shape[1]), lambda i, j: (i, 0)),
                  pl.BlockSpec((tm, ssd.shape[1]), lambda i, j: (i, 0)),
                  pl.BlockSpec((attn.shape[1], tn), lambda i, j: (0, j)),
                  pl.BlockSpec((ssd.shape[1], tn), lambda i, j: (0, j)),
                  pl.BlockSpec((tm, tn), lambda i, j: (i, ga0 + j)),
                  pl.BlockSpec((tm, tn), lambda i, j: (i, gs0 + j))],
        out_specs=pl.BlockSpec((tm, tn), lambda i, j: (i, j)),
        out_shape=jax.ShapeDtypeStruct((m, d), BF16),
        compiler_params=_cparams(("parallel", "parallel")),
        name="branch_merge",
    )(attn, ssd, w_attn_o, w_ssd_o, proj, proj)


def _resid_body(rows, groups, gate_row, a_ref, w_ref, x_ref, mod_ref, o_ref):
    acc = jnp.dot(a_ref[...], w_ref[...], preferred_element_type=F32)
    for g in range(groups):
        sl = slice(g * rows, (g + 1) * rows)
        o_ref[sl, :] = x_ref[sl, :] + mod_ref[g, gate_row:gate_row + 1, :] * acc[sl, :]


def _gated_residual(a, w, x, mod, t_seq, gate_row, tm, tn, name):
    m, k = a.shape
    n = w.shape[1]
    tm, tn = _tile(m, tm), _tile(n, tn)
    rows, groups, seq_of = _seq_map(t_seq, tm)
    return pl.pallas_call(
        functools.partial(_resid_body, rows, groups, gate_row),
        grid=(m // tm, n // tn),
        in_specs=[pl.BlockSpec((tm, k), lambda i, j: (i, 0)),
                  pl.BlockSpec((k, tn), lambda i, j: (0, j)),
                  pl.BlockSpec((tm, tn), lambda i, j: (i, j)),
                  pl.BlockSpec((groups, 6, tn), lambda i, j: (seq_of(i), 0, j))],
        out_specs=pl.BlockSpec((tm, tn), lambda i, j: (i, j)),
        out_shape=jax.ShapeDtypeStruct((m, n), F32),
        compiler_params=_cparams(("parallel", "parallel")),
        name=name,
    )(a, w, x, mod)


SUBLANES = 8


def _shifted_rows(x, prev, n):
    rid = lax.broadcasted_iota(I32, (SUBLANES, x.shape[1]), 0)
    out = []
    for k in range(1, n + 1):
        rolled = pltpu.roll(x, k, 0)
        head = rolled[0:SUBLANES, :]
        for r in range(k):
            head = jnp.where(rid == r, prev[n - k + r:n - k + r + 1, :], head)
        out.append(jnp.concatenate([head, rolled[SUBLANES:, :]], axis=0))
    return out


def _up_body(rows, groups, tiles_per_seq, a_ref, wg_ref, wv_ref, cwg_ref, cwv_ref, cbg_ref, cbv_ref, hg_ref, hv_ref,
             h_ref, ng_ref, nv_ref, up_ref, carry_ref, w_scr):
    i = pl.program_id(1)

    @pl.when(i == 0)
    def _():
        w_scr[0] = wg_ref[...].astype(w_scr.dtype)
        w_scr[1] = wv_ref[...].astype(w_scr.dtype)

    a = a_ref[...]
    stride = rows + SUBLANES
    hc = []
    for half, (cw_ref, cb_ref, hist_ref, new_ref) in enumerate(
            ((cwg_ref, cbg_ref, hg_ref, ng_ref), (cwv_ref, cbv_ref, hv_ref, nv_ref))):
        up = jnp.dot(a, w_scr[half], preferred_element_type=F32)
        outs = []
        for g in range(groups):
            base = g * stride + SUBLANES
            prev = hist_ref[g]
            if groups == 1 and tiles_per_seq > 1:
                prev = jnp.where(i % tiles_per_seq == 0, prev, carry_ref[half])
            up_ref[half, base - 2:base, :] = prev
            up_ref[half, base:base + rows, :] = up[g * rows:(g + 1) * rows, :]
            last = up[(g + 1) * rows - 2:(g + 1) * rows, :]
            new_ref[g] = last
            if groups == 1 and tiles_per_seq > 1:
                carry_ref[half] = last
            taps =[up_ref[half, base - k:base - k + rows, :] for k in (2, 1, 0)]
            outs.append(cw_ref[0:1, :] * taps[0] + cw_ref[1:2, :] * taps[1] + cw_ref[2:3, :] * taps[2] + cb_ref[...])
        hc.append(outs)
    for g in range(groups):
        gate, val = hc[0][g], hc[1][g]
        h_ref[g * rows:(g + 1) * rows, :] = (gate * jax.nn.sigmoid(gate) * val).astype(h_ref.dtype)


def _ffn_up(cfg, u2, w_up, conv_w, conv_b, hist, t_seq):
    m, d = u2.shape
    nf = cfg.d_ff
    tm, tn = _tile(m, cfg.tm), _tile(nf, cfg.tn_ff)
    rows, groups, seq_of = _seq_map(t_seq, tm)
    nj = nf // tn
    wspec = lambda half: pl.BlockSpec((d, tn), lambda j, i: (0, j + half * nj))
    cspec = lambda r, half: pl.BlockSpec((r, tn), lambda j, i: (0, j + half * nj))
    hspec = lambda half: pl.BlockSpec((groups, 2, tn), lambda j, i: (seq_of(i), 0, j + half * nj))
    ospec = pl.BlockSpec((groups, 2, tn), lambda j, i: (i, 0, j))
    h, new_g, new_v = pl.pallas_call(
        functools.partial(_up_body, rows, groups, max(1, t_seq // tm)),
        grid=(nj, m // tm),
        in_specs=[pl.BlockSpec((tm, d), lambda j, i: (i, 0)), wspec(0), wspec(1),
                  cspec(3, 0), cspec(3, 1), cspec(1, 0), cspec(1, 1), hspec(0), hspec(1)],
        out_specs=[pl.BlockSpec((tm, tn), lambda j, i: (i, j)), ospec, ospec],
        out_shape=[jax.ShapeDtypeStruct((m, nf), BF16),
                   jax.ShapeDtypeStruct((m // rows, 2, nf), F32),
                   jax.ShapeDtypeStruct((m // rows, 2, nf), F32)],
        scratch_shapes=[pltpu.VMEM((2, groups * (rows + SUBLANES), tn), F32), pltpu.VMEM((2, 2, tn), F32),
                        pltpu.VMEM((2, d, tn), BF16)],
        compiler_params=_cparams(("arbitrary", "arbitrary")),
        name="ffn_up_conv_swiglu",
    )(u2, w_up, w_up, conv_w, conv_w, conv_b.reshape(1, -1), conv_b.reshape(1, -1), hist, hist)
    per_seq = t_seq // rows
    return h, jnp.concatenate([new_g[per_seq - 1::per_seq], new_v[per_seq - 1::per_seq]], axis=-1)


def _ssd_body(cfg, lc, z_ref, xbc_ref, dt_ref, hist_ref, s0_ref, cw_ref, cb_ref, dtb_ref, alog_ref, dskip_ref,
              ng_ref, expand_ref, y_ref, state_ref, cnew_ref, xcarry_ref, ydiag_ref):
    nh, hp, ng, ns = cfg.ssd_heads, cfg.ssd_head_dim, cfg.ssd_groups, cfg.d_state
    ds = cfg.d_ssd
    per_group = nh // ng
    c = pl.program_id(1)

    @pl.when(c == 0)
    def _():
        state_ref[...] = s0_ref[...]
        xcarry_ref[0:3, :] = hist_ref[...]

    x = xbc_ref[...]
    prev = xcarry_ref[0:3, :]
    s1, s2, s3 = _shifted_rows(x, prev, 3)
    xc =cw_ref[0:1, :] * s3 + cw_ref[1:2, :] * s2 + cw_ref[2:3, :] * s1 + cw_ref[3:4, :] * x + cb_ref[...]
    xc = xc * jax.nn.sigmoid(xc)
    last = x[lc - 3:lc, :]
    xcarry_ref[0:3, :] = last
    cnew_ref[...] = last
    xs = xc[:, :ds]
    bm = xc[:, ds:ds + ng * ns].astype(BF16)
    cm = xc[:, ds + ng * ns:].astype(BF16)

    raw = dt_ref[...] + dtb_ref[...]
    dt = jnp.maximum(raw, 0.0) + jnp.log1p(jnp.exp(-jnp.abs(raw)))
    a = dt * (-jnp.exp(alog_ref[...]))
    ri = lax.broadcasted_iota(I32, (lc, lc), 0)
    ci = lax.broadcasted_iota(I32, (lc, lc), 1)
    causal = ri >= ci
    a_cum = jnp.dot(causal.astype(F32), a, precision=HIGHEST, preferred_element_type=F32)
    eye = (lax.broadcasted_iota(I32, (LANES, LANES), 0) == lax.broadcasted_iota(I32, (LANES, LANES), 1))
    nt = (((1,), (1,)), ((), ()))
    a_cum_t = lax.dot_general(eye.astype(F32), a_cum, nt, precision=HIGHEST, preferred_element_type=F32)
    expand = expand_ref[...]
    dt_x = jnp.dot(dt, expand, precision=HIGHEST, preferred_element_type=F32)
    a_x = jnp.dot(a_cum, expand, precision=HIGHEST, preferred_element_type=F32)
    a_end = a_x[lc - 1:lc, :]
    xd = xs * dt_x

    for g in range(ng):
        cb = lax.dot_general(cm[:, g * ns:(g + 1) * ns], bm[:, g * ns:(g + 1) * ns], nt, preferred_element_type=F32)
        for r in range(per_group):
            h = g * per_group + r
            seg = a_cum[:, h:h + 1] - a_cum_t[h:h + 1, :]
            lmat = jnp.exp(jnp.where(causal, seg, -jnp.inf))
            ydiag_ref[:, h * hp:(h + 1) * hp] = jnp.dot((cb * lmat).astype(BF16), xd[:, h * hp:(h + 1) * hp].astype(BF16),
                                                        preferred_element_type=F32)

    state = state_ref[...]
    xdd = (xd * jnp.exp(a_end - a_x)).astype(BF16)
    w = per_group * hp
    y_off, new_cols = [], []
    for g in range(ng):
        y_off.append(jnp.dot(cm[:, g * ns:(g + 1) * ns], state[:, g * w:(g + 1) * w].astype(BF16),
                             preferred_element_type=F32))
        b_t = lax.dot_general(eye.astype(BF16), bm[:, g * ns:(g + 1) * ns], nt, preferred_element_type=F32).astype(BF16)
        new_cols.append(jnp.dot(b_t, xdd[:, g * w:(g + 1) * w], preferred_element_type=F32))
    state_ref[...] = state * jnp.exp(a_end) + jnp.concatenate(new_cols, axis=1)
    y = ydiag_ref[...] + jnp.concatenate(y_off, axis=1) * jnp.exp(a_x) + dskip_ref[...] * xs

    z = z_ref[...]
    y = y * (z * jax.nn.sigmoid(z))
    gw = ds // ng
    for g in range(ng):
        yg = y[:, g * gw:(g + 1) * gw]
        yg = yg * lax.rsqrt(jnp.mean(yg * yg, axis=-1, keepdims=True) + cfg.eps)
        y_ref[:, g * gw:(g + 1) * gw] = (yg * ng_ref[:, g * gw:(g + 1) * gw]).astype(y_ref.dtype)


def _ssd(cfg, proj, t_seq, hist, state0_t, conv_w, conv_b, dt_bias, a_log, d_skip, norm_g):
    m = proj.shape[0]
    n_seq = m // t_seq
    lc = _tile(t_seq, cfg.ssd_chunk)
    nc = t_seq // lc
    ds, cd, ns = cfg.d_ssd, cfg.conv_dim, cfg.d_state
    assert cfg.off["z"] % ds == 0 and cfg.off["xbc"] % cd == 0
    zb, xb, db = cfg.off["z"] // ds, cfg.off["xbc"] // cd, cfg.off["dt"] // LANES
    pad = LANES - cfg.ssd_heads
    lane_row = lambda v: jnp.pad(v.astype(F32), (0, pad)).reshape(1, LANES)
    expand = (jnp.arange(LANES)[:, None] == (jnp.arange(ds) // cfg.ssd_head_dim)[None, :]).astype(F32)
    const = lambda shape: pl.BlockSpec(shape, lambda b, c: (0,) * len(shape))
    return pl.pallas_call(
        functools.partial(_ssd_body, cfg, lc),
        grid=(n_seq, nc),
        in_specs=[pl.BlockSpec((lc, ds), lambda b, c: (b * nc + c, zb)),
                  pl.BlockSpec((lc, cd), lambda b, c: (b * nc + c, xb)),
                  pl.BlockSpec((lc, LANES), lambda b, c: (b * nc + c, db)),
                  pl.BlockSpec((None, 3, cd), lambda b, c: (b, 0, 0)),
                  pl.BlockSpec((None, ns, ds), lambda b, c: (b, 0, 0)),
                  const((cfg.ssd_conv, cd)), const((1, cd)), const((1, LANES)), const((1, LANES)), const((1, ds)),
                  const((1, ds)), const((LANES, ds))],
        out_specs=[pl.BlockSpec((lc, ds), lambda b, c: (b * nc + c, 0)),
                   pl.BlockSpec((None, ns, ds), lambda b, c: (b, 0, 0)),
                   pl.BlockSpec((None, 3, cd), lambda b, c: (b, 0, 0))],
        out_shape=[jax.ShapeDtypeStruct((m, ds), BF16),
                   jax.ShapeDtypeStruct((n_seq, ns, ds), F32),
                   jax.ShapeDtypeStruct((n_seq, 3, cd), F32)],
        scratch_shapes=[pltpu.VMEM((8, cd), F32), pltpu.VMEM((lc, ds), F32)],
        compiler_params=_cparams(("parallel", "arbitrary")),
        name="ssd_scan",
    )(proj, proj, proj, hist, state0_t, conv_w, conv_b.reshape(1, cd), lane_row(dt_bias), lane_row(a_log),
      jnp.repeat(d_skip.astype(F32), cfg.ssd_head_dim).reshape(1, ds), norm_g.reshape(1, ds), expand)


NEAR_COLS = 640
NEAR_BACK = 512
TK_NEAR = 128
TK_IDX = 512


def _t5_bucket(cfg, rel):
    nb = cfg.n_buckets // 2
    max_exact = nb // 2
    ret = jnp.where(rel > 0, nb, 0)
    n = jnp.abs(rel)
    nf = jnp.maximum(n, 1).astype(F32)
    large = max_exact + (jnp.log(nf / max_exact) / math.log(cfg.max_distance / max_exact) * (nb - max_exact)).astype(I32)
    large = jnp.minimum(large, nb - 1)
    return ret + jnp.where(n < max_exact, n, large)


def _near_bias(cfg, rel_bias, tq):
    assert cfg.max_distance <= LANES
    rel = jnp.arange(NEAR_COLS, dtype=I32)[:, None] - NEAR_BACK - jnp.arange(tq, dtype=I32)[None, :]
    far = rel_bias[_t5_bucket(cfg, jnp.asarray(-cfg.max_distance, I32))].astype(F32)
    onehot = (_t5_bucket(cfg, rel)[..., None] == jnp.arange(cfg.n_buckets, dtype=I32)).astype(F32)
    tab = jnp.einsum("ctb,bh->cth", onehot, rel_bias.astype(F32), precision=HIGHEST)
    tab = (tab - far) * LOG2E
    grp = cfg.n_heads // cfg.n_kv
    tab = tab.reshape(NEAR_COLS, tq, cfg.n_kv, grp)
    return jnp.transpose(tab, (2, 0, 3, 1)).reshape(cfg.n_kv, NEAR_COLS, grp * tq)


def _dsa_body(cfg, tq, past, n_select, q_ref, qi_ref, kiw_ref, k_ref, v_ref, kidx_ref, bias_ref, o_ref,
              keys_ref, qs_ref, qis_ref, w_ref, m_ref, l_ref, acc_ref, s_ref):
    nkv, hd, di, nih = cfg.n_kv, cfg.head_dim, cfg.idx_dim, cfg.n_idx_heads
    grp = cfg.n_heads // nkv
    gw = grp * tq
    nt = (((1,), (1,)), ((), ()))
    tn = (((0,), (0,)), ((), ()))
    eye = lax.broadcasted_iota(I32, (LANES, LANES), 0) == lax.broadcasted_iota(I32, (LANES, LANES), 1)
    eye_bf = eye.astype(BF16)
    x0 = past + pl.program_id(1) * tq
    k_end = x0 + tq
    n_idx = (k_end + TK_IDX - 1) // TK_IDX

    q = (q_ref[...] * ((hd ** -0.5) * LOG2E)).astype(BF16)
    for n in range(nkv):
        for g in range(grp):
            h = n * grp + g
            qs_ref[n, :, g * tq:(g + 1) * tq] = lax.dot_general(
                eye_bf, q[:, h * hd:(h + 1) * hd], nt, preferred_element_type=F32).astype(BF16)
    qi = qi_ref[...].astype(BF16)
    per_blk = LANES // di
    for j in range(nih // per_blk):
        t_blk = lax.dot_general(eye_bf, qi[:, j * LANES:(j + 1) * LANES], nt, preferred_element_type=F32)
        for r in range(per_blk):
            h = j * per_blk + r
            qis_ref[:, h * tq:(h + 1) * tq] = t_blk[r * di:(r + 1) * di, :].astype(BF16)
    kiw_t = lax.dot_general(eye.astype(F32), kiw_ref[...], nt, precision=HIGHEST, preferred_element_type=F32)
    w_ref[...] = kiw_t[di:di + nih, :] * ((di ** -0.5) * (nih ** -0.5))

    qpos = x0 + lax.broadcasted_iota(I32, (1, tq), 1)
    limit = (jnp.right_shift(qpos, int(math.log2(cfg.chunk))) + 1) * cfg.chunk
    krow = lax.broadcasted_iota(I32, (TK_IDX, tq), 0)

    def idx_tile(t, carry):
        k0 = pl.multiple_of(t * TK_IDX, TK_IDX)
        logits = jnp.dot(kidx_ref[pl.ds(k0, TK_IDX), :], qis_ref[...], preferred_element_type=F32)
        sc = jnp.zeros((TK_IDX, tq), F32)
        for h in range(nih):
            sc = sc + w_ref[h:h + 1, :] * jnp.maximum(logits[:, h * tq:(h + 1) * tq], 0.0)
        bits = pltpu.bitcast(jnp.where(sc == 0.0, 0.0, sc), I32)
        key = jnp.where(bits < 0, bits ^ 0x7FFFFFFF, bits)
        keys_ref[pl.ds(k0, TK_IDX), :] = jnp.where(k0 + krow < limit, key, INT_MIN)
        return carry

    lax.fori_loop(0, n_idx, idx_tile, 0)

    def count(pred):
        def body(t, accs):
            k0 = pl.multiple_of(t * TK_IDX, TK_IDX)
            blk = keys_ref[pl.ds(k0, TK_IDX), :]
            accs = list(accs)
            for r in range(TK_IDX // SUBLANES):
                hit = pred(blk[r * SUBLANES:(r + 1) * SUBLANES, :])
                accs[r % len(accs)] = accs[r % len(accs)] + jnp.where(hit, 1.0, 0.0)
            return tuple(accs)

        accs = lax.fori_loop(0, n_idx, body, (jnp.zeros((SUBLANES, tq), F32),) * 8)
        return jnp.sum(functools.reduce(lambda a, b: a + b, accs), axis=0, keepdims=True)

    def bit_step(s, thr):
        bit = 31 - s
        cand = jnp.where(bit == 31, jnp.zeros_like(thr), thr | jnp.left_shift(jnp.int32(1), bit))
        cand8 = jnp.broadcast_to(cand, (SUBLANES, tq))
        return jnp.where(count(lambda b: b >= cand8) >= float(n_select), cand, thr)

    thr = lax.fori_loop(0, 32, bit_step, jnp.full((1, tq), INT_MIN, I32))
    thr = jnp.maximum(thr, INT_MIN + 1)

    thr8 = jnp.broadcast_to(thr, (SUBLANES, tq))
    need = float(n_select) - count(lambda b: b > thr8)
    n_tied = count(lambda b: b == thr8)

    @pl.when(jnp.max(n_tied - need) > 0.0)
    def _():
        tri = (lax.broadcasted_iota(I32, (TK_IDX, TK_IDX), 0) >= lax.broadcasted_iota(I32, (TK_IDX, TK_IDX), 1))
        tri = tri.astype(BF16)

        def drop_late_ties(t, seen):
            k0 = pl.multiple_of(t * TK_IDX, TK_IDX)
            blk = keys_ref[pl.ds(k0, TK_IDX), :]
            tied = blk == thr
            rank = seen + jnp.dot(tri, jnp.where(tied, 1.0, 0.0).astype(BF16), preferred_element_type=F32)
            keys_ref[pl.ds(k0, TK_IDX), :] = jnp.where(tied, jnp.where(rank > need, INT_MIN, blk), blk)
            return rank[TK_IDX - 1:TK_IDX, :]

        lax.fori_loop(0, n_idx, drop_late_ties, jnp.zeros((1, tq), F32))

    far_end = jnp.maximum(x0 - LANES, 0) // cfg.tk_far * cfg.tk_far
    n_far = far_end // cfg.tk_far
    n_near = ((k_end + TK_NEAR - 1) // TK_NEAR * TK_NEAR - far_end) // TK_NEAR
    m_ref[...] = jnp.full(m_ref.shape, M_INIT, F32)
    l_ref[...] = jnp.zeros(l_ref.shape, F32)
    acc_ref[...] = jnp.zeros(acc_ref.shape, F32)

    def key_tiles(n_tiles, first, width, with_bias):
        ahead = nkv % 2 == 0

        def logits(n, k0):
            s = jnp.dot(k_ref[pl.ds(k0, width), n * hd:(n + 1) * hd], qs_ref[n], preferred_element_type=F32)
            if with_bias:
                s = s + bias_ref[n, pl.ds(pl.multiple_of(k0 - (x0 - NEAR_BACK), TK_NEAR), width), :]
            sel = keys_ref[pl.ds(k0, width), :] >= thr
            for g in range(grp):
                s_ref[n % 2, 0:width, g * tq:(g + 1) * tq] = jnp.where(sel, s[:, g * tq:(g + 1) * tq], NEG_BIG)

        start = lambda t: pl.multiple_of(first + t * width, TK_NEAR)
        if ahead:
            logits(0, start(0))

        def body(t, carry):
            k0 = start(t)
            if not ahead:
                logits(0, k0)
            for n in range(nkv):
                if n + 1 < nkv:
                    logits(n + 1, k0)
                elif ahead:
                    logits(0, start(jnp.minimum(t + 1, n_tiles - 1)))
                s = s_ref[n % 2, 0:width, :]
                m_prev = m_ref[n]
                m_new = jnp.maximum(m_prev, jnp.max(s, axis=0, keepdims=True))
                alpha = jnp.exp2(m_prev - m_new)
                p = jnp.exp2(s - m_new)
                l_ref[n] = alpha * l_ref[n] + jnp.sum(p, axis=0, keepdims=True)
                acc_ref[n] = alpha * acc_ref[n] + lax.dot_general(
                    v_ref[pl.ds(k0, width), n * hd:(n + 1) * hd], p.astype(BF16), tn, preferred_element_type=F32)
                m_ref[n] = m_new
            return carry

        lax.fori_loop(0, n_tiles, body, 0)

    key_tiles(n_far, 0, cfg.tk_far, False)
    key_tiles(n_near, far_end, TK_NEAR, True)
    for n in range(nkv):
        out_t = (acc_ref[n] / l_ref[n]).astype(BF16)
        for g in range(grp):
            h = n * grp + g
            o_ref[:, h * hd:(h + 1) * hd] = lax.dot_general(
                out_t[:, g * tq:(g + 1) * tq], eye_bf, tn, preferred_element_type=F32).astype(o_ref.dtype)


def _dsa(cfg, proj, t_seq, past, k_all, v_all, kidx_all, rel_bias):
    m = proj.shape[0]
    n_seq = m // t_seq
    tq = _tile(t_seq, cfg.tq)
    nq = t_seq // tq
    n_keys = past + t_seq
    lp = k_all.shape[1]
    assert tq % cfg.chunk == 0 and past % LANES == 0 and lp % TK_IDX == 0 and lp >= n_keys
    assert cfg.tk_far % TK_NEAR == 0 and NEAR_BACK == cfg.tk_far and NEAR_COLS == NEAR_BACK + LANES and tq <= LANES
    n_select = min(cfg.top_k_max, n_keys // 4)
    grp = cfg.n_heads // cfg.n_kv
    hq, hidx = cfg.hq, cfg.hidx
    assert cfg.off["q"] % hq == 0 and cfg.off["qi"] % hidx == 0
    assert cfg.head_dim == LANES and LANES % cfg.idx_dim == 0 and cfg.n_idx_heads % (LANES // cfg.idx_dim) == 0
    bias = _near_bias(cfg, rel_bias, tq)
    whole = lambda shape: pl.BlockSpec(shape, lambda b, i: (b,) + (0,) * (len(shape) - 1), pipeline_mode=pl.Buffered(1))
    return pl.pallas_call(
        functools.partial(_dsa_body, cfg, tq, past, n_select),
        grid=(n_seq, nq),
        in_specs=[pl.BlockSpec((tq, hq), lambda b, i: (b * nq + i, cfg.off["q"] // hq)),
                  pl.BlockSpec((tq, hidx), lambda b, i: (b * nq + i, cfg.off["qi"] // hidx)),
                  pl.BlockSpec((tq, LANES), lambda b, i: (b * nq + i, cfg.off["kiw"] // LANES)),
                  whole((None, lp, cfg.hkv)), whole((None, lp, cfg.hkv)), whole((None, lp, cfg.idx_dim)),
                  pl.BlockSpec(bias.shape, lambda b, i: (0, 0, 0), pipeline_mode=pl.Buffered(1))],
        out_specs=pl.BlockSpec((tq, hq), lambda b, i: (b * nq + i, 0)),
        out_shape=jax.ShapeDtypeStruct((m, hq), BF16),
        scratch_shapes=[pltpu.VMEM((lp, tq), I32),
                        pltpu.VMEM((cfg.n_kv, cfg.head_dim, grp * tq), BF16),
                        pltpu.VMEM((cfg.idx_dim, cfg.n_idx_heads * tq), BF16),
                        pltpu.VMEM((cfg.n_idx_heads, tq), F32),
                        pltpu.VMEM((cfg.n_kv, 1, grp * tq), F32), pltpu.VMEM((cfg.n_kv, 1, grp * tq), F32),
                        pltpu.VMEM((cfg.n_kv, cfg.head_dim, grp * tq), F32),
                        pltpu.VMEM((2, cfg.tk_far, grp * tq), F32)],
        compiler_params=_cparams(("parallel", "arbitrary")),
        name="dsa_attention",
    )(proj, proj, proj, k_all, v_all, kidx_all, bias)


def _pack_w_in(cfg, w_in):
    offs = np.cumsum(np.array(cfg.in_sizes))[:-1].tolist()
    q, k, v, qi, ki, wi, z, xbc, dt, ga, gs = jnp.split(w_in.T, offs, axis=0)
    row_pad = lambda a: jnp.pad(a, ((0, LANES - a.shape[0]), (0, 0)))
    packed = jnp.concatenate([q, k, v, qi, z, xbc, ga, gs, row_pad(jnp.concatenate([ki, wi], axis=0)), row_pad(dt)],
                             axis=0)
    return jnp.pad(packed, ((0, cfg.n_packed - packed.shape[0]), (0, 0))).astype(BF16)


def _pad_keys(a, lp):
    return jnp.pad(a, ((0, 0), (0, lp - a.shape[1]), (0, 0))).astype(BF16)


def _trunk_layer(cfg, x, mod, past_k, past_v, past_ik, ssm0, ssd_conv0, ffn_conv0, rel_bias, wts):
    bsz, t, d = x.shape
    m = bsz * t
    past = past_k.shape[1]
    x2d = x.reshape(m, d)
    off = cfg.off

    u = _norm_mod(cfg, x2d, wts["norm_mix_g"], mod, t, 0, 1)
    proj = _matmul_nt(u, wts["w_in_t"], cfg.tm, cfg.tn_in, F32, "in_proj")
    k_new = proj[:, off["k"]:off["k"] + cfg.hkv].reshape(bsz, t, cfg.hkv)
    v_new = proj[:, off["v"]:off["v"] + cfg.hkv].reshape(bsz, t, cfg.hkv)
    ki_new = proj[:, off["kiw"]:off["kiw"] + cfg.idx_dim].reshape(bsz, t, cfg.idx_dim)
    lp = -(-(past + t) // TK_IDX) * TK_IDX
    k_all = _pad_keys(jnp.concatenate([past_k.reshape(bsz, past, cfg.hkv), k_new], axis=1), lp)
    v_all = _pad_keys(jnp.concatenate([past_v.reshape(bsz, past, cfg.hkv), v_new], axis=1), lp)
    ki_all = _pad_keys(jnp.concatenate([past_ik, ki_new], axis=1), lp)
    attn = _dsa(cfg, proj, t, past, k_all, v_all, ki_all, rel_bias)

    state0_t = jnp.transpose(ssm0.astype(F32), (0, 3, 1, 2)).reshape(bsz, cfg.d_state, cfg.d_ssd)
    ssd_out, state_t, ssd_conv_new = _ssd(cfg, proj, t, ssd_conv0, state0_t, wts["ssd_conv_w"], wts["ssd_conv_b"],
                                          wts["dt_bias"], wts["a_log"], wts["d_skip"], wts["ssd_norm_g"])
    h_new = jnp.transpose(state_t.reshape(bsz, cfg.d_state, cfg.ssd_heads, cfg.ssd_head_dim), (0, 2, 3, 1))

    merged = _merge(cfg, attn, ssd_out, wts["w_attn_o"], wts["w_ssd_o"], proj)
    x1 = _gated_residual(merged, wts["w_out"], x2d, mod, t, 2, cfg.tm, cfg.tn, "out_proj_residual")

    u2 = _norm_mod(cfg, x1, wts["norm_ffn_g"], mod, t, 3, 4)
    h, ffn_conv_new = _ffn_up(cfg, u2, wts["w_up"], wts["ffn_conv_w"], wts["ffn_conv_b"], ffn_conv0, t)
    x2 = _gated_residual(h, wts["w_down"], x1, mod, t, 5, cfg.tm_down, cfg.tn_down, "down_proj_residual")
    states = (k_new.reshape(bsz, t, cfg.n_kv, cfg.head_dim), v_new.reshape(bsz, t, cfg.n_kv, cfg.head_dim), ki_new,
              h_new.astype(ssm0.dtype), ssd_conv_new, ffn_conv_new)
    return x2.reshape(bsz, t, d), states


def _forward(cfg, x_prompt, x_sample, c_prompt, c_sample, cache_k, cache_v, cache_idx_k, state_ssm, state_ssd_conv,
             state_ffn_conv, rel_bias, w_ada, b_ada, norm_mix_g, w_in, ssd_conv_w, ssd_conv_b, dt_bias, a_log, d_skip,
             ssd_norm_g, w_attn_o, w_ssd_o, w_out, norm_ffn_g, w_up, ffn_conv_w, ffn_conv_b, w_down, final_norm_g):
    depth = w_in.shape[0]
    bp, tp, d = x_prompt.shape
    bs, ts, _ = x_sample.shape
    dt_ = x_prompt.dtype
    hp, hs = x_prompt, x_sample
    c_all = jnp.concatenate([c_prompt, c_sample], axis=0)
    c_all = jnp.pad(c_all, ((0, -(bp + bs) % 8), (0, 0)))
    prompt_states, sample_states = [], []
    for l in range(depth):
        mod = _modulation(cfg, c_all, w_ada[l], b_ada[l]).reshape(c_all.shape[0], 6, d)
        wts = dict(norm_mix_g=norm_mix_g[l], w_in_t=_pack_w_in(cfg, w_in[l]), ssd_conv_w=ssd_conv_w[l],
                   ssd_conv_b=ssd_conv_b[l], dt_bias=dt_bias[l], a_log=a_log[l], d_skip=d_skip[l],
                   ssd_norm_g=ssd_norm_g[l], w_attn_o=w_attn_o[l].astype(BF16), w_ssd_o=w_ssd_o[l].astype(BF16),
                   w_out=w_out[l].astype(BF16), norm_ffn_g=norm_ffn_g[l], w_up=w_up[l],
                   ffn_conv_w=ffn_conv_w[l], ffn_conv_b=ffn_conv_b[l], w_down=w_down[l].astype(BF16))
        hp, st_p = _trunk_layer(cfg, hp, mod[:bp],
                                jnp.zeros((bp, 0, cfg.n_kv, cfg.head_dim), dt_),
                                jnp.zeros((bp, 0, cfg.n_kv, cfg.head_dim), dt_),
                                jnp.zeros((bp, 0, cfg.idx_dim), dt_),
                                jnp.zeros((bp, cfg.ssd_heads, cfg.ssd_head_dim, cfg.d_state), state_ssm.dtype),
                                jnp.zeros((bp, cfg.ssd_conv - 1, cfg.conv_dim), dt_),
                                jnp.zeros((bp, cfg.ffn_conv - 1, 2 * cfg.d_ff), dt_),
                                rel_bias, wts)
        hs, st_s = _trunk_layer(cfg, hs, mod[bp:bp + bs], cache_k[l], cache_v[l], cache_idx_k[l], state_ssm[l],
                                state_ssd_conv[l], state_ffn_conv[l], rel_bias, wts)
        prompt_states.append(st_p)
        sample_states.append(st_s)
    y_prompt = _final_norm(cfg, hp.reshape(bp * tp, d), final_norm_g).reshape(bp, tp, d)
    y_sample = _final_norm(cfg, hs.reshape(bs * ts, d), final_norm_g).reshape(bs, ts, d)
    stack = lambda states, i: jnp.stack([s[i] for s in states], axis=0)
    return (y_prompt, y_sample) + tuple(stack(prompt_states, i) for i in range(6)) + tuple(
        stack(sample_states, i) for i in range(6))


def kernel(x_prompt, x_sample, c_prompt, c_sample, cache_k, cache_v, cache_idx_k, state_ssm, state_ssd_conv,
           state_ffn_conv, rel_bias, w_ada, b_ada, norm_mix_g, w_in, ssd_conv_w, ssd_conv_b, dt_bias, a_log, d_skip,
           ssd_norm_g, w_attn_o, w_ssd_o, w_out, norm_ffn_g, w_up, ffn_conv_w, ffn_conv_b, w_down, final_norm_g):
    return _forward(Cfg(), x_prompt, x_sample, c_prompt, c_sample, cache_k, cache_v, cache_idx_k, state_ssm,
                    state_ssd_conv, state_ffn_conv, rel_bias, w_ada, b_ada, norm_mix_g, w_in, ssd_conv_w, ssd_conv_b,
                    dt_bias, a_log, d_skip, ssd_norm_g, w_attn_o, w_ssd_o, w_out, norm_ffn_g, w_up, ffn_conv_w,
                    ffn_conv_b, w_down, final_norm_g)
```

```python
import functools
import math

import numpy as np
import jax
import jax.numpy as jnp
from jax import lax
from jax.experimental import pallas as pl
from jax.experimental.pallas import tpu as pltpu

F32 = jnp.float32
BF16 = jnp.bfloat16
I32 = jnp.int32

LANES = 128
V7X_VMEM_BYTES = 64 * 1024 * 1024
VMEM_LIMIT = V7X_VMEM_BYTES - 8 * 1024 * 1024
INT_MIN = -(2 ** 31)
NEG_BIG = -1e30
M_INIT = -1e29
LOG2E = math.log2(math.e)
HIGHEST = lax.Precision.HIGHEST


class Cfg:
    def __init__(self, **kw):
        self.d_model = 4096
        self.chunk = 64
        self.n_heads = 16
        self.n_kv = 4
        self.head_dim = 128
        self.n_idx_heads = 16
        self.idx_dim = 64
        self.top_k_max = 256
        self.n_buckets = 32
        self.max_distance = 128
        self.ssd_heads = 32
        self.ssd_head_dim = 64
        self.ssd_groups = 4
        self.d_state = 128
        self.ssd_conv = 4
        self.d_ff = 11008
        self.ffn_conv = 3
        self.eps = 1e-6
        self.tm = 1024
        self.tn_in = 768
        self.tn = 512
        self.tn_ff = 256
        self.tm_down = 512
        self.tn_down = 512
        self.tr = 256
        self.tq = 128
        self.tk_far = 512
        self.ssd_chunk = 128
        self.tn_mod = 512
        for k, v in kw.items():
            assert hasattr(self, k), k
            setattr(self, k, v)
        self.d_ssd = self.ssd_heads * self.ssd_head_dim
        self.conv_dim = self.d_ssd + 2 * self.ssd_groups * self.d_state
        self.hq = self.n_heads * self.head_dim
        self.hkv = self.n_kv * self.head_dim
        self.hidx = self.n_idx_heads * self.idx_dim
        self.in_sizes = (self.hq, self.hkv, self.hkv, self.hidx, self.idx_dim, self.n_idx_heads, self.d_ssd,
                         self.conv_dim, self.ssd_heads, self.d_model, self.d_model)
        segs = [("q", self.hq), ("k", self.hkv), ("v", self.hkv), ("qi", self.hidx), ("z", self.d_ssd),
                ("xbc", self.conv_dim), ("ga", self.d_model), ("gs", self.d_model), ("kiw", LANES), ("dt", LANES)]
        off, self.off = 0, {}
        for name, width in segs:
            assert width % LANES == 0
            self.off[name] = off
            off += width
        self.n_packed = -(-off // self.tn_in) * self.tn_in
        assert self.idx_dim + self.n_idx_heads <= LANES and self.ssd_heads <= LANES


def _silu(x):
    half = 0.5 * x
    return half + half * jnp.tanh(half)


def _cparams(sem):
    return pltpu.CompilerParams(dimension_semantics=sem, vmem_limit_bytes=VMEM_LIMIT)


def _tile(n, pref):
    t = min(n, pref)
    assert n % t == 0, (n, pref)
    return t


def _seq_map(t_seq, tm):
    if t_seq >= tm:
        assert t_seq % tm == 0
        per = t_seq // tm
        return tm, 1, (lambda i: i // per)
    assert tm % t_seq == 0
    return t_seq, tm // t_seq, (lambda i: i)


def _mod_body(c_ref, w_ref, b_ref, o_ref):
    c = c_ref[...]
    a = (c * jax.nn.sigmoid(c)).astype(BF16)
    o_ref[...] = jnp.dot(a, w_ref[...].astype(BF16), preferred_element_type=F32) + b_ref[...]


def _modulation(cfg, c, w_ada, b_ada):
    rows, d = c.shape
    n = w_ada.shape[1]
    tn = _tile(n, cfg.tn_mod)
    return pl.pallas_call(
        _mod_body,
        grid=(n // tn,),
        in_specs=[pl.BlockSpec((rows, d), lambda j: (0, 0)),
                  pl.BlockSpec((d, tn), lambda j: (0, j)),
                  pl.BlockSpec((1, tn), lambda j: (0, j))],
        out_specs=pl.BlockSpec((rows, tn), lambda j: (0, j)),
        out_shape=jax.ShapeDtypeStruct((rows, n), F32),
        compiler_params=_cparams(("parallel",)),
        name="adaln_mod",
    )(c, w_ada, b_ada.reshape(1, n))


def _norm_mod_body(eps, sh_row, sc_row, x_ref, g_ref, mod_ref, o_ref):
    x = x_ref[...]
    y = x * lax.rsqrt(jnp.mean(x * x, axis=-1, keepdims=True) + eps) * g_ref[...]
    y = y * (1.0 + mod_ref[sc_row:sc_row + 1, :]) + mod_ref[sh_row:sh_row + 1, :]
    o_ref[...] = y.astype(o_ref.dtype)


def _norm_mod(cfg, x, g, mod, t_seq, sh_row, sc_row):
    m, d = x.shape
    tr = _tile(t_seq, cfg.tr)
    per = t_seq // tr
    return pl.pallas_call(
        functools.partial(_norm_mod_body, cfg.eps, sh_row, sc_row),
        grid=(m // tr,),
        in_specs=[pl.BlockSpec((tr, d), lambda i: (i, 0)),
                  pl.BlockSpec((1, d), lambda i: (0, 0)),
                  pl.BlockSpec((None, 6, d), lambda i: (i // per, 0, 0))],
        out_specs=pl.BlockSpec((tr, d), lambda i: (i, 0)),
        out_shape=jax.ShapeDtypeStruct((m, d), BF16),
        compiler_params=_cparams(("parallel",)),
        name="rmsnorm_mod",
    )(x, g.reshape(1, d), mod)


def _norm_body(eps, x_ref, g_ref, o_ref):
    x = x_ref[...]
    o_ref[...] = x * lax.rsqrt(jnp.mean(x * x, axis=-1, keepdims=True) + eps) * g_ref[...]


def _final_norm(cfg, x, g):
    m, d = x.shape
    tr = _tile(m, cfg.tr)
    return pl.pallas_call(
        functools.partial(_norm_body, cfg.eps),
        grid=(m // tr,),
        in_specs=[pl.BlockSpec((tr, d), lambda i: (i, 0)), pl.BlockSpec((1, d), lambda i: (0, 0))],
        out_specs=pl.BlockSpec((tr, d), lambda i: (i, 0)),
        out_shape=jax.ShapeDtypeStruct((m, d), F32),
        compiler_params=_cparams(("parallel",)),
        name="final_rmsnorm",
    )(x, g.reshape(1, d))


def _mm_nt_body(a_ref, wt_ref, o_ref):
    o_ref[...] = lax.dot_general(a_ref[...], wt_ref[...], (((1,), (1,)), ((), ())),
                                 preferred_element_type=F32).astype(o_ref.dtype)


def _matmul_nt(a, w_t, tm, tn, out_dtype, name):
    m, k = a.shape
    n = w_t.shape[0]
    tm, tn = _tile(m, tm), _tile(n, tn)
    return pl.pallas_call(
        _mm_nt_body,
        grid=(m // tm, n // tn),
        in_specs=[pl.BlockSpec((tm, k), lambda i, j: (i, 0)), pl.BlockSpec((tn, k), lambda i, j: (j, 0))],
        out_specs=pl.BlockSpec((tm, tn), lambda i, j: (i, j)),
        out_shape=jax.ShapeDtypeStruct((m, n), out_dtype),
        compiler_params=_cparams(("parallel", "parallel")),
        name=name,
    )(a, w_t)


def _merge_body(attn_ref, ssd_ref, wa_ref, ws_ref, ga_ref, gs_ref, o_ref):
    a = jnp.dot(attn_ref[...], wa_ref[...], preferred_element_type=F32)
    s = jnp.dot(ssd_ref[...], ws_ref[...], preferred_element_type=F32)
    o_ref[...] = (jax.nn.sigmoid(ga_ref[...]) * a + jax.nn.sigmoid(gs_ref[...]) * s).astype(o_ref.dtype)


def _merge(cfg, attn, ssd, w_attn_o, w_ssd_o, proj):
    m = attn.shape[0]
    d = cfg.d_model
    tm = _tile(m, cfg.tm)
    tn = math.gcd(math.gcd(cfg.off["ga"], cfg.off["gs"]), _tile(d, cfg.tn))
    ga0, gs0 = cfg.off["ga"] // tn, cfg.off["gs"] // tn
    return pl.pallas_call(
        _merge_body,
        grid=(m // tm, d // tn),
        in_specs=[pl.BlockSpec((tm, attn.shape[1]), lambda i, j: (i, 0)),
                  pl.BlockSpec((tm, ssd.shape[1]), lambda i, j: (i, 0)),
                  pl.BlockSpec((attn.shape[1], tn), lambda i, j: (0, j)),
                  pl.BlockSpec((ssd.shape[1], tn), lambda i, j: (0, j)),
                  pl.BlockSpec((tm, tn), lambda i, j: (i, ga0 + j)),
                  pl.BlockSpec((tm, tn), lambda i, j: (i, gs0 + j))],
        out_specs=pl.BlockSpec((tm, tn), lambda i, j: (i, j)),
        out_shape=jax.ShapeDtypeStruct((m, d), BF16),
        compiler_params=_cparams(("parallel", "parallel")),
        name="branch_merge",
    )(attn, ssd, w_attn_o, w_ssd_o, proj, proj)


def _resid_body(rows, groups, gate_row, a_ref, w_ref, x_ref, mod_ref, o_ref):
    acc = jnp.dot(a_ref[...], w_ref[...], preferred_element_type=F32)
    for g in range(groups):
        sl = slice(g * rows, (g + 1) * rows)
        o_ref[sl, :] = x_ref[sl, :] + mod_ref[g, gate_row:gate_row + 1, :] * acc[sl, :]


def _gated_residual(a, w, x, mod, t_seq, gate_row, tm, tn, name):
    m, k = a.shape
    n = w.shape[1]
    tm, tn = _tile(m, tm), _tile(n, tn)
    rows, groups, seq_of = _seq_map(t_seq, tm)
    return pl.pallas_call(
        functools.partial(_resid_body, rows, groups, gate_row),
        grid=(m // tm, n // tn),
        in_specs=[pl.BlockSpec((tm, k), lambda i, j: (i, 0)),
                  pl.BlockSpec((k, tn), lambda i, j: (0, j)),
                  pl.BlockSpec((tm, tn), lambda i, j: (i, j)),
                  pl.BlockSpec((groups, 6, tn), lambda i, j: (seq_of(i), 0, j))],
        out_specs=pl.BlockSpec((tm, tn), lambda i, j: (i, j)),
        out_shape=jax.ShapeDtypeStruct((m, n), F32),
        compiler_params=_cparams(("parallel", "parallel")),
        name=name,
    )(a, w, x, mod)


SUBLANES = 8


def _shifted_rows(x, prev, n):
    rid = lax.broadcasted_iota(I32, (SUBLANES, x.shape[1]), 0)
    out = []
    for k in range(1, n + 1):
        rolled = pltpu.roll(x, k, 0)
        head = rolled[0:SUBLANES, :]
        for r in range(k):
            head = jnp.where(rid == r, prev[n - k + r:n - k + r + 1, :], head)
        out.append(jnp.concatenate([head, rolled[SUBLANES:, :]], axis=0))
    return out


def _up_body(rows, groups, tiles_per_seq, a_ref, wg_ref, wv_ref, cwg_ref, cwv_ref, cbg_ref, cbv_ref, hg_ref, hv_ref,
             h_ref, ng_ref, nv_ref, up_ref, carry_ref, w_scr):
    i = pl.program_id(1)

    @pl.when(i == 0)
    def _():
        w_scr[0] = wg_ref[...].astype(w_scr.dtype)
        w_scr[1] = wv_ref[...].astype(w_scr.dtype)

    a = a_ref[...]
    stride = rows + SUBLANES
    hc = []
    for half, (cw_ref, cb_ref, hist_ref, new_ref) in enumerate(
            ((cwg_ref, cbg_ref, hg_ref, ng_ref), (cwv_ref, cbv_ref, hv_ref, nv_ref))):
        up = jnp.dot(a, w_scr[half], preferred_element_type=F32)
        outs = []
        for g in range(groups):
            base = g * stride + SUBLANES
            prev = hist_ref[g]
            if groups == 1 and tiles_per_seq > 1:
                prev = jnp.where(i % tiles_per_seq == 0, prev, carry_ref[half])
            up_ref[half, base - 2:base, :] = prev
            up_ref[half, base:base + rows, :] = up[g * rows:(g + 1) * rows, :]
            last = up[(g + 1) * rows - 2:(g + 1) * rows, :]
            new_ref[g] = last
            if groups == 1 and tiles_per_seq > 1:
                carry_ref[half] = last
            taps =[up_ref[half, base - k:base - k + rows, :] for k in (2, 1, 0)]
            outs.append(cw_ref[0:1, :] * taps[0] + cw_ref[1:2, :] * taps[1] + cw_ref[2:3, :] * taps[2] + cb_ref[...])
        hc.append(outs)
    for g in range(groups):
        gate, val = hc[0][g], hc[1][g]
        h_ref[g * rows:(g + 1) * rows, :] = (_silu(gate) * val).astype(h_ref.dtype)


def _ffn_up(cfg, u2, w_up, conv_w, conv_b, hist, t_seq):
    m, d = u2.shape
    nf = cfg.d_ff
    tm, tn = _tile(m, cfg.tm), _tile(nf, cfg.tn_ff)
    rows, groups, seq_of = _seq_map(t_seq, tm)
    nj = nf // tn
    wspec = lambda half: pl.BlockSpec((d, tn), lambda j, i: (0, j + half * nj))
    cspec = lambda r, half: pl.BlockSpec((r, tn), lambda j, i: (0, j + half * nj))
    hspec = lambda half: pl.BlockSpec((groups, 2, tn), lambda j, i: (seq_of(i), 0, j + half * nj))
    ospec = pl.BlockSpec((groups, 2, tn), lambda j, i: (i, 0, j))
    h, new_g, new_v = pl.pallas_call(
        functools.partial(_up_body, rows, groups, max(1, t_seq // tm)),
        grid=(nj, m // tm),
        in_specs=[pl.BlockSpec((tm, d), lambda j, i: (i, 0)), wspec(0), wspec(1),
                  cspec(3, 0), cspec(3, 1), cspec(1, 0), cspec(1, 1), hspec(0), hspec(1)],
        out_specs=[pl.BlockSpec((tm, tn), lambda j, i: (i, j)), ospec, ospec],
        out_shape=[jax.ShapeDtypeStruct((m, nf), BF16),
                   jax.ShapeDtypeStruct((m // rows, 2, nf), F32),
                   jax.ShapeDtypeStruct((m // rows, 2, nf), F32)],
        scratch_shapes=[pltpu.VMEM((2, groups * (rows + SUBLANES), tn), F32), pltpu.VMEM((2, 2, tn), F32),
                        pltpu.VMEM((2, d, tn), BF16)],
        compiler_params=_cparams(("arbitrary", "arbitrary")),
        name="ffn_up_conv_swiglu",
    )(u2, w_up, w_up, conv_w, conv_w, conv_b.reshape(1, -1), conv_b.reshape(1, -1), hist, hist)
    per_seq = t_seq // rows
    return h, jnp.concatenate([new_g[per_seq - 1::per_seq], new_v[per_seq - 1::per_seq]], axis=-1)


def _ssd_body(cfg, lc, z_ref, xbc_ref, dt_ref, hist_ref, s0_ref, cw_ref, cb_ref, dtb_ref, alog_ref, dskip_ref,
              ng_ref, expand_ref, y_ref, state_ref, cnew_ref, xcarry_ref, ydiag_ref):
    nh, hp, ng, ns = cfg.ssd_heads, cfg.ssd_head_dim, cfg.ssd_groups, cfg.d_state
    ds = cfg.d_ssd
    per_group = nh // ng
    c = pl.program_id(1)

    @pl.when(c == 0)
    def _():
        state_ref[...] = s0_ref[...]
        xcarry_ref[0:3, :] = hist_ref[...]

    x = xbc_ref[...]
    prev = xcarry_ref[0:3, :]
    s1, s2, s3 = _shifted_rows(x, prev, 3)
    xc =cw_ref[0:1, :] * s3 + cw_ref[1:2, :] * s2 + cw_ref[2:3, :] * s1 + cw_ref[3:4, :] * x + cb_ref[...]
    xc = _silu(xc)
    last = x[lc - 3:lc, :]
    xcarry_ref[0:3, :] = last
    cnew_ref[...] = last
    xs = xc[:, :ds]
    bm = xc[:, ds:ds + ng * ns].astype(BF16)
    cm = xc[:, ds + ng * ns:].astype(BF16)

    raw = dt_ref[...] + dtb_ref[...]
    dt = jnp.maximum(raw, 0.0) + jnp.log1p(jnp.exp(-jnp.abs(raw)))
    a = dt * (-jnp.exp(alog_ref[...]))
    ri = lax.broadcasted_iota(I32, (lc, lc), 0)
    ci = lax.broadcasted_iota(I32, (lc, lc), 1)
    causal = ri >= ci
    a_cum = jnp.dot(causal.astype(F32), a, precision=HIGHEST, preferred_element_type=F32)
    eye = (lax.broadcasted_iota(I32, (LANES, LANES), 0) == lax.broadcasted_iota(I32, (LANES, LANES), 1))
    nt = (((1,), (1,)), ((), ()))
    a_cum_t = lax.dot_general(eye.astype(F32), a_cum, nt, precision=HIGHEST, preferred_element_type=F32)
    both = jnp.concatenate([dt, a_cum], axis=0)
    hi = both.astype(BF16)
    rest = both - hi.astype(F32)
    mid = rest.astype(BF16)
    lo = (rest - mid.astype(F32)).astype(BF16)
    wide = jnp.dot(jnp.concatenate([hi, mid, lo], axis=1), expand_ref[...], preferred_element_type=F32)
    dt_x, a_x = wide[:lc, :], wide[lc:, :]
    a_end = a_x[lc - 1:lc, :]
    xd = xs * dt_x

    for g in range(ng):
        cb = lax.dot_general(cm[:, g * ns:(g + 1) * ns], bm[:, g * ns:(g + 1) * ns], nt, preferred_element_type=F32)
        for r in range(per_group):
            h = g * per_group + r
            seg = a_cum[:, h:h + 1] - a_cum_t[h:h + 1, :]
            lmat = jnp.exp(jnp.where(causal, seg, -jnp.inf))
            ydiag_ref[:, h * hp:(h + 1) * hp] = jnp.dot((cb * lmat).astype(BF16), xd[:, h * hp:(h + 1) * hp].astype(BF16),
                                                        preferred_element_type=F32)

    state = state_ref[...]
    xdd = (xd * jnp.exp(a_end - a_x)).astype(BF16)
    w = per_group * hp
    y_off, new_cols = [], []
    for g in range(ng):
        y_off.append(jnp.dot(cm[:, g * ns:(g + 1) * ns], state[:, g * w:(g + 1) * w].astype(BF16),
                             preferred_element_type=F32))
        b_t = lax.dot_general(eye.astype(BF16), bm[:, g * ns:(g + 1) * ns], nt, preferred_element_type=F32).astype(BF16)
        new_cols.append(jnp.dot(b_t, xdd[:, g * w:(g + 1) * w], preferred_element_type=F32))
    state_ref[...] = state * jnp.exp(a_end) + jnp.concatenate(new_cols, axis=1)
    y = ydiag_ref[...] + jnp.concatenate(y_off, axis=1) * jnp.exp(a_x) + dskip_ref[...] * xs

    z = z_ref[...]
    y = y * _silu(z)
    gw = ds // ng
    for g in range(ng):
        yg = y[:, g * gw:(g + 1) * gw]
        yg = yg * lax.rsqrt(jnp.mean(yg * yg, axis=-1, keepdims=True) + cfg.eps)
        y_ref[:, g * gw:(g + 1) * gw] = (yg * ng_ref[:, g * gw:(g + 1) * gw]).astype(y_ref.dtype)


def _ssd(cfg, proj, t_seq, hist, state0_t, conv_w, conv_b, dt_bias, a_log, d_skip, norm_g):
    m = proj.shape[0]
    n_seq = m // t_seq
    lc = _tile(t_seq, cfg.ssd_chunk)
    nc = t_seq // lc
    ds, cd, ns = cfg.d_ssd, cfg.conv_dim, cfg.d_state
    assert cfg.off["z"] % ds == 0 and cfg.off["xbc"] % cd == 0
    zb, xb, db = cfg.off["z"] // ds, cfg.off["xbc"] // cd, cfg.off["dt"] // LANES
    pad = LANES - cfg.ssd_heads
    lane_row = lambda v: jnp.pad(v.astype(F32), (0, pad)).reshape(1, LANES)
    expand = (jnp.arange(LANES)[:, None] == (jnp.arange(ds) // cfg.ssd_head_dim)[None, :]).astype(BF16)
    expand = jnp.concatenate([expand] * 3, axis=0)
    const = lambda shape: pl.BlockSpec(shape, lambda b, c: (0,) * len(shape))
    return pl.pallas_call(
        functools.partial(_ssd_body, cfg, lc),
        grid=(n_seq, nc),
        in_specs=[pl.BlockSpec((lc, ds), lambda b, c: (b * nc + c, zb)),
                  pl.BlockSpec((lc, cd), lambda b, c: (b * nc + c, xb)),
                  pl.BlockSpec((lc, LANES), lambda b, c: (b * nc + c, db)),
                  pl.BlockSpec((None, 3, cd), lambda b, c: (b, 0, 0)),
                  pl.BlockSpec((None, ns, ds), lambda b, c: (b, 0, 0)),
                  const((cfg.ssd_conv, cd)), const((1, cd)), const((1, LANES)), const((1, LANES)), const((1, ds)),
                  const((1, ds)), const((3 * LANES, ds))],
        out_specs=[pl.BlockSpec((lc, ds), lambda b, c: (b * nc + c, 0)),
                   pl.BlockSpec((None, ns, ds), lambda b, c: (b, 0, 0)),
                   pl.BlockSpec((None, 3, cd), lambda b, c: (b, 0, 0))],
        out_shape=[jax.ShapeDtypeStruct((m, ds), BF16),
                   jax.ShapeDtypeStruct((n_seq, ns, ds), F32),
                   jax.ShapeDtypeStruct((n_seq, 3, cd), F32)],
        scratch_shapes=[pltpu.VMEM((8, cd), F32), pltpu.VMEM((lc, ds), F32)],
        compiler_params=_cparams(("parallel", "arbitrary")),
        name="ssd_scan",
    )(proj, proj, proj, hist, state0_t, conv_w, conv_b.reshape(1, cd), lane_row(dt_bias), lane_row(a_log),
      jnp.repeat(d_skip.astype(F32), cfg.ssd_head_dim).reshape(1, ds), norm_g.reshape(1, ds), expand)


NEAR_COLS = 640
NEAR_BACK = 512
TK_NEAR = 128
TK_IDX = 512
SEARCH_BITS_FIRST = 27


def _t5_bucket(cfg, rel):
    nb = cfg.n_buckets // 2
    max_exact = nb // 2
    ret = jnp.where(rel > 0, nb, 0)
    n = jnp.abs(rel)
    nf = jnp.maximum(n, 1).astype(F32)
    large = max_exact + (jnp.log(nf / max_exact) / math.log(cfg.max_distance / max_exact) * (nb - max_exact)).astype(I32)
    large = jnp.minimum(large, nb - 1)
    return ret + jnp.where(n < max_exact, n, large)


def _near_bias(cfg, rel_bias, tq):
    assert cfg.max_distance <= LANES
    rel = jnp.arange(NEAR_COLS, dtype=I32)[:, None] - NEAR_BACK - jnp.arange(tq, dtype=I32)[None, :]
    far = rel_bias[_t5_bucket(cfg, jnp.asarray(-cfg.max_distance, I32))].astype(F32)
    onehot = (_t5_bucket(cfg, rel)[..., None] == jnp.arange(cfg.n_buckets, dtype=I32)).astype(F32)
    tab = jnp.einsum("ctb,bh->cth", onehot, rel_bias.astype(F32), precision=HIGHEST)
    tab = (tab - far) * LOG2E
    grp = cfg.n_heads // cfg.n_kv
    tab = tab.reshape(NEAR_COLS, tq, cfg.n_kv, grp)
    return jnp.transpose(tab, (2, 0, 3, 1)).reshape(cfg.n_kv, NEAR_COLS, grp * tq)


def _dsa_body(cfg, tq, past, n_select, q_ref, qi_ref, kiw_ref, k_ref, v_ref, kidx_ref, bias_ref, o_ref,
              keys_ref, qs_ref, qis_ref, w_ref, m_ref, l_ref, acc_ref, s_ref, thr_ref):
    nkv, hd, di, nih = cfg.n_kv, cfg.head_dim, cfg.idx_dim, cfg.n_idx_heads
    grp = cfg.n_heads // nkv
    gw = grp * tq
    nt = (((1,), (1,)), ((), ()))
    tn = (((0,), (0,)), ((), ()))
    eye = lax.broadcasted_iota(I32, (LANES, LANES), 0) == lax.broadcasted_iota(I32, (LANES, LANES), 1)
    eye_bf = eye.astype(BF16)
    x0 = past + pl.program_id(1) * tq
    k_end = x0 + tq
    n_idx = (k_end + TK_IDX - 1) // TK_IDX

    q = (q_ref[...] * ((hd ** -0.5) * LOG2E)).astype(BF16)
    for n in range(nkv):
        for g in range(grp):
            h = n * grp + g
            qs_ref[n, :, g * tq:(g + 1) * tq] = lax.dot_general(
                eye_bf, q[:, h * hd:(h + 1) * hd], nt, preferred_element_type=F32).astype(BF16)
    qi = qi_ref[...].astype(BF16)
    per_blk = LANES // di
    for j in range(nih // per_blk):
        t_blk = lax.dot_general(eye_bf, qi[:, j * LANES:(j + 1) * LANES], nt, preferred_element_type=F32)
        for r in range(per_blk):
            h = j * per_blk + r
            qis_ref[:, h * tq:(h + 1) * tq] = t_blk[r * di:(r + 1) * di, :].astype(BF16)
    kiw_t = lax.dot_general(eye.astype(F32), kiw_ref[...], nt, precision=HIGHEST, preferred_element_type=F32)
    w_ref[...] = kiw_t[di:di + nih, :] * ((di ** -0.5) * (nih ** -0.5))

    qpos = x0 + lax.broadcasted_iota(I32, (1, tq), 1)
    limit = (jnp.right_shift(qpos, int(math.log2(cfg.chunk))) + 1) * cfg.chunk
    krow = lax.broadcasted_iota(I32, (TK_IDX, tq), 0)

    def idx_tile(t, carry):
        k0 = pl.multiple_of(t * TK_IDX, TK_IDX)
        logits = jnp.dot(kidx_ref[pl.ds(k0, TK_IDX), :], qis_ref[...], preferred_element_type=F32)
        sc = jnp.zeros((TK_IDX, tq), F32)
        for h in range(nih):
            sc = sc + w_ref[h:h + 1, :] * jnp.maximum(logits[:, h * tq:(h + 1) * tq], 0.0)
        bits = pltpu.bitcast(jnp.where(sc == 0.0, 0.0, sc), I32)
        key = jnp.where(bits < 0, bits ^ 0x7FFFFFFF, bits)
        keys_ref[pl.ds(k0, TK_IDX), :] = jnp.where(k0 + krow < limit, key, INT_MIN)
        return carry

    lax.fori_loop(0, n_idx, idx_tile, 0)

    def count(pred):
        def body(t, accs):
            k0 = pl.multiple_of(t * TK_IDX, TK_IDX)
            blk = keys_ref[pl.ds(k0, TK_IDX), :]
            accs = list(accs)
            for r in range(TK_IDX // SUBLANES):
                hit = pred(blk[r * SUBLANES:(r + 1) * SUBLANES, :])
                accs[r % len(accs)] = accs[r % len(accs)] + jnp.where(hit, 1.0, 0.0)
            return tuple(accs)

        accs = lax.fori_loop(0, n_idx, body, (jnp.zeros((SUBLANES, tq), F32),) * 8)
        return jnp.sum(functools.reduce(lambda a, b: a + b, accs), axis=0, keepdims=True)

    settled0 = jnp.where(limit < n_select, 1.0, 0.0)

    def bit_step(s, c):
        thr, settled = c
        bit = 31 - s
        cand = jnp.where(bit == 31, jnp.zeros_like(thr), thr | jnp.left_shift(jnp.int32(1), bit))
        cand8 = jnp.broadcast_to(cand, (SUBLANES, tq))
        kept = count(lambda b: b >= cand8)
        thr = jnp.where(kept >= float(n_select), cand, thr)
        return thr, jnp.maximum(settled, jnp.where(kept == float(n_select), 1.0, 0.0))

    thr, settled = lax.fori_loop(0, SEARCH_BITS_FIRST, bit_step, (jnp.full((1, tq), INT_MIN, I32), settled0))
    thr_ref[...] = thr

    @pl.when(jnp.sum(1.0 - settled) > 0.0)
    def _():
        thr = lax.fori_loop(SEARCH_BITS_FIRST, 32, bit_step, (thr_ref[...], settled))[0]
        thr_ref[...] = thr
        thr = jnp.maximum(thr, INT_MIN + 1)
        thr8 = jnp.broadcast_to(thr, (SUBLANES, tq))
        need = float(n_select) - count(lambda b: b > thr8)
        n_tied = count(lambda b: b == thr8)

        @pl.when(jnp.max(n_tied - need) > 0.0)
        def _():
            tri = (lax.broadcasted_iota(I32, (TK_IDX, TK_IDX), 0) >= lax.broadcasted_iota(I32, (TK_IDX, TK_IDX), 1))
            tri = tri.astype(BF16)

            def drop_late_ties(t, seen):
                k0 = pl.multiple_of(t * TK_IDX, TK_IDX)
                blk = keys_ref[pl.ds(k0, TK_IDX), :]
                tied = blk == thr
                rank = seen + jnp.dot(tri, jnp.where(tied, 1.0, 0.0).astype(BF16), preferred_element_type=F32)
                keys_ref[pl.ds(k0, TK_IDX), :] = jnp.where(tied, jnp.where(rank > need, INT_MIN, blk), blk)
                return rank[TK_IDX - 1:TK_IDX, :]

            lax.fori_loop(0, n_idx, drop_late_ties, jnp.zeros((1, tq), F32))

    thr = jnp.maximum(thr_ref[...], INT_MIN + 1)

    far_end = jnp.maximum(x0 - LANES, 0) // cfg.tk_far * cfg.tk_far
    n_far = far_end // cfg.tk_far
    n_near = ((k_end + TK_NEAR - 1) // TK_NEAR * TK_NEAR - far_end) // TK_NEAR
    m_ref[...] = jnp.full(m_ref.shape, M_INIT, F32)
    l_ref[...] = jnp.zeros(l_ref.shape, F32)
    acc_ref[...] = jnp.zeros(acc_ref.shape, F32)

    def key_tiles(n_tiles, first, width, with_bias):
        ahead = nkv % 2 == 0

        def logits(n, k0):
            s = jnp.dot(k_ref[pl.ds(k0, width), n * hd:(n + 1) * hd], qs_ref[n], preferred_element_type=F32)
            if with_bias:
                s = s + bias_ref[n, pl.ds(pl.multiple_of(k0 - (x0 - NEAR_BACK), TK_NEAR), width), :]
            sel = keys_ref[pl.ds(k0, width), :] >= thr
            for g in range(grp):
                s_ref[n % 2, 0:width, g * tq:(g + 1) * tq] = jnp.where(sel, s[:, g * tq:(g + 1) * tq], NEG_BIG)

        start = lambda t: pl.multiple_of(first + t * width, TK_NEAR)
        if ahead:
            logits(0, start(0))

        def body(t, carry):
            k0 = start(t)
            if not ahead:
                logits(0, k0)
            for n in range(nkv):
                if n + 1 < nkv:
                    logits(n + 1, k0)
                elif ahead:
                    logits(0, start(jnp.minimum(t + 1, n_tiles - 1)))
                s = s_ref[n % 2, 0:width, :]
                m_prev = m_ref[n]
                m_new = jnp.maximum(m_prev, jnp.max(s, axis=0, keepdims=True))
                alpha = jnp.exp2(m_prev - m_new)
                p = jnp.exp2(s - m_new)
                l_ref[n] = alpha * l_ref[n] + jnp.sum(p, axis=0, keepdims=True)
                acc_ref[n] = alpha * acc_ref[n] + lax.dot_general(
                    v_ref[pl.ds(k0, width), n * hd:(n + 1) * hd], p.astype(BF16), tn, preferred_element_type=F32)
                m_ref[n] = m_new
            return carry

        lax.fori_loop(0, n_tiles, body, 0)

    key_tiles(n_far, 0, cfg.tk_far, False)
    key_tiles(n_near, far_end, TK_NEAR, True)
    for n in range(nkv):
        out_t = (acc_ref[n] / l_ref[n]).astype(BF16)
        for g in range(grp):
            h = n * grp + g
            o_ref[:, h * hd:(h + 1) * hd] = lax.dot_general(
                out_t[:, g * tq:(g + 1) * tq], eye_bf, tn, preferred_element_type=F32).astype(o_ref.dtype)


def _dsa(cfg, proj, t_seq, past, k_all, v_all, kidx_all, rel_bias):
    m = proj.shape[0]
    n_seq = m // t_seq
    tq = _tile(t_seq, cfg.tq)
    nq = t_seq // tq
    n_keys = past + t_seq
    lp = k_all.shape[1]
    assert tq % cfg.chunk == 0 and past % LANES == 0 and lp % TK_IDX == 0 and lp >= n_keys
    assert cfg.tk_far % TK_NEAR == 0 and NEAR_BACK == cfg.tk_far and NEAR_COLS == NEAR_BACK + LANES and tq <= LANES
    n_select = min(cfg.top_k_max, n_keys // 4)
    grp = cfg.n_heads // cfg.n_kv
    hq, hidx = cfg.hq, cfg.hidx
    assert cfg.off["q"] % hq == 0 and cfg.off["qi"] % hidx == 0
    assert cfg.head_dim == LANES and LANES % cfg.idx_dim == 0 and cfg.n_idx_heads % (LANES // cfg.idx_dim) == 0
    bias = _near_bias(cfg, rel_bias, tq)
    whole = lambda shape: pl.BlockSpec(shape, lambda b, i: (b,) + (0,) * (len(shape) - 1), pipeline_mode=pl.Buffered(1))
    return pl.pallas_call(
        functools.partial(_dsa_body, cfg, tq, past, n_select),
        grid=(n_seq, nq),
        in_specs=[pl.BlockSpec((tq, hq), lambda b, i: (b * nq + i, cfg.off["q"] // hq)),
                  pl.BlockSpec((tq, hidx), lambda b, i: (b * nq + i, cfg.off["qi"] // hidx)),
                  pl.BlockSpec((tq, LANES), lambda b, i: (b * nq + i, cfg.off["kiw"] // LANES)),
                  whole((None, lp, cfg.hkv)), whole((None, lp, cfg.hkv)), whole((None, lp, cfg.idx_dim)),
                  pl.BlockSpec(bias.shape, lambda b, i: (0, 0, 0), pipeline_mode=pl.Buffered(1))],
        out_specs=pl.BlockSpec((tq, hq), lambda b, i: (b * nq + i, 0)),
        out_shape=jax.ShapeDtypeStruct((m, hq), BF16),
        scratch_shapes=[pltpu.VMEM((lp, tq), I32),
                        pltpu.VMEM((cfg.n_kv, cfg.head_dim, grp * tq), BF16),
                        pltpu.VMEM((cfg.idx_dim, cfg.n_idx_heads * tq), BF16),
                        pltpu.VMEM((cfg.n_idx_heads, tq), F32),
                        pltpu.VMEM((cfg.n_kv, 1, grp * tq), F32), pltpu.VMEM((cfg.n_kv, 1, grp * tq), F32),
                        pltpu.VMEM((cfg.n_kv, cfg.head_dim, grp * tq), F32),
                        pltpu.VMEM((2, cfg.tk_far, grp * tq), F32),
                        pltpu.VMEM((1, tq), I32)],
        compiler_params=_cparams(("parallel", "arbitrary")),
        name="dsa_attention",
    )(proj, proj, proj, k_all, v_all, kidx_all, bias)


def _pack_w_in(cfg, w_in):
    offs = np.cumsum(np.array(cfg.in_sizes))[:-1].tolist()
    q, k, v, qi, ki, wi, z, xbc, dt, ga, gs = jnp.split(w_in.T, offs, axis=0)
    row_pad = lambda a: jnp.pad(a, ((0, LANES - a.shape[0]), (0, 0)))
    packed = jnp.concatenate([q, k, v, qi, z, xbc, ga, gs, row_pad(jnp.concatenate([ki, wi], axis=0)), row_pad(dt)],
                             axis=0)
    return jnp.pad(packed, ((0, cfg.n_packed - packed.shape[0]), (0, 0))).astype(BF16)


def _pad_keys(a, lp):
    return jnp.pad(a, ((0, 0), (0, lp - a.shape[1]), (0, 0))).astype(BF16)


def _trunk_layer(cfg, x, mod, past_k, past_v, past_ik, ssm0, ssd_conv0, ffn_conv0, rel_bias, wts):
    bsz, t, d = x.shape
    m = bsz * t
    past = past_k.shape[1]
    x2d = x.reshape(m, d)
    off = cfg.off

    u = _norm_mod(cfg, x2d, wts["norm_mix_g"], mod, t, 0, 1)
    proj = _matmul_nt(u, wts["w_in_t"], cfg.tm, cfg.tn_in, F32, "in_proj")
    k_new = proj[:, off["k"]:off["k"] + cfg.hkv].reshape(bsz, t, cfg.hkv)
    v_new = proj[:, off["v"]:off["v"] + cfg.hkv].reshape(bsz, t, cfg.hkv)
    ki_new = proj[:, off["kiw"]:off["kiw"] + cfg.idx_dim].reshape(bsz, t, cfg.idx_dim)
    lp = -(-(past + t) // TK_IDX) * TK_IDX
    k_all = _pad_keys(jnp.concatenate([past_k.reshape(bsz, past, cfg.hkv), k_new], axis=1), lp)
    v_all = _pad_keys(jnp.concatenate([past_v.reshape(bsz, past, cfg.hkv), v_new], axis=1), lp)
    ki_all = _pad_keys(jnp.concatenate([past_ik, ki_new], axis=1), lp)
    attn = _dsa(cfg, proj, t, past, k_all, v_all, ki_all, rel_bias)

    state0_t = jnp.transpose(ssm0.astype(F32), (0, 3, 1, 2)).reshape(bsz, cfg.d_state, cfg.d_ssd)
    ssd_out, state_t, ssd_conv_new = _ssd(cfg, proj, t, ssd_conv0, state0_t, wts["ssd_conv_w"], wts["ssd_conv_b"],
                                          wts["dt_bias"], wts["a_log"], wts["d_skip"], wts["ssd_norm_g"])
    h_new = jnp.transpose(state_t.reshape(bsz, cfg.d_state, cfg.ssd_heads, cfg.ssd_head_dim), (0, 2, 3, 1))

    merged = _merge(cfg, attn, ssd_out, wts["w_attn_o"], wts["w_ssd_o"], proj)
    x1 = _gated_residual(merged, wts["w_out"], x2d, mod, t, 2, cfg.tm, cfg.tn, "out_proj_residual")

    u2 = _norm_mod(cfg, x1, wts["norm_ffn_g"], mod, t, 3, 4)
    h, ffn_conv_new = _ffn_up(cfg, u2, wts["w_up"], wts["ffn_conv_w"], wts["ffn_conv_b"], ffn_conv0, t)
    x2 = _gated_residual(h, wts["w_down"], x1, mod, t, 5, cfg.tm_down, cfg.tn_down, "down_proj_residual")
    states = (k_new.reshape(bsz, t, cfg.n_kv, cfg.head_dim), v_new.reshape(bsz, t, cfg.n_kv, cfg.head_dim), ki_new,
              h_new.astype(ssm0.dtype), ssd_conv_new, ffn_conv_new)
    return x2.reshape(bsz, t, d), states


def _forward(cfg, x_prompt, x_sample, c_prompt, c_sample, cache_k, cache_v, cache_idx_k, state_ssm, state_ssd_conv,
             state_ffn_conv, rel_bias, w_ada, b_ada, norm_mix_g, w_in, ssd_conv_w, ssd_conv_b, dt_bias, a_log, d_skip,
             ssd_norm_g, w_attn_o, w_ssd_o, w_out, norm_ffn_g, w_up, ffn_conv_w, ffn_conv_b, w_down, final_norm_g):
    depth = w_in.shape[0]
    bp, tp, d = x_prompt.shape
    bs, ts, _ = x_sample.shape
    dt_ = x_prompt.dtype
    hp, hs = x_prompt, x_sample
    c_all = jnp.concatenate([c_prompt, c_sample], axis=0)
    c_all = jnp.pad(c_all, ((0, -(bp + bs) % 8), (0, 0)))
    prompt_states, sample_states = [], []
    for l in range(depth):
        mod = _modulation(cfg, c_all, w_ada[l], b_ada[l]).reshape(c_all.shape[0], 6, d)
        wts = dict(norm_mix_g=norm_mix_g[l], w_in_t=_pack_w_in(cfg, w_in[l]), ssd_conv_w=ssd_conv_w[l],
                   ssd_conv_b=ssd_conv_b[l], dt_bias=dt_bias[l], a_log=a_log[l], d_skip=d_skip[l],
                   ssd_norm_g=ssd_norm_g[l], w_attn_o=w_attn_o[l].astype(BF16), w_ssd_o=w_ssd_o[l].astype(BF16),
                   w_out=w_out[l].astype(BF16), norm_ffn_g=norm_ffn_g[l], w_up=w_up[l],
                   ffn_conv_w=ffn_conv_w[l], ffn_conv_b=ffn_conv_b[l], w_down=w_down[l].astype(BF16))
        hp, st_p = _trunk_layer(cfg, hp, mod[:bp],
                                jnp.zeros((bp, 0, cfg.n_kv, cfg.head_dim), dt_),
                                jnp.zeros((bp, 0, cfg.n_kv, cfg.head_dim), dt_),
                                jnp.zeros((bp, 0, cfg.idx_dim), dt_),
                                jnp.zeros((bp, cfg.ssd_heads, cfg.ssd_head_dim, cfg.d_state), state_ssm.dtype),
                                jnp.zeros((bp, cfg.ssd_conv - 1, cfg.conv_dim), dt_),
                                jnp.zeros((bp, cfg.ffn_conv - 1, 2 * cfg.d_ff), dt_),
                                rel_bias, wts)
        hs, st_s = _trunk_layer(cfg, hs, mod[bp:bp + bs], cache_k[l], cache_v[l], cache_idx_k[l], state_ssm[l],
                                state_ssd_conv[l], state_ffn_conv[l], rel_bias, wts)
        prompt_states.append(st_p)
        sample_states.append(st_s)
    y_prompt = _final_norm(cfg, hp.reshape(bp * tp, d), final_norm_g).reshape(bp, tp, d)
    y_sample = _final_norm(cfg, hs.reshape(bs * ts, d), final_norm_g).reshape(bs, ts, d)
    stack = lambda states, i: jnp.stack([s[i] for s in states], axis=0)
    return (y_prompt, y_sample) + tuple(stack(prompt_states, i) for i in range(6)) + tuple(
        stack(sample_states, i) for i in range(6))


def kernel(x_prompt, x_sample, c_prompt, c_sample, cache_k, cache_v, cache_idx_k, state_ssm, state_ssd_conv,
           state_ffn_conv, rel_bias, w_ada, b_ada, norm_mix_g, w_in, ssd_conv_w, ssd_conv_b, dt_bias, a_log, d_skip,
           ssd_norm_g, w_attn_o, w_ssd_o, w_out, norm_ffn_g, w_up, ffn_conv_w, ffn_conv_b, w_down, final_norm_g):
    return _forward(Cfg(), x_prompt, x_sample, c_prompt, c_sample, cache_k, cache_v, cache_idx_k, state_ssm,
                    state_ssd_conv, state_ffn_conv, rel_bias, w_ada, b_ada, norm_mix_g, w_in, ssd_conv_w, ssd_conv_b,
                    dt_bias, a_log, d_skip, ssd_norm_g, w_attn_o, w_ssd_o, w_out, norm_ffn_g, w_up, ffn_conv_w,
                    ffn_conv_b, w_down, final_norm_g)
```

```python
import functools
import math

import numpy as np
import jax
import jax.numpy as jnp
from jax import lax
from jax.experimental import pallas as pl
from jax.experimental.pallas import tpu as pltpu

F32 = jnp.float32
BF16 = jnp.bfloat16
I32 = jnp.int32

LANES = 128
V7X_VMEM_BYTES = 64 * 1024 * 1024
VMEM_LIMIT = V7X_VMEM_BYTES - 8 * 1024 * 1024
INT_MIN = -(2 ** 31)
NEG_BIG = -1e30
M_INIT = -1e29
LOG2E = math.log2(math.e)
HIGHEST = lax.Precision.HIGHEST


class Cfg:
    def __init__(self, **kw):
        self.d_model = 4096
        self.chunk = 64
        self.n_heads = 16
        self.n_kv = 4
        self.head_dim = 128
        self.n_idx_heads = 16
        self.idx_dim = 64
        self.top_k_max = 256
        self.n_buckets = 32
        self.max_distance = 128
        self.ssd_heads = 32
        self.ssd_head_dim = 64
        self.ssd_groups = 4
        self.d_state = 128
        self.ssd_conv = 4
        self.d_ff = 11008
        self.ffn_conv = 3
        self.eps = 1e-6
        self.tm = 1024
        self.tn_in = 768
        self.tn = 512
        self.tn_ff = 256
        self.tm_down = 512
        self.tn_down = 512
        self.tr = 512
        self.tq = 128
        self.tk_far = 512
        self.ssd_chunk = 128
        self.tn_mod = 512
        for k, v in kw.items():
            assert hasattr(self, k), k
            setattr(self, k, v)
        self.d_ssd = self.ssd_heads * self.ssd_head_dim
        self.conv_dim = self.d_ssd + 2 * self.ssd_groups * self.d_state
        self.hq = self.n_heads * self.head_dim
        self.hkv = self.n_kv * self.head_dim
        self.hidx = self.n_idx_heads * self.idx_dim
        self.in_sizes = (self.hq, self.hkv, self.hkv, self.hidx, self.idx_dim, self.n_idx_heads, self.d_ssd,
                         self.conv_dim, self.ssd_heads, self.d_model, self.d_model)
        segs = [("q", self.hq), ("k", self.hkv), ("v", self.hkv), ("qi", self.hidx), ("z", self.d_ssd),
                ("xbc", self.conv_dim), ("ga", self.d_model), ("gs", self.d_model), ("kiw", LANES), ("dt", LANES)]
        off, self.off = 0, {}
        for name, width in segs:
            assert width % LANES == 0
            self.off[name] = off
            off += width
        self.n_packed = -(-off // self.tn_in) * self.tn_in
        assert self.idx_dim + self.n_idx_heads <= LANES and self.ssd_heads <= LANES


def _silu(x):
    half = 0.5 * x
    return half + half * jnp.tanh(half)


def _cparams(sem):
    return pltpu.CompilerParams(dimension_semantics=sem, vmem_limit_bytes=VMEM_LIMIT)


def _tile(n, pref):
    t = min(n, pref)
    assert n % t == 0, (n, pref)
    return t


def _seq_map(t_seq, tm):
    if t_seq >= tm:
        assert t_seq % tm == 0
        per = t_seq // tm
        return tm, 1, (lambda i: i // per)
    assert tm % t_seq == 0
    return t_seq, tm // t_seq, (lambda i: i)


def _mod_body(c_ref, w_ref, b_ref, o_ref):
    c = c_ref[...]
    a = (c * jax.nn.sigmoid(c)).astype(BF16)
    o_ref[...] = jnp.dot(a, w_ref[...].astype(BF16), preferred_element_type=F32) + b_ref[...]


def _modulation(cfg, c, w_ada, b_ada):
    rows, d = c.shape
    n = w_ada.shape[1]
    tn = _tile(n, cfg.tn_mod)
    return pl.pallas_call(
        _mod_body,
        grid=(n // tn,),
        in_specs=[pl.BlockSpec((rows, d), lambda j: (0, 0)),
                  pl.BlockSpec((d, tn), lambda j: (0, j)),
                  pl.BlockSpec((1, tn), lambda j: (0, j))],
        out_specs=pl.BlockSpec((rows, tn), lambda j: (0, j)),
        out_shape=jax.ShapeDtypeStruct((rows, n), F32),
        compiler_params=_cparams(("parallel",)),
        name="adaln_mod",
    )(c, w_ada, b_ada.reshape(1, n))


def _norm_mod_body(eps, sh_row, sc_row, x_ref, g_ref, mod_ref, o_ref):
    x = x_ref[...]
    y = x * lax.rsqrt(jnp.mean(x * x, axis=-1, keepdims=True) + eps) * g_ref[...]
    y = y * (1.0 + mod_ref[sc_row:sc_row + 1, :]) + mod_ref[sh_row:sh_row + 1, :]
    o_ref[...] = y.astype(o_ref.dtype)


def _norm_mod(cfg, x, g, mod, t_seq, sh_row, sc_row):
    m, d = x.shape
    tr = _tile(t_seq, cfg.tr)
    per = t_seq // tr
    return pl.pallas_call(
        functools.partial(_norm_mod_body, cfg.eps, sh_row, sc_row),
        grid=(m // tr,),
        in_specs=[pl.BlockSpec((tr, d), lambda i: (i, 0)),
                  pl.BlockSpec((1, d), lambda i: (0, 0)),
                  pl.BlockSpec((None, 6, d), lambda i: (i // per, 0, 0))],
        out_specs=pl.BlockSpec((tr, d), lambda i: (i, 0)),
        out_shape=jax.ShapeDtypeStruct((m, d), BF16),
        compiler_params=_cparams(("parallel",)),
        name="rmsnorm_mod",
    )(x, g.reshape(1, d), mod)


def _norm_body(eps, x_ref, g_ref, o_ref):
    x = x_ref[...]
    o_ref[...] = x * lax.rsqrt(jnp.mean(x * x, axis=-1, keepdims=True) + eps) * g_ref[...]


def _final_norm(cfg, x, g):
    m, d = x.shape
    tr = _tile(m, cfg.tr)
    return pl.pallas_call(
        functools.partial(_norm_body, cfg.eps),
        grid=(m // tr,),
        in_specs=[pl.BlockSpec((tr, d), lambda i: (i, 0)), pl.BlockSpec((1, d), lambda i: (0, 0))],
        out_specs=pl.BlockSpec((tr, d), lambda i: (i, 0)),
        out_shape=jax.ShapeDtypeStruct((m, d), F32),
        compiler_params=_cparams(("parallel",)),
        name="final_rmsnorm",
    )(x, g.reshape(1, d))


def _mm_nt_body(a_ref, wt_ref, o_ref):
    o_ref[...] = lax.dot_general(a_ref[...], wt_ref[...], (((1,), (1,)), ((), ())),
                                 preferred_element_type=F32).astype(o_ref.dtype)


def _matmul_nt(a, w_t, tm, tn, out_dtype, name):
    m, k = a.shape
    n = w_t.shape[0]
    tm, tn = _tile(m, tm), _tile(n, tn)
    return pl.pallas_call(
        _mm_nt_body,
        grid=(m // tm, n // tn),
        in_specs=[pl.BlockSpec((tm, k), lambda i, j: (i, 0)), pl.BlockSpec((tn, k), lambda i, j: (j, 0))],
        out_specs=pl.BlockSpec((tm, tn), lambda i, j: (i, j)),
        out_shape=jax.ShapeDtypeStruct((m, n), out_dtype),
        compiler_params=_cparams(("parallel", "parallel")),
        name=name,
    )(a, w_t)


def _merge_body(attn_ref, ssd_ref, wa_ref, ws_ref, ga_ref, gs_ref, o_ref, wa_scr, ws_scr):
    @pl.when(pl.program_id(1) == 0)
    def _():
        wa_scr[...] = wa_ref[...].astype(wa_scr.dtype)
        ws_scr[...] = ws_ref[...].astype(ws_scr.dtype)

    a = jnp.dot(attn_ref[...], wa_scr[...], preferred_element_type=F32)
    s = jnp.dot(ssd_ref[...], ws_scr[...], preferred_element_type=F32)
    o_ref[...] = (jax.nn.sigmoid(ga_ref[...]) * a + jax.nn.sigmoid(gs_ref[...]) * s).astype(o_ref.dtype)


def _merge(cfg, attn, ssd, w_attn_o, w_ssd_o, proj):
    m = attn.shape[0]
    d = cfg.d_model
    tm = _tile(m, cfg.tm)
    tn = math.gcd(math.gcd(cfg.off["ga"], cfg.off["gs"]), _tile(d, cfg.tn))
    ga0, gs0 = cfg.off["ga"] // tn, cfg.off["gs"] // tn
    return pl.pallas_call(
        _merge_body,
        grid=(d // tn, m // tm),
        in_specs=[pl.BlockSpec((tm, attn.shape[1]), lambda j, i: (i, 0)),
                  pl.BlockSpec((tm, ssd.shape[1]), lambda j, i: (i, 0)),
                  pl.BlockSpec((attn.shape[1], tn), lambda j, i: (0, j)),
                  pl.BlockSpec((ssd.shape[1], tn), lambda j, i: (0, j)),
                  pl.BlockSpec((tm, tn), lambda j, i: (i, ga0 + j)),
                  pl.BlockSpec((tm, tn), lambda j, i: (i, gs0 + j))],
        out_specs=pl.BlockSpec((tm, tn), lambda j, i: (i, j)),
        out_shape=jax.ShapeDtypeStruct((m, d), BF16),
        scratch_shapes=[pltpu.VMEM((attn.shape[1], tn), attn.dtype), pltpu.VMEM((ssd.shape[1], tn), ssd.dtype)],
        compiler_params=_cparams(("arbitrary", "arbitrary")),
        name="branch_merge",
    )(attn, ssd, w_attn_o, w_ssd_o, proj, proj)


def _resid_body(rows, groups, gate_row, a_ref, w_ref, x_ref, mod_ref, o_ref, *w_scr):
    if w_scr:
        @pl.when(pl.program_id(1) == 0)
        def _():
            w_scr[0][...] = w_ref[...].astype(w_scr[0].dtype)
        w = w_scr[0][...]
    else:
        w = w_ref[...]
    acc = jnp.dot(a_ref[...], w, preferred_element_type=F32)
    for g in range(groups):
        sl = slice(g * rows, (g + 1) * rows)
        o_ref[sl, :] = x_ref[sl, :] + mod_ref[g, gate_row:gate_row + 1, :] * acc[sl, :]


def _gated_residual(a, w, x, mod, t_seq, gate_row, tm, tn, name):
    m, k = a.shape
    n = w.shape[1]
    tm, tn = _tile(m, tm), _tile(n, tn)
    rows, groups, seq_of = _seq_map(t_seq, tm)
    cast = w.dtype != a.dtype
    ij = (lambda f: (lambda j, i: f(i, j))) if cast else (lambda f: f)
    return pl.pallas_call(
        functools.partial(_resid_body, rows, groups, gate_row),
        grid=(n // tn, m // tm) if cast else (m // tm, n // tn),
        in_specs=[pl.BlockSpec((tm, k), ij(lambda i, j: (i, 0))),
                  pl.BlockSpec((k, tn), ij(lambda i, j: (0, j))),
                  pl.BlockSpec((tm, tn), ij(lambda i, j: (i, j))),
                  pl.BlockSpec((groups, 6, tn), ij(lambda i, j: (seq_of(i), 0, j)))],
        out_specs=pl.BlockSpec((tm, tn), ij(lambda i, j: (i, j))),
        out_shape=jax.ShapeDtypeStruct((m, n), F32),
        scratch_shapes=[pltpu.VMEM((k, tn), a.dtype)] if cast else [],
        compiler_params=_cparams(("arbitrary", "arbitrary") if cast else ("parallel", "parallel")),
        name=name,
    )(a, w, x, mod)


SUBLANES = 8


def _shifted_rows(x, prev, n):
    rid = lax.broadcasted_iota(I32, (SUBLANES, x.shape[1]), 0)
    out = []
    for k in range(1, n + 1):
        rolled = pltpu.roll(x, k, 0)
        head = rolled[0:SUBLANES, :]
        for r in range(k):
            head = jnp.where(rid == r, prev[n - k + r:n - k + r + 1, :], head)
        out.append(jnp.concatenate([head, rolled[SUBLANES:, :]], axis=0))
    return out


def _up_body(rows, groups, tiles_per_seq, a_ref, wg_ref, wv_ref, cwg_ref, cwv_ref, cbg_ref, cbv_ref, hg_ref, hv_ref,
             h_ref, ng_ref, nv_ref, up_ref, carry_ref, w_scr):
    i = pl.program_id(1)

    @pl.when(i == 0)
    def _():
        w_scr[0] = wg_ref[...].astype(w_scr.dtype)
        w_scr[1] = wv_ref[...].astype(w_scr.dtype)

    a = a_ref[...]
    stride = rows + SUBLANES
    hc = []
    for half, (cw_ref, cb_ref, hist_ref, new_ref) in enumerate(
            ((cwg_ref, cbg_ref, hg_ref, ng_ref), (cwv_ref, cbv_ref, hv_ref, nv_ref))):
        up = jnp.dot(a, w_scr[half], preferred_element_type=F32)
        outs = []
        for g in range(groups):
            base = g * stride + SUBLANES
            prev = hist_ref[g]
            if groups == 1 and tiles_per_seq > 1:
                prev = jnp.where(i % tiles_per_seq == 0, prev, carry_ref[half])
            up_ref[half, base - 2:base, :] = prev
            up_ref[half, base:base + rows, :] = up[g * rows:(g + 1) * rows, :]
            last = up[(g + 1) * rows - 2:(g + 1) * rows, :]
            new_ref[g] = last
            if groups == 1 and tiles_per_seq > 1:
                carry_ref[half] = last
            taps =[up_ref[half, base - k:base - k + rows, :] for k in (2, 1, 0)]
            outs.append(cw_ref[0:1, :] * taps[0] + cw_ref[1:2, :] * taps[1] + cw_ref[2:3, :] * taps[2] + cb_ref[...])
        hc.append(outs)
    for g in range(groups):
        gate, val = hc[0][g], hc[1][g]
        h_ref[g * rows:(g + 1) * rows, :] = (_silu(gate) * val).astype(h_ref.dtype)


def _ffn_up(cfg, u2, w_up, conv_w, conv_b, hist, t_seq):
    m, d = u2.shape
    nf = cfg.d_ff
    tm, tn = _tile(m, cfg.tm), _tile(nf, cfg.tn_ff)
    rows, groups, seq_of = _seq_map(t_seq, tm)
    nj = nf // tn
    wspec = lambda half: pl.BlockSpec((d, tn), lambda j, i: (0, j + half * nj))
    cspec = lambda r, half: pl.BlockSpec((r, tn), lambda j, i: (0, j + half * nj))
    hspec = lambda half: pl.BlockSpec((groups, 2, tn), lambda j, i: (seq_of(i), 0, j + half * nj))
    ospec = pl.BlockSpec((groups, 2, tn), lambda j, i: (i, 0, j))
    h, new_g, new_v = pl.pallas_call(
        functools.partial(_up_body, rows, groups, max(1, t_seq // tm)),
        grid=(nj, m // tm),
        in_specs=[pl.BlockSpec((tm, d), lambda j, i: (i, 0)), wspec(0), wspec(1),
                  cspec(3, 0), cspec(3, 1), cspec(1, 0), cspec(1, 1), hspec(0), hspec(1)],
        out_specs=[pl.BlockSpec((tm, tn), lambda j, i: (i, j)), ospec, ospec],
        out_shape=[jax.ShapeDtypeStruct((m, nf), BF16),
                   jax.ShapeDtypeStruct((m // rows, 2, nf), F32),
                   jax.ShapeDtypeStruct((m // rows, 2, nf), F32)],
        scratch_shapes=[pltpu.VMEM((2, groups * (rows + SUBLANES), tn), F32), pltpu.VMEM((2, 2, tn), F32),
                        pltpu.VMEM((2, d, tn), BF16)],
        compiler_params=_cparams(("arbitrary", "arbitrary")),
        name="ffn_up_conv_swiglu",
    )(u2, w_up, w_up, conv_w, conv_w, conv_b.reshape(1, -1), conv_b.reshape(1, -1), hist, hist)
    per_seq = t_seq // rows
    return h, jnp.concatenate([new_g[per_seq - 1::per_seq], new_v[per_seq - 1::per_seq]], axis=-1)


def _ssd_body(cfg, lc, z_ref, xbc_ref, dt_ref, hist_ref, s0_ref, cw_ref, cb_ref, dtb_ref, alog_ref, dskip_ref,
              ng_ref, expand_ref, y_ref, state_ref, cnew_ref, xcarry_ref, ydiag_ref):
    nh, hp, ng, ns = cfg.ssd_heads, cfg.ssd_head_dim, cfg.ssd_groups, cfg.d_state
    ds = cfg.d_ssd
    per_group = nh // ng
    c = pl.program_id(1)

    @pl.when(c == 0)
    def _():
        state_ref[...] = s0_ref[...]
        xcarry_ref[0:3, :] = hist_ref[...]

    x = xbc_ref[...]
    prev = xcarry_ref[0:3, :]
    s1, s2, s3 = _shifted_rows(x, prev, 3)
    xc =cw_ref[0:1, :] * s3 + cw_ref[1:2, :] * s2 + cw_ref[2:3, :] * s1 + cw_ref[3:4, :] * x + cb_ref[...]
    xc = _silu(xc)
    last = x[lc - 3:lc, :]
    xcarry_ref[0:3, :] = last
    cnew_ref[...] = last
    xs = xc[:, :ds]
    bm = xc[:, ds:ds + ng * ns].astype(BF16)
    cm = xc[:, ds + ng * ns:].astype(BF16)

    raw = dt_ref[...] + dtb_ref[...]
    dt = jnp.maximum(raw, 0.0) + jnp.log1p(jnp.exp(-jnp.abs(raw)))
    a = dt * (-jnp.exp(alog_ref[...]))
    ri = lax.broadcasted_iota(I32, (lc, lc), 0)
    ci = lax.broadcasted_iota(I32, (lc, lc), 1)
    causal = ri >= ci
    a_cum = jnp.dot(causal.astype(F32), a, precision=HIGHEST, preferred_element_type=F32)
    eye = (lax.broadcasted_iota(I32, (LANES, LANES), 0) == lax.broadcasted_iota(I32, (LANES, LANES), 1))
    nt = (((1,), (1,)), ((), ()))
    a_cum_t = lax.dot_general(eye.astype(F32), a_cum, nt, precision=HIGHEST, preferred_element_type=F32)
    both = jnp.concatenate([dt, a_cum], axis=0)
    hi = both.astype(BF16)
    rest = both - hi.astype(F32)
    mid = rest.astype(BF16)
    lo = (rest - mid.astype(F32)).astype(BF16)
    wide = jnp.dot(jnp.concatenate([hi, mid, lo], axis=1), expand_ref[...], preferred_element_type=F32)
    dt_x, a_x = wide[:lc, :], wide[lc:, :]
    a_end = a_x[lc - 1:lc, :]
    xd = xs * dt_x

    for g in range(ng):
        cb = lax.dot_general(cm[:, g * ns:(g + 1) * ns], bm[:, g * ns:(g + 1) * ns], nt, preferred_element_type=F32)
        for r in range(per_group):
            h = g * per_group + r
            seg = a_cum[:, h:h + 1] - a_cum_t[h:h + 1, :]
            lmat = jnp.exp(jnp.where(causal, seg, -jnp.inf))
            ydiag_ref[:, h * hp:(h + 1) * hp] = jnp.dot((cb * lmat).astype(BF16), xd[:, h * hp:(h + 1) * hp].astype(BF16),
                                                        preferred_element_type=F32)

    state = state_ref[...]
    xdd = (xd * jnp.exp(a_end - a_x)).astype(BF16)
    w = per_group * hp
    y_off, new_cols = [], []
    for g in range(ng):
        y_off.append(jnp.dot(cm[:, g * ns:(g + 1) * ns], state[:, g * w:(g + 1) * w].astype(BF16),
                             preferred_element_type=F32))
        b_t = lax.dot_general(eye.astype(BF16), bm[:, g * ns:(g + 1) * ns], nt, preferred_element_type=F32).astype(BF16)
        new_cols.append(jnp.dot(b_t, xdd[:, g * w:(g + 1) * w], preferred_element_type=F32))
    state_ref[...] = state * jnp.exp(a_end) + jnp.concatenate(new_cols, axis=1)
    y = ydiag_ref[...] + jnp.concatenate(y_off, axis=1) * jnp.exp(a_x) + dskip_ref[...] * xs

    z = z_ref[...]
    y = y * _silu(z)
    gw = ds // ng
    for g in range(ng):
        yg = y[:, g * gw:(g + 1) * gw]
        yg = yg * lax.rsqrt(jnp.mean(yg * yg, axis=-1, keepdims=True) + cfg.eps)
        y_ref[:, g * gw:(g + 1) * gw] = (yg * ng_ref[:, g * gw:(g + 1) * gw]).astype(y_ref.dtype)


def _ssd(cfg, proj, t_seq, hist, state0_t, conv_w, conv_b, dt_bias, a_log, d_skip, norm_g):
    m = proj.shape[0]
    n_seq = m // t_seq
    lc = _tile(t_seq, cfg.ssd_chunk)
    nc = t_seq // lc
    ds, cd, ns = cfg.d_ssd, cfg.conv_dim, cfg.d_state
    assert cfg.off["z"] % ds == 0 and cfg.off["xbc"] % cd == 0
    zb, xb, db = cfg.off["z"] // ds, cfg.off["xbc"] // cd, cfg.off["dt"] // LANES
    pad = LANES - cfg.ssd_heads
    lane_row = lambda v: jnp.pad(v.astype(F32), (0, pad)).reshape(1, LANES)
    expand = (jnp.arange(LANES)[:, None] == (jnp.arange(ds) // cfg.ssd_head_dim)[None, :]).astype(BF16)
    expand = jnp.concatenate([expand] * 3, axis=0)
    const = lambda shape: pl.BlockSpec(shape, lambda b, c: (0,) * len(shape))
    return pl.pallas_call(
        functools.partial(_ssd_body, cfg, lc),
        grid=(n_seq, nc),
        in_specs=[pl.BlockSpec((lc, ds), lambda b, c: (b * nc + c, zb)),
                  pl.BlockSpec((lc, cd), lambda b, c: (b * nc + c, xb)),
                  pl.BlockSpec((lc, LANES), lambda b, c: (b * nc + c, db)),
                  pl.BlockSpec((None, 3, cd), lambda b, c: (b, 0, 0)),
                  pl.BlockSpec((None, ns, ds), lambda b, c: (b, 0, 0)),
                  const((cfg.ssd_conv, cd)), const((1, cd)), const((1, LANES)), const((1, LANES)), const((1, ds)),
                  const((1, ds)), const((3 * LANES, ds))],
        out_specs=[pl.BlockSpec((lc, ds), lambda b, c: (b * nc + c, 0)),
                   pl.BlockSpec((None, ns, ds), lambda b, c: (b, 0, 0)),
                   pl.BlockSpec((None, 3, cd), lambda b, c: (b, 0, 0))],
        out_shape=[jax.ShapeDtypeStruct((m, ds), BF16),
                   jax.ShapeDtypeStruct((n_seq, ns, ds), F32),
                   jax.ShapeDtypeStruct((n_seq, 3, cd), F32)],
        scratch_shapes=[pltpu.VMEM((8, cd), F32), pltpu.VMEM((lc, ds), F32)],
        compiler_params=_cparams(("parallel", "arbitrary")),
        name="ssd_scan",
    )(proj, proj, proj, hist, state0_t, conv_w, conv_b.reshape(1, cd), lane_row(dt_bias), lane_row(a_log),
      jnp.repeat(d_skip.astype(F32), cfg.ssd_head_dim).reshape(1, ds), norm_g.reshape(1, ds), expand)


NEAR_COLS = 640
NEAR_BACK = 512
TK_NEAR = 128
TK_IDX = 512
SEARCH_BITS_FIRST = 27


def _t5_bucket(cfg, rel):
    nb = cfg.n_buckets // 2
    max_exact = nb // 2
    ret = jnp.where(rel > 0, nb, 0)
    n = jnp.abs(rel)
    nf = jnp.maximum(n, 1).astype(F32)
    large = max_exact + (jnp.log(nf / max_exact) / math.log(cfg.max_distance / max_exact) * (nb - max_exact)).astype(I32)
    large = jnp.minimum(large, nb - 1)
    return ret + jnp.where(n < max_exact, n, large)


def _near_bias(cfg, rel_bias, tq):
    assert cfg.max_distance <= LANES
    rel = jnp.arange(NEAR_COLS, dtype=I32)[:, None] - NEAR_BACK - jnp.arange(tq, dtype=I32)[None, :]
    far = rel_bias[_t5_bucket(cfg, jnp.asarray(-cfg.max_distance, I32))].astype(F32)
    onehot = (_t5_bucket(cfg, rel)[..., None] == jnp.arange(cfg.n_buckets, dtype=I32)).astype(F32)
    tab = jnp.einsum("ctb,bh->cth", onehot, rel_bias.astype(F32), precision=HIGHEST)
    tab = (tab - far) * LOG2E
    grp = cfg.n_heads // cfg.n_kv
    tab = tab.reshape(NEAR_COLS, tq, cfg.n_kv, grp)
    return jnp.transpose(tab, (2, 0, 3, 1)).reshape(cfg.n_kv, NEAR_COLS, grp * tq)


def _dsa_body(cfg, tq, past, n_select, q_ref, qi_ref, kiw_ref, k_ref, v_ref, kidx_ref, bias_ref, o_ref,
              keys_ref, qs_ref, qis_ref, w_ref, m_ref, l_ref, acc_ref, s_ref, thr_ref):
    nkv, hd, di, nih = cfg.n_kv, cfg.head_dim, cfg.idx_dim, cfg.n_idx_heads
    grp = cfg.n_heads // nkv
    gw = grp * tq
    nt = (((1,), (1,)), ((), ()))
    tn = (((0,), (0,)), ((), ()))
    eye = lax.broadcasted_iota(I32, (LANES, LANES), 0) == lax.broadcasted_iota(I32, (LANES, LANES), 1)
    eye_bf = eye.astype(BF16)
    x0 = past + pl.program_id(1) * tq
    k_end = x0 + tq
    n_idx = (k_end + TK_IDX - 1) // TK_IDX

    q = (q_ref[...] * ((hd ** -0.5) * LOG2E)).astype(BF16)
    for n in range(nkv):
        for g in range(grp):
            h = n * grp + g
            qs_ref[n, :, g * tq:(g + 1) * tq] = lax.dot_general(
                eye_bf, q[:, h * hd:(h + 1) * hd], nt, preferred_element_type=F32).astype(BF16)
    qi = qi_ref[...].astype(BF16)
    per_blk = LANES // di
    for j in range(nih // per_blk):
        t_blk = lax.dot_general(eye_bf, qi[:, j * LANES:(j + 1) * LANES], nt, preferred_element_type=F32)
        for r in range(per_blk):
            h = j * per_blk + r
            qis_ref[:, h * tq:(h + 1) * tq] = t_blk[r * di:(r + 1) * di, :].astype(BF16)
    kiw_t = lax.dot_general(eye.astype(F32), kiw_ref[...], nt, precision=HIGHEST, preferred_element_type=F32)
    w_ref[...] = kiw_t[di:di + nih, :] * ((di ** -0.5) * (nih ** -0.5))

    qpos = x0 + lax.broadcasted_iota(I32, (1, tq), 1)
    limit = (jnp.right_shift(qpos, int(math.log2(cfg.chunk))) + 1) * cfg.chunk
    krow = lax.broadcasted_iota(I32, (TK_IDX, tq), 0)

    def idx_tile(t, carry):
        k0 = pl.multiple_of(t * TK_IDX, TK_IDX)
        logits = jnp.dot(kidx_ref[pl.ds(k0, TK_IDX), :], qis_ref[...], preferred_element_type=F32)
        sc = jnp.zeros((TK_IDX, tq), F32)
        for h in range(nih):
            sc = sc + w_ref[h:h + 1, :] * jnp.maximum(logits[:, h * tq:(h + 1) * tq], 0.0)
        bits = pltpu.bitcast(jnp.where(sc == 0.0, 0.0, sc), I32)
        key = jnp.where(bits < 0, bits ^ 0x7FFFFFFF, bits)
        keys_ref[pl.ds(k0, TK_IDX), :] = jnp.where(k0 + krow < limit, key, INT_MIN)
        return carry

    lax.fori_loop(0, n_idx, idx_tile, 0)

    def count(pred):
        def body(t, accs):
            k0 = pl.multiple_of(t * TK_IDX, TK_IDX)
            blk = keys_ref[pl.ds(k0, TK_IDX), :]
            accs = list(accs)
            for r in range(TK_IDX // SUBLANES):
                hit = pred(blk[r * SUBLANES:(r + 1) * SUBLANES, :])
                accs[r % len(accs)] = accs[r % len(accs)] + jnp.where(hit, 1.0, 0.0)
            return tuple(accs)

        accs = lax.fori_loop(0, n_idx, body, (jnp.zeros((SUBLANES, tq), F32),) * 8)
        return jnp.sum(functools.reduce(lambda a, b: a + b, accs), axis=0, keepdims=True)

    settled0 = jnp.where(limit < n_select, 1.0, 0.0)

    def bit_step(s, c):
        thr, settled = c
        bit = 31 - s
        cand = jnp.where(bit == 31, jnp.zeros_like(thr), thr | jnp.left_shift(jnp.int32(1), bit))
        cand8 = jnp.broadcast_to(cand, (SUBLANES, tq))
        kept = count(lambda b: b >= cand8)
        thr = jnp.where(kept >= float(n_select), cand, thr)
        return thr, jnp.maximum(settled, jnp.where(kept == float(n_select), 1.0, 0.0))

    thr, settled = lax.fori_loop(0, SEARCH_BITS_FIRST, bit_step, (jnp.full((1, tq), INT_MIN, I32), settled0))
    thr_ref[...] = thr

    @pl.when(jnp.sum(1.0 - settled) > 0.0)
    def _():
        thr = lax.fori_loop(SEARCH_BITS_FIRST, 32, bit_step, (thr_ref[...], settled))[0]
        thr_ref[...] = thr
        thr = jnp.maximum(thr, INT_MIN + 1)
        thr8 = jnp.broadcast_to(thr, (SUBLANES, tq))
        need = float(n_select) - count(lambda b: b > thr8)
        n_tied = count(lambda b: b == thr8)

        @pl.when(jnp.max(n_tied - need) > 0.0)
        def _():
            tri = (lax.broadcasted_iota(I32, (TK_IDX, TK_IDX), 0) >= lax.broadcasted_iota(I32, (TK_IDX, TK_IDX), 1))
            tri = tri.astype(BF16)

            def drop_late_ties(t, seen):
                k0 = pl.multiple_of(t * TK_IDX, TK_IDX)
                blk = keys_ref[pl.ds(k0, TK_IDX), :]
                tied = blk == thr
                rank = seen + jnp.dot(tri, jnp.where(tied, 1.0, 0.0).astype(BF16), preferred_element_type=F32)
                keys_ref[pl.ds(k0, TK_IDX), :] = jnp.where(tied, jnp.where(rank > need, INT_MIN, blk), blk)
                return rank[TK_IDX - 1:TK_IDX, :]

            lax.fori_loop(0, n_idx, drop_late_ties, jnp.zeros((1, tq), F32))

    thr = jnp.maximum(thr_ref[...], INT_MIN + 1)

    far_end = jnp.maximum(x0 - LANES, 0) // cfg.tk_far * cfg.tk_far
    n_far = far_end // cfg.tk_far
    n_near = ((k_end + TK_NEAR - 1) // TK_NEAR * TK_NEAR - far_end) // TK_NEAR
    m_ref[...] = jnp.full(m_ref.shape, M_INIT, F32)
    l_ref[...] = jnp.zeros(l_ref.shape, F32)
    acc_ref[...] = jnp.zeros(acc_ref.shape, F32)

    def key_tiles(n_tiles, first, width, with_bias):
        ahead = nkv % 2 == 0

        def logits(n, k0):
            s = jnp.dot(k_ref[pl.ds(k0, width), n * hd:(n + 1) * hd], qs_ref[n], preferred_element_type=F32)
            if with_bias:
                s = s + bias_ref[n, pl.ds(pl.multiple_of(k0 - (x0 - NEAR_BACK), TK_NEAR), width), :]
            sel = keys_ref[pl.ds(k0, width), :] >= thr
            for g in range(grp):
                s_ref[n % 2, 0:width, g * tq:(g + 1) * tq] = jnp.where(sel, s[:, g * tq:(g + 1) * tq], NEG_BIG)

        start = lambda t: pl.multiple_of(first + t * width, TK_NEAR)
        if ahead:
            logits(0, start(0))

        def body(t, carry):
            k0 = start(t)
            if not ahead:
                logits(0, k0)
            for n in range(nkv):
                if n + 1 < nkv:
                    logits(n + 1, k0)
                elif ahead:
                    logits(0, start(jnp.minimum(t + 1, n_tiles - 1)))
                s = s_ref[n % 2, 0:width, :]
                m_prev = m_ref[n]
                m_new = jnp.maximum(m_prev, jnp.max(s, axis=0, keepdims=True))
                alpha = jnp.exp2(m_prev - m_new)
                p = jnp.exp2(s - m_new)
                l_ref[n] = alpha * l_ref[n] + jnp.sum(p, axis=0, keepdims=True)
                acc_ref[n] = alpha * acc_ref[n] + lax.dot_general(
                    v_ref[pl.ds(k0, width), n * hd:(n + 1) * hd], p.astype(BF16), tn, preferred_element_type=F32)
                m_ref[n] = m_new
            return carry

        lax.fori_loop(0, n_tiles, body, 0)

    key_tiles(n_far, 0, cfg.tk_far, False)
    key_tiles(n_near, far_end, TK_NEAR, True)
    for n in range(nkv):
        out_t = (acc_ref[n] / l_ref[n]).astype(BF16)
        for g in range(grp):
            h = n * grp + g
            o_ref[:, h * hd:(h + 1) * hd] = lax.dot_general(
                out_t[:, g * tq:(g + 1) * tq], eye_bf, tn, preferred_element_type=F32).astype(o_ref.dtype)


def _dsa(cfg, proj, t_seq, past, k_all, v_all, kidx_all, rel_bias):
    m = proj.shape[0]
    n_seq = m // t_seq
    tq = _tile(t_seq, cfg.tq)
    nq = t_seq // tq
    n_keys = past + t_seq
    lp = k_all.shape[1]
    assert tq % cfg.chunk == 0 and past % LANES == 0 and lp % TK_IDX == 0 and lp >= n_keys
    assert cfg.tk_far % TK_NEAR == 0 and NEAR_BACK == cfg.tk_far and NEAR_COLS == NEAR_BACK + LANES and tq <= LANES
    n_select = min(cfg.top_k_max, n_keys // 4)
    grp = cfg.n_heads // cfg.n_kv
    hq, hidx = cfg.hq, cfg.hidx
    assert cfg.off["q"] % hq == 0 and cfg.off["qi"] % hidx == 0
    assert cfg.head_dim == LANES and LANES % cfg.idx_dim == 0 and cfg.n_idx_heads % (LANES // cfg.idx_dim) == 0
    bias = _near_bias(cfg, rel_bias, tq)
    whole = lambda shape: pl.BlockSpec(shape, lambda b, i: (b,) + (0,) * (len(shape) - 1), pipeline_mode=pl.Buffered(1))
    return pl.pallas_call(
        functools.partial(_dsa_body, cfg, tq, past, n_select),
        grid=(n_seq, nq),
        in_specs=[pl.BlockSpec((tq, hq), lambda b, i: (b * nq + i, cfg.off["q"] // hq)),
                  pl.BlockSpec((tq, hidx), lambda b, i: (b * nq + i, cfg.off["qi"] // hidx)),
                  pl.BlockSpec((tq, LANES), lambda b, i: (b * nq + i, cfg.off["kiw"] // LANES)),
                  whole((None, lp, cfg.hkv)), whole((None, lp, cfg.hkv)), whole((None, lp, cfg.idx_dim)),
                  pl.BlockSpec(bias.shape, lambda b, i: (0, 0, 0), pipeline_mode=pl.Buffered(1))],
        out_specs=pl.BlockSpec((tq, hq), lambda b, i: (b * nq + i, 0)),
        out_shape=jax.ShapeDtypeStruct((m, hq), BF16),
        scratch_shapes=[pltpu.VMEM((lp, tq), I32),
                        pltpu.VMEM((cfg.n_kv, cfg.head_dim, grp * tq), BF16),
                        pltpu.VMEM((cfg.idx_dim, cfg.n_idx_heads * tq), BF16),
                        pltpu.VMEM((cfg.n_idx_heads, tq), F32),
                        pltpu.VMEM((cfg.n_kv, 1, grp * tq), F32), pltpu.VMEM((cfg.n_kv, 1, grp * tq), F32),
                        pltpu.VMEM((cfg.n_kv, cfg.head_dim, grp * tq), F32),
                        pltpu.VMEM((2, cfg.tk_far, grp * tq), F32),
                        pltpu.VMEM((1, tq), I32)],
        compiler_params=_cparams(("parallel", "arbitrary")),
        name="dsa_attention",
    )(proj, proj, proj, k_all, v_all, kidx_all, bias)


def _pack_w_in(cfg, w_in):
    offs = np.cumsum(np.array(cfg.in_sizes))[:-1].tolist()
    q, k, v, qi, ki, wi, z, xbc, dt, ga, gs = jnp.split(w_in.T, offs, axis=0)
    row_pad = lambda a: jnp.pad(a, ((0, LANES - a.shape[0]), (0, 0)))
    packed = jnp.concatenate([q, k, v, qi, z, xbc, ga, gs, row_pad(jnp.concatenate([ki, wi], axis=0)), row_pad(dt)],
                             axis=0)
    return jnp.pad(packed, ((0, cfg.n_packed - packed.shape[0]), (0, 0))).astype(BF16)


def _pad_keys(a, lp):
    return jnp.pad(a, ((0, 0), (0, lp - a.shape[1]), (0, 0))).astype(BF16)


def _trunk_layer(cfg, x, mod, past_k, past_v, past_ik, ssm0, ssd_conv0, ffn_conv0, rel_bias, wts):
    bsz, t, d = x.shape
    m = bsz * t
    past = past_k.shape[1]
    x2d = x.reshape(m, d)
    off = cfg.off

    u = _norm_mod(cfg, x2d, wts["norm_mix_g"], mod, t, 0, 1)
    proj = _matmul_nt(u, wts["w_in_t"], cfg.tm, cfg.tn_in, F32, "in_proj")
    k_new = proj[:, off["k"]:off["k"] + cfg.hkv].reshape(bsz, t, cfg.hkv)
    v_new = proj[:, off["v"]:off["v"] + cfg.hkv].reshape(bsz, t, cfg.hkv)
    ki_new = proj[:, off["kiw"]:off["kiw"] + cfg.idx_dim].reshape(bsz, t, cfg.idx_dim)
    lp = -(-(past + t) // TK_IDX) * TK_IDX
    k_all = _pad_keys(jnp.concatenate([past_k.reshape(bsz, past, cfg.hkv), k_new], axis=1), lp)
    v_all = _pad_keys(jnp.concatenate([past_v.reshape(bsz, past, cfg.hkv), v_new], axis=1), lp)
    ki_all = _pad_keys(jnp.concatenate([past_ik, ki_new], axis=1), lp)
    attn = _dsa(cfg, proj, t, past, k_all, v_all, ki_all, rel_bias)

    state0_t = jnp.transpose(ssm0.astype(F32), (0, 3, 1, 2)).reshape(bsz, cfg.d_state, cfg.d_ssd)
    ssd_out, state_t, ssd_conv_new = _ssd(cfg, proj, t, ssd_conv0, state0_t, wts["ssd_conv_w"], wts["ssd_conv_b"],
                                          wts["dt_bias"], wts["a_log"], wts["d_skip"], wts["ssd_norm_g"])
    h_new = jnp.transpose(state_t.reshape(bsz, cfg.d_state, cfg.ssd_heads, cfg.ssd_head_dim), (0, 2, 3, 1))

    merged = _merge(cfg, attn, ssd_out, wts["w_attn_o"], wts["w_ssd_o"], proj)
    x1 = _gated_residual(merged, wts["w_out"], x2d, mod, t, 2, cfg.tm, cfg.tn, "out_proj_residual")

    u2 = _norm_mod(cfg, x1, wts["norm_ffn_g"], mod, t, 3, 4)
    h, ffn_conv_new = _ffn_up(cfg, u2, wts["w_up"], wts["ffn_conv_w"], wts["ffn_conv_b"], ffn_conv0, t)
    x2 = _gated_residual(h, wts["w_down"], x1, mod, t, 5, cfg.tm_down, cfg.tn_down, "down_proj_residual")
    states = (k_new.reshape(bsz, t, cfg.n_kv, cfg.head_dim), v_new.reshape(bsz, t, cfg.n_kv, cfg.head_dim), ki_new,
              h_new.astype(ssm0.dtype), ssd_conv_new, ffn_conv_new)
    return x2.reshape(bsz, t, d), states


def _forward(cfg, x_prompt, x_sample, c_prompt, c_sample, cache_k, cache_v, cache_idx_k, state_ssm, state_ssd_conv,
             state_ffn_conv, rel_bias, w_ada, b_ada, norm_mix_g, w_in, ssd_conv_w, ssd_conv_b, dt_bias, a_log, d_skip,
             ssd_norm_g, w_attn_o, w_ssd_o, w_out, norm_ffn_g, w_up, ffn_conv_w, ffn_conv_b, w_down, final_norm_g):
    depth = w_in.shape[0]
    bp, tp, d = x_prompt.shape
    bs, ts, _ = x_sample.shape
    dt_ = x_prompt.dtype
    hp, hs = x_prompt, x_sample
    c_all = jnp.concatenate([c_prompt, c_sample], axis=0)
    c_all = jnp.pad(c_all, ((0, -(bp + bs) % 8), (0, 0)))
    prompt_states, sample_states = [], []
    for l in range(depth):
        mod = _modulation(cfg, c_all, w_ada[l], b_ada[l]).reshape(c_all.shape[0], 6, d)
        wts = dict(norm_mix_g=norm_mix_g[l], w_in_t=_pack_w_in(cfg, w_in[l]), ssd_conv_w=ssd_conv_w[l],
                   ssd_conv_b=ssd_conv_b[l], dt_bias=dt_bias[l], a_log=a_log[l], d_skip=d_skip[l],
                   ssd_norm_g=ssd_norm_g[l], w_attn_o=w_attn_o[l], w_ssd_o=w_ssd_o[l],
                   w_out=w_out[l], norm_ffn_g=norm_ffn_g[l], w_up=w_up[l],
                   ffn_conv_w=ffn_conv_w[l], ffn_conv_b=ffn_conv_b[l], w_down=w_down[l].astype(BF16))
        hp, st_p = _trunk_layer(cfg, hp, mod[:bp],
                                jnp.zeros((bp, 0, cfg.n_kv, cfg.head_dim), dt_),
                                jnp.zeros((bp, 0, cfg.n_kv, cfg.head_dim), dt_),
                                jnp.zeros((bp, 0, cfg.idx_dim), dt_),
                                jnp.zeros((bp, cfg.ssd_heads, cfg.ssd_head_dim, cfg.d_state), state_ssm.dtype),
                                jnp.zeros((bp, cfg.ssd_conv - 1, cfg.conv_dim), dt_),
                                jnp.zeros((bp, cfg.ffn_conv - 1, 2 * cfg.d_ff), dt_),
                                rel_bias, wts)
        hs, st_s = _trunk_layer(cfg, hs, mod[bp:bp + bs], cache_k[l], cache_v[l], cache_idx_k[l], state_ssm[l],
                                state_ssd_conv[l], state_ffn_conv[l], rel_bias, wts)
        prompt_states.append(st_p)
        sample_states.append(st_s)
    y_prompt = _final_norm(cfg, hp.reshape(bp * tp, d), final_norm_g).reshape(bp, tp, d)
    y_sample = _final_norm(cfg, hs.reshape(bs * ts, d), final_norm_g).reshape(bs, ts, d)
    stack = lambda states, i: jnp.stack([s[i] for s in states], axis=0)
    return (y_prompt, y_sample) + tuple(stack(prompt_states, i) for i in range(6)) + tuple(
        stack(sample_states, i) for i in range(6))


def kernel(x_prompt, x_sample, c_prompt, c_sample, cache_k, cache_v, cache_idx_k, state_ssm, state_ssd_conv,
           state_ffn_conv, rel_bias, w_ada, b_ada, norm_mix_g, w_in, ssd_conv_w, ssd_conv_b, dt_bias, a_log, d_skip,
           ssd_norm_g, w_attn_o, w_ssd_o, w_out, norm_ffn_g, w_up, ffn_conv_w, ffn_conv_b, w_down, final_norm_g):
    return _forward(Cfg(), x_prompt, x_sample, c_prompt, c_sample, cache_k, cache_v, cache_idx_k, state_ssm,
                    state_ssd_conv, state_ffn_conv, rel_bias, w_ada, b_ada, norm_mix_g, w_in, ssd_conv_w, ssd_conv_b,
                    dt_bias, a_log, d_skip, ssd_norm_g, w_attn_o, w_ssd_o, w_out, norm_ffn_g, w_up, ffn_conv_w,
                    ffn_conv_b, w_down, final_norm_g)
```

```python
import functools
import math

import numpy as np
import jax
import jax.numpy as jnp
from jax import lax
from jax.experimental import pallas as pl
from jax.experimental.pallas import tpu as pltpu

F32 = jnp.float32
BF16 = jnp.bfloat16
I32 = jnp.int32

LANES = 128
SUBLANES = 8
V7X_VMEM_BYTES = 64 * 1024 * 1024
VMEM_LIMIT = V7X_VMEM_BYTES - 8 * 1024 * 1024
INT_MIN = -(2 ** 31)
NEG_BIG = -1e30
M_INIT = -1e29
LOG2E = math.log2(math.e)
HIGHEST = lax.Precision.HIGHEST


class Cfg:
    def __init__(self, **kw):
        self.d_model = 4096
        self.chunk = 64
        self.n_heads = 16
        self.n_kv = 4
        self.head_dim = 128
        self.n_idx_heads = 16
        self.idx_dim = 64
        self.top_k_max = 256
        self.n_buckets = 32
        self.max_distance = 128
        self.ssd_heads = 32
        self.ssd_head_dim = 64
        self.ssd_groups = 4
        self.d_state = 128
        self.ssd_conv = 4
        self.d_ff = 11008
        self.ffn_conv = 3
        self.eps = 1e-6
        self.tm = 1024
        self.tn_in = 768
        self.tn = 512
        self.tn_ff = 256
        self.tm_down = 512
        self.tn_down = 512
        self.tr = 512
        self.tq = 128
        self.tk_far = 512
        self.ssd_chunk = 128
        self.tn_mod = 512
        for k, v in kw.items():
            assert hasattr(self, k), k
            setattr(self, k, v)
        self.d_ssd = self.ssd_heads * self.ssd_head_dim
        self.conv_dim = self.d_ssd + 2 * self.ssd_groups * self.d_state
        self.hq = self.n_heads * self.head_dim
        self.hkv = self.n_kv * self.head_dim
        self.hidx = self.n_idx_heads * self.idx_dim
        self.in_sizes = (self.hq, self.hkv, self.hkv, self.hidx, self.idx_dim, self.n_idx_heads, self.d_ssd,
                         self.conv_dim, self.ssd_heads, self.d_model, self.d_model)
        segs = [("q", self.hq), ("k", self.hkv), ("v", self.hkv), ("qi", self.hidx), ("z", self.d_ssd),
                ("xbc", self.conv_dim), ("ga", self.d_model), ("gs", self.d_model), ("kiw", LANES), ("dt", LANES)]
        off, self.off = 0, {}
        for name, width in segs:
            assert width % LANES == 0
            self.off[name] = off
            off += width
        self.n_packed = -(-off // self.tn_in) * self.tn_in
        assert self.idx_dim + self.n_idx_heads <= LANES and self.ssd_heads <= LANES


def _silu(x):
    half = 0.5 * x
    return half + half * jnp.tanh(half)


def _cparams(sem):
    return pltpu.CompilerParams(dimension_semantics=sem, vmem_limit_bytes=VMEM_LIMIT)


def _tile(n, pref):
    t = min(n, pref)
    assert n % t == 0, (n, pref)
    return t


def _seq_map(t_seq, tm):
    if t_seq >= tm:
        assert t_seq % tm == 0
        per = t_seq // tm
        return tm, 1, (lambda i: i // per)
    assert tm % t_seq == 0
    return t_seq, tm // t_seq, (lambda i: i)


def _mod_body(c_ref, w_ref, b_ref, o_ref):
    c = c_ref[...]
    a = (c * jax.nn.sigmoid(c)).astype(BF16)
    o_ref[...] = jnp.dot(a, w_ref[...].astype(BF16), preferred_element_type=F32) + b_ref[...]


def _modulation(cfg, c, w_ada, b_ada):
    rows, d = c.shape
    n = w_ada.shape[1]
    tn = _tile(n, cfg.tn_mod)
    return pl.pallas_call(
        _mod_body,
        grid=(n // tn,),
        in_specs=[pl.BlockSpec((rows, d), lambda j: (0, 0)),
                  pl.BlockSpec((d, tn), lambda j: (0, j)),
                  pl.BlockSpec((1, tn), lambda j: (0, j))],
        out_specs=pl.BlockSpec((rows, tn), lambda j: (0, j)),
        out_shape=jax.ShapeDtypeStruct((rows, n), F32),
        compiler_params=_cparams(("parallel",)),
        name="adaln_mod",
    )(c, w_ada, b_ada.reshape(1, n))


def _norm_mod_body(eps, sh_row, sc_row, x_ref, g_ref, mod_ref, o_ref):
    x = x_ref[...]
    y = x * lax.rsqrt(jnp.mean(x * x, axis=-1, keepdims=True) + eps) * g_ref[...]
    y = y * (1.0 + mod_ref[sc_row:sc_row + 1, :]) + mod_ref[sh_row:sh_row + 1, :]
    o_ref[...] = y.astype(o_ref.dtype)


def _norm_mod(cfg, x, g, mod, t_seq, sh_row, sc_row):
    m, d = x.shape
    tr = _tile(t_seq, cfg.tr)
    per = t_seq // tr
    return pl.pallas_call(
        functools.partial(_norm_mod_body, cfg.eps, sh_row, sc_row),
        grid=(m // tr,),
        in_specs=[pl.BlockSpec((tr, d), lambda i: (i, 0)),
                  pl.BlockSpec((1, d), lambda i: (0, 0)),
                  pl.BlockSpec((None, 6, d), lambda i: (i // per, 0, 0))],
        out_specs=pl.BlockSpec((tr, d), lambda i: (i, 0)),
        out_shape=jax.ShapeDtypeStruct((m, d), BF16),
        compiler_params=_cparams(("parallel",)),
        name="rmsnorm_mod",
    )(x, g.reshape(1, d), mod)


def _norm_body(eps, x_ref, g_ref, o_ref):
    x = x_ref[...]
    o_ref[...] = x * lax.rsqrt(jnp.mean(x * x, axis=-1, keepdims=True) + eps) * g_ref[...]


def _final_norm(cfg, x, g):
    m, d = x.shape
    tr = _tile(m, cfg.tr)
    return pl.pallas_call(
        functools.partial(_norm_body, cfg.eps),
        grid=(m // tr,),
        in_specs=[pl.BlockSpec((tr, d), lambda i: (i, 0)), pl.BlockSpec((1, d), lambda i: (0, 0))],
        out_specs=pl.BlockSpec((tr, d), lambda i: (i, 0)),
        out_shape=jax.ShapeDtypeStruct((m, d), F32),
        compiler_params=_cparams(("parallel",)),
        name="final_rmsnorm",
    )(x, g.reshape(1, d))


def _mm_nt_body(a_ref, wt_ref, o_ref):
    o_ref[...] = lax.dot_general(a_ref[...], wt_ref[...], (((1,), (1,)), ((), ())),
                                 preferred_element_type=F32).astype(o_ref.dtype)


def _matmul_nt(a, w_t, tm, tn, out_dtype, name):
    m, k = a.shape
    n = w_t.shape[0]
    tm, tn = _tile(m, tm), _tile(n, tn)
    return pl.pallas_call(
        _mm_nt_body,
        grid=(m // tm, n // tn),
        in_specs=[pl.BlockSpec((tm, k), lambda i, j: (i, 0)), pl.BlockSpec((tn, k), lambda i, j: (j, 0))],
        out_specs=pl.BlockSpec((tm, tn), lambda i, j: (i, j)),
        out_shape=jax.ShapeDtypeStruct((m, n), out_dtype),
        compiler_params=_cparams(("parallel", "parallel")),
        name=name,
    )(a, w_t)


def _merge_body(attn_ref, ssd_ref, wa_ref, ws_ref, ga_ref, gs_ref, o_ref, wa_scr, ws_scr):
    @pl.when(pl.program_id(1) == 0)
    def _():
        wa_scr[...] = wa_ref[...].astype(wa_scr.dtype)
        ws_scr[...] = ws_ref[...].astype(ws_scr.dtype)

    a = jnp.dot(attn_ref[...], wa_scr[...], preferred_element_type=F32)
    s = jnp.dot(ssd_ref[...], ws_scr[...], preferred_element_type=F32)
    o_ref[...] = (jax.nn.sigmoid(ga_ref[...]) * a + jax.nn.sigmoid(gs_ref[...]) * s).astype(o_ref.dtype)


def _merge(cfg, attn, ssd, w_attn_o, w_ssd_o, proj):
    m = attn.shape[0]
    d = cfg.d_model
    tm = _tile(m, cfg.tm)
    tn = math.gcd(math.gcd(cfg.off["ga"], cfg.off["gs"]), _tile(d, cfg.tn))
    ga0, gs0 = cfg.off["ga"] // tn, cfg.off["gs"] // tn
    return pl.pallas_call(
        _merge_body,
        grid=(d // tn, m // tm),
        in_specs=[pl.BlockSpec((tm, attn.shape[1]), lambda j, i: (i, 0)),
                  pl.BlockSpec((tm, ssd.shape[1]), lambda j, i: (i, 0)),
                  pl.BlockSpec((attn.shape[1], tn), lambda j, i: (0, j)),
                  pl.BlockSpec((ssd.shape[1], tn), lambda j, i: (0, j)),
                  pl.BlockSpec((tm, tn), lambda j, i: (i, ga0 + j)),
                  pl.BlockSpec((tm, tn), lambda j, i: (i, gs0 + j))],
        out_specs=pl.BlockSpec((tm, tn), lambda j, i: (i, j)),
        out_shape=jax.ShapeDtypeStruct((m, d), BF16),
        scratch_shapes=[pltpu.VMEM((attn.shape[1], tn), attn.dtype), pltpu.VMEM((ssd.shape[1], tn), ssd.dtype)],
        compiler_params=_cparams(("arbitrary", "arbitrary")),
        name="branch_merge",
    )(attn, ssd, w_attn_o, w_ssd_o, proj, proj)


def _resid_body(rows, groups, gate_row, a_ref, w_ref, x_ref, mod_ref, o_ref, *w_scr):
    if w_scr:
        @pl.when(pl.program_id(1) == 0)
        def _():
            w_scr[0][...] = w_ref[...].astype(w_scr[0].dtype)
        w = w_scr[0][...]
    else:
        w = w_ref[...]
    acc = jnp.dot(a_ref[...], w, preferred_element_type=F32)
    for g in range(groups):
        sl = slice(g * rows, (g + 1) * rows)
        o_ref[sl, :] = x_ref[sl, :] + mod_ref[g, gate_row:gate_row + 1, :] * acc[sl, :]


def _gated_residual(a, w, x, mod, t_seq, gate_row, tm, tn, name):
    m, k = a.shape
    n = w.shape[1]
    tm, tn = _tile(m, tm), _tile(n, tn)
    rows, groups, seq_of = _seq_map(t_seq, tm)
    cast = w.dtype != a.dtype
    ij = (lambda f: (lambda j, i: f(i, j))) if cast else (lambda f: f)
    return pl.pallas_call(
        functools.partial(_resid_body, rows, groups, gate_row),
        grid=(n // tn, m // tm) if cast else (m // tm, n // tn),
        in_specs=[pl.BlockSpec((tm, k), ij(lambda i, j: (i, 0))),
                  pl.BlockSpec((k, tn), ij(lambda i, j: (0, j))),
                  pl.BlockSpec((tm, tn), ij(lambda i, j: (i, j))),
                  pl.BlockSpec((groups, 6, tn), ij(lambda i, j: (seq_of(i), 0, j)))],
        out_specs=pl.BlockSpec((tm, tn), ij(lambda i, j: (i, j))),
        out_shape=jax.ShapeDtypeStruct((m, n), F32),
        scratch_shapes=[pltpu.VMEM((k, tn), a.dtype)] if cast else [],
        compiler_params=_cparams(("arbitrary", "arbitrary") if cast else ("parallel", "parallel")),
        name=name,
    )(a, w, x, mod)


def _shifted_rows(x, prev, n):
    rid = lax.broadcasted_iota(I32, (SUBLANES, x.shape[1]), 0)
    out = []
    for k in range(1, n + 1):
        rolled = pltpu.roll(x, k, 0)
        head = rolled[0:SUBLANES, :]
        for r in range(k):
            head = jnp.where(rid == r, prev[n - k + r:n - k + r + 1, :], head)
        out.append(jnp.concatenate([head, rolled[SUBLANES:, :]], axis=0))
    return out


def _up_body(rows, groups, tiles_per_seq, a_ref, wg_ref, wv_ref, cwg_ref, cwv_ref, cbg_ref, cbv_ref, hg_ref, hv_ref,
             h_ref, ng_ref, nv_ref, up_ref, carry_ref, w_scr):
    i = pl.program_id(1)

    @pl.when(i == 0)
    def _():
        w_scr[0] = wg_ref[...].astype(w_scr.dtype)
        w_scr[1] = wv_ref[...].astype(w_scr.dtype)

    a = a_ref[...]
    stride = rows + SUBLANES
    hc = []
    for half, (cw_ref, cb_ref, hist_ref, new_ref) in enumerate(
            ((cwg_ref, cbg_ref, hg_ref, ng_ref), (cwv_ref, cbv_ref, hv_ref, nv_ref))):
        up = jnp.dot(a, w_scr[half], preferred_element_type=F32)
        outs = []
        for g in range(groups):
            base = g * stride + SUBLANES
            prev = hist_ref[g]
            if groups == 1 and tiles_per_seq > 1:
                prev = jnp.where(i % tiles_per_seq == 0, prev, carry_ref[half])
            up_ref[half, base - 2:base, :] = prev
            up_ref[half, base:base + rows, :] = up[g * rows:(g + 1) * rows, :]
            last = up[(g + 1) * rows - 2:(g + 1) * rows, :]
            new_ref[g] = last
            if groups == 1 and tiles_per_seq > 1:
                carry_ref[half] = last
            taps = [up_ref[half, base - k:base - k + rows, :] for k in (2, 1, 0)]
            outs.append(cw_ref[0:1, :] * taps[0] + cw_ref[1:2, :] * taps[1] + cw_ref[2:3, :] * taps[2] + cb_ref[...])
        hc.append(outs)
    for g in range(groups):
        gate, val = hc[0][g], hc[1][g]
        h_ref[g * rows:(g + 1) * rows, :] = (_silu(gate) * val).astype(h_ref.dtype)


def _ffn_up(cfg, u2, w_up, conv_w, conv_b, hist, t_seq):
    m, d = u2.shape
    nf = cfg.d_ff
    tm, tn = _tile(m, cfg.tm), _tile(nf, cfg.tn_ff)
    rows, groups, seq_of = _seq_map(t_seq, tm)
    nj = nf // tn
    wspec = lambda half: pl.BlockSpec((d, tn), lambda j, i: (0, j + half * nj))
    cspec = lambda r, half: pl.BlockSpec((r, tn), lambda j, i: (0, j + half * nj))
    hspec = lambda half: pl.BlockSpec((groups, 2, tn), lambda j, i: (seq_of(i), 0, j + half * nj))
    ospec = pl.BlockSpec((groups, 2, tn), lambda j, i: (i, 0, j))
    h, new_g, new_v = pl.pallas_call(
        functools.partial(_up_body, rows, groups, max(1, t_seq // tm)),
        grid=(nj, m // tm),
        in_specs=[pl.BlockSpec((tm, d), lambda j, i: (i, 0)), wspec(0), wspec(1),
                  cspec(3, 0), cspec(3, 1), cspec(1, 0), cspec(1, 1), hspec(0), hspec(1)],
        out_specs=[pl.BlockSpec((tm, tn), lambda j, i: (i, j)), ospec, ospec],
        out_shape=[jax.ShapeDtypeStruct((m, nf), BF16),
                   jax.ShapeDtypeStruct((m // rows, 2, nf), F32),
                   jax.ShapeDtypeStruct((m // rows, 2, nf), F32)],
        scratch_shapes=[pltpu.VMEM((2, groups * (rows + SUBLANES), tn), F32), pltpu.VMEM((2, 2, tn), F32),
                        pltpu.VMEM((2, d, tn), BF16)],
        compiler_params=_cparams(("arbitrary", "arbitrary")),
        name="ffn_up_conv_swiglu",
    )(u2, w_up, w_up, conv_w, conv_w, conv_b.reshape(1, -1), conv_b.reshape(1, -1), hist, hist)
    per_seq = t_seq // rows
    return h, jnp.concatenate([new_g[per_seq - 1::per_seq], new_v[per_seq - 1::per_seq]], axis=-1)


def _ssd_body(cfg, lc, z_ref, xbc_ref, dt_ref, hist_ref, s0_ref, cw_ref, cb_ref, dtb_ref, alog_ref, dskip_ref,
              ng_ref, expand_ref, y_ref, state_ref, cnew_ref, xcarry_ref, ydiag_ref):
    nh, hp, ng, ns = cfg.ssd_heads, cfg.ssd_head_dim, cfg.ssd_groups, cfg.d_state
    ds = cfg.d_ssd
    per_group = nh // ng
    c = pl.program_id(1)

    @pl.when(c == 0)
    def _():
        state_ref[...] = s0_ref[...]
        xcarry_ref[0:3, :] = hist_ref[...]

    x = xbc_ref[...]
    prev = xcarry_ref[0:3, :]
    s1, s2, s3 = _shifted_rows(x, prev, 3)
    xc =cw_ref[0:1, :] * s3 + cw_ref[1:2, :] * s2 + cw_ref[2:3, :] * s1 + cw_ref[3:4, :] * x + cb_ref[...]
    xc = _silu(xc)
    last = x[lc - 3:lc, :]
    xcarry_ref[0:3, :] = last
    cnew_ref[...] = last
    xs = xc[:, :ds]
    bm = xc[:, ds:ds + ng * ns].astype(BF16)
    cm = xc[:, ds + ng * ns:].astype(BF16)

    raw = dt_ref[...] + dtb_ref[...]
    dt = jnp.maximum(raw, 0.0) + jnp.log1p(jnp.exp(-jnp.abs(raw)))
    a = dt * (-jnp.exp(alog_ref[...]))
    ri = lax.broadcasted_iota(I32, (lc, lc), 0)
    ci = lax.broadcasted_iota(I32, (lc, lc), 1)
    causal = ri >= ci
    a_cum = jnp.dot(causal.astype(F32), a, precision=HIGHEST, preferred_element_type=F32)
    eye = (lax.broadcasted_iota(I32, (LANES, LANES), 0) == lax.broadcasted_iota(I32, (LANES, LANES), 1))
    nt = (((1,), (1,)), ((), ()))
    a_cum_t = lax.dot_general(eye.astype(F32), a_cum, nt, precision=HIGHEST, preferred_element_type=F32)
    both = jnp.concatenate([dt, a_cum], axis=0)
    hi = both.astype(BF16)
    rest = both - hi.astype(F32)
    mid = rest.astype(BF16)
    lo = (rest - mid.astype(F32)).astype(BF16)
    wide = jnp.dot(jnp.concatenate([hi, mid, lo], axis=1), expand_ref[...], preferred_element_type=F32)
    dt_x, a_x = wide[:lc, :], wide[lc:, :]
    a_end = a_x[lc - 1:lc, :]
    xd = xs * dt_x

    for g in range(ng):
        cb = lax.dot_general(cm[:, g * ns:(g + 1) * ns], bm[:, g * ns:(g + 1) * ns], nt, preferred_element_type=F32)
        for r in range(per_group):
            h = g * per_group + r
            seg = a_cum[:, h:h + 1] - a_cum_t[h:h + 1, :]
            lmat = jnp.exp(jnp.where(causal, seg, -jnp.inf))
            ydiag_ref[:, h * hp:(h + 1) * hp] = jnp.dot((cb * lmat).astype(BF16), xd[:, h * hp:(h + 1) * hp].astype(BF16),
                                                        preferred_element_type=F32)

    state = state_ref[...]
    xdd = (xd * jnp.exp(a_end - a_x)).astype(BF16)
    w = per_group * hp
    y_off, new_cols = [], []
    for g in range(ng):
        y_off.append(jnp.dot(cm[:, g * ns:(g + 1) * ns], state[:, g * w:(g + 1) * w].astype(BF16),
                             preferred_element_type=F32))
        b_t = lax.dot_general(eye.astype(BF16), bm[:, g * ns:(g + 1) * ns], nt, preferred_element_type=F32).astype(BF16)
        new_cols.append(jnp.dot(b_t, xdd[:, g * w:(g + 1) * w], preferred_element_type=F32))
    state_ref[...] = state * jnp.exp(a_end) + jnp.concatenate(new_cols, axis=1)
    y = ydiag_ref[...] + jnp.concatenate(y_off, axis=1) * jnp.exp(a_x) + dskip_ref[...] * xs

    z = z_ref[...]
    y = y * _silu(z)
    gw = ds // ng
    for g in range(ng):
        yg = y[:, g * gw:(g + 1) * gw]
        yg = yg * lax.rsqrt(jnp.mean(yg * yg, axis=-1, keepdims=True) + cfg.eps)
        y_ref[:, g * gw:(g + 1) * gw] = (yg * ng_ref[:, g * gw:(g + 1) * gw]).astype(y_ref.dtype)


def _ssd(cfg, proj, t_seq, hist, state0_t, conv_w, conv_b, dt_bias, a_log, d_skip, norm_g):
    m = proj.shape[0]
    n_seq = m // t_seq
    lc = _tile(t_seq, cfg.ssd_chunk)
    nc = t_seq // lc
    ds, cd, ns = cfg.d_ssd, cfg.conv_dim, cfg.d_state
    assert cfg.off["z"] % ds == 0 and cfg.off["xbc"] % cd == 0
    zb, xb, db = cfg.off["z"] // ds, cfg.off["xbc"] // cd, cfg.off["dt"] // LANES
    pad = LANES - cfg.ssd_heads
    lane_row = lambda v: jnp.pad(v.astype(F32), (0, pad)).reshape(1, LANES)
    expand = (jnp.arange(LANES)[:, None] == (jnp.arange(ds) // cfg.ssd_head_dim)[None, :]).astype(BF16)
    expand = jnp.concatenate([expand] * 3, axis=0)
    const = lambda shape: pl.BlockSpec(shape, lambda b, c: (0,) * len(shape))
    return pl.pallas_call(
        functools.partial(_ssd_body, cfg, lc),
        grid=(n_seq, nc),
        in_specs=[pl.BlockSpec((lc, ds), lambda b, c: (b * nc + c, zb)),
                  pl.BlockSpec((lc, cd), lambda b, c: (b * nc + c, xb)),
                  pl.BlockSpec((lc, LANES), lambda b, c: (b * nc + c, db)),
                  pl.BlockSpec((None, 3, cd), lambda b, c: (b, 0, 0)),
                  pl.BlockSpec((None, ns, ds), lambda b, c: (b, 0, 0)),
                  const((cfg.ssd_conv, cd)), const((1, cd)), const((1, LANES)), const((1, LANES)), const((1, ds)),
                  const((1, ds)), const((3 * LANES, ds))],
        out_specs=[pl.BlockSpec((lc, ds), lambda b, c: (b * nc + c, 0)),
                   pl.BlockSpec((None, ns, ds), lambda b, c: (b, 0, 0)),
                   pl.BlockSpec((None, 3, cd), lambda b, c: (b, 0, 0))],
        out_shape=[jax.ShapeDtypeStruct((m, ds), BF16),
                   jax.ShapeDtypeStruct((n_seq, ns, ds), F32),
                   jax.ShapeDtypeStruct((n_seq, 3, cd), F32)],
        scratch_shapes=[pltpu.VMEM((8, cd), F32), pltpu.VMEM((lc, ds), F32)],
        compiler_params=_cparams(("parallel", "arbitrary")),
        name="ssd_scan",
    )(proj, proj, proj, hist, state0_t, conv_w, conv_b.reshape(1, cd), lane_row(dt_bias), lane_row(a_log),
      jnp.repeat(d_skip.astype(F32), cfg.ssd_head_dim).reshape(1, ds), norm_g.reshape(1, ds), expand)


NEAR_COLS = 640
NEAR_BACK = 512
TK_NEAR = 128
TK_IDX = 512
QK_AHEAD_FAR, QK_AHEAD_NEAR = 1, 2
SEARCH_BITS_FIRST = 27


def _t5_bucket(cfg, rel):
    nb = cfg.n_buckets // 2
    max_exact = nb // 2
    ret = jnp.where(rel > 0, nb, 0)
    n = jnp.abs(rel)
    nf = jnp.maximum(n, 1).astype(F32)
    large = max_exact + (jnp.log(nf / max_exact) / math.log(cfg.max_distance / max_exact) * (nb - max_exact)).astype(I32)
    large = jnp.minimum(large, nb - 1)
    return ret + jnp.where(n < max_exact, n, large)


def _near_bias(cfg, rel_bias, tq):
    assert cfg.max_distance <= LANES
    rel = jnp.arange(NEAR_COLS, dtype=I32)[:, None] - NEAR_BACK - jnp.arange(tq, dtype=I32)[None, :]
    far = rel_bias[_t5_bucket(cfg, jnp.asarray(-cfg.max_distance, I32))].astype(F32)
    onehot = (_t5_bucket(cfg, rel)[..., None] == jnp.arange(cfg.n_buckets, dtype=I32)).astype(F32)
    tab = jnp.einsum("ctb,bh->cth", onehot, rel_bias.astype(F32), precision=HIGHEST)
    tab = (tab - far) * LOG2E
    grp = cfg.n_heads // cfg.n_kv
    tab = tab.reshape(NEAR_COLS, tq, cfg.n_kv, grp)
    return jnp.transpose(tab, (2, 0, 3, 1)).reshape(cfg.n_kv, NEAR_COLS, grp * tq)


def _dsa_body(cfg, tq, past, n_select, q_ref, qi_ref, kiw_ref, k_ref, v_ref, kidx_ref, bias_ref, o_ref,
              keys_ref, qs_ref, qis_ref, w_ref, m_ref, l_ref, acc_ref, s_ref, thr_ref):
    nkv, hd, di, nih = cfg.n_kv, cfg.head_dim, cfg.idx_dim, cfg.n_idx_heads
    grp = cfg.n_heads // nkv
    nt = (((1,), (1,)), ((), ()))
    tn = (((0,), (0,)), ((), ()))
    eye = lax.broadcasted_iota(I32, (LANES, LANES), 0) == lax.broadcasted_iota(I32, (LANES, LANES), 1)
    eye_bf = eye.astype(BF16)
    x0 = past + pl.program_id(1) * tq
    k_end = x0 + tq
    n_idx = (k_end + TK_IDX - 1) // TK_IDX

    q = (q_ref[...] * ((hd ** -0.5) * LOG2E)).astype(BF16)
    for n in range(nkv):
        for g in range(grp):
            h = n * grp + g
            qs_ref[n, :, g * tq:(g + 1) * tq] = lax.dot_general(
                eye_bf, q[:, h * hd:(h + 1) * hd], nt, preferred_element_type=F32).astype(BF16)
    qi = qi_ref[...].astype(BF16)
    per_blk = LANES // di
    for j in range(nih // per_blk):
        t_blk = lax.dot_general(eye_bf, qi[:, j * LANES:(j + 1) * LANES], nt, preferred_element_type=F32)
        for r in range(per_blk):
            h = j * per_blk + r
            qis_ref[:, h * tq:(h + 1) * tq] = t_blk[r * di:(r + 1) * di, :].astype(BF16)
    kiw_t = lax.dot_general(eye.astype(F32), kiw_ref[...], nt, precision=HIGHEST, preferred_element_type=F32)
    w_ref[...] = kiw_t[di:di + nih, :] * ((di ** -0.5) * (nih ** -0.5))

    qpos = x0 + lax.broadcasted_iota(I32, (1, tq), 1)
    limit = (jnp.right_shift(qpos, int(math.log2(cfg.chunk))) + 1) * cfg.chunk
    krow = lax.broadcasted_iota(I32, (TK_IDX, tq), 0)

    def idx_tile(t, carry):
        k0 = pl.multiple_of(t * TK_IDX, TK_IDX)
        logits = jnp.dot(kidx_ref[pl.ds(k0, TK_IDX), :], qis_ref[...], preferred_element_type=F32)
        sc = jnp.zeros((TK_IDX, tq), F32)
        for h in range(nih):
            sc = sc + w_ref[h:h + 1, :] * jnp.maximum(logits[:, h * tq:(h + 1) * tq], 0.0)
        bits = pltpu.bitcast(jnp.where(sc == 0.0, 0.0, sc), I32)
        key = jnp.where(bits < 0, bits ^ 0x7FFFFFFF, bits)
        keys_ref[pl.ds(k0, TK_IDX), :] = jnp.where(k0 + krow < limit, key, INT_MIN)
        return carry

    lax.fori_loop(0, n_idx, idx_tile, 0)

    def count(pred):
        def body(t, accs):
            k0 = pl.multiple_of(t * TK_IDX, TK_IDX)
            blk = keys_ref[pl.ds(k0, TK_IDX), :]
            accs = list(accs)
            for r in range(TK_IDX // SUBLANES):
                hit = pred(blk[r * SUBLANES:(r + 1) * SUBLANES, :])
                accs[r % len(accs)] = jnp.where(hit, accs[r % len(accs)] + 1.0, accs[r % len(accs)])
            return tuple(accs)

        accs = lax.fori_loop(0, n_idx, body, (jnp.zeros((SUBLANES, tq), F32),) * 8)
        return jnp.sum(functools.reduce(lambda a, b: a + b, accs), axis=0, keepdims=True)

    settled0 = jnp.where(limit < n_select, 1.0, 0.0)

    def bit_step(s, c):
        thr, settled = c
        bit = 31 - s
        cand = jnp.where(bit == 31, jnp.zeros_like(thr), thr | jnp.left_shift(jnp.int32(1), bit))
        cand8 = jnp.broadcast_to(cand, (SUBLANES, tq))
        kept = count(lambda b: b >= cand8)
        thr = jnp.where(kept >= float(n_select), cand, thr)
        return thr, jnp.maximum(settled, jnp.where(kept == float(n_select), 1.0, 0.0))

    thr, settled = lax.fori_loop(0, SEARCH_BITS_FIRST, bit_step, (jnp.full((1, tq), INT_MIN, I32), settled0))
    thr_ref[...] = thr

    @pl.when(jnp.sum(1.0 - settled) > 0.0)
    def _():
        thr = lax.fori_loop(SEARCH_BITS_FIRST, 32, bit_step, (thr_ref[...], settled))[0]
        thr_ref[...] = thr
        thr = jnp.maximum(thr, INT_MIN + 1)
        thr8 = jnp.broadcast_to(thr, (SUBLANES, tq))
        need = float(n_select) - count(lambda b: b > thr8)
        n_tied = count(lambda b: b == thr8)

        @pl.when(jnp.max(n_tied - need) > 0.0)
        def _():
            tri = (lax.broadcasted_iota(I32, (TK_IDX, TK_IDX), 0) >= lax.broadcasted_iota(I32, (TK_IDX, TK_IDX), 1))
            tri = tri.astype(BF16)

            def drop_late_ties(t, seen):
                k0 = pl.multiple_of(t * TK_IDX, TK_IDX)
                blk = keys_ref[pl.ds(k0, TK_IDX), :]
                tied = blk == thr
                rank = seen + jnp.dot(tri, jnp.where(tied, 1.0, 0.0).astype(BF16), preferred_element_type=F32)
                keys_ref[pl.ds(k0, TK_IDX), :] = jnp.where(tied, jnp.where(rank > need, INT_MIN, blk), blk)
                return rank[TK_IDX - 1:TK_IDX, :]

            lax.fori_loop(0, n_idx, drop_late_ties, jnp.zeros((1, tq), F32))

    thr = jnp.maximum(thr_ref[...], INT_MIN + 1)

    far_end = jnp.maximum(x0 - LANES, 0) // cfg.tk_far * cfg.tk_far
    n_far = far_end // cfg.tk_far
    n_near = ((k_end + TK_NEAR - 1) // TK_NEAR * TK_NEAR - far_end) // TK_NEAR
    m_ref[...] = jnp.full(m_ref.shape, M_INIT, F32)
    l_ref[...] = jnp.zeros(l_ref.shape, F32)
    acc_ref[...] = jnp.zeros(acc_ref.shape, F32)

    def key_tiles(n_tiles, first, width, with_bias, qk_ahead):
        ahead = min(qk_ahead, nkv - 1)

        def logits(n, k0):
            s = jnp.dot(k_ref[pl.ds(k0, width), n * hd:(n + 1) * hd], qs_ref[n], preferred_element_type=F32)
            if with_bias:
                s = s + bias_ref[n, pl.ds(pl.multiple_of(k0 - (x0 - NEAR_BACK), TK_NEAR), width), :]
            sel = keys_ref[pl.ds(k0, width), :] >= thr
            for g in range(grp):
                s_ref[n, 0:width, g * tq:(g + 1) * tq] = jnp.where(sel, s[:, g * tq:(g + 1) * tq], NEG_BIG)

        start = lambda t: pl.multiple_of(first + t * width, TK_NEAR)
        for n in range(ahead):
            logits(n, start(0))

        def body(t, carry):
            k0 = start(t)
            k_next = start(jnp.minimum(t + 1, n_tiles - 1))
            for n in range(nkv):
                later = n + ahead
                logits(later % nkv, k0 if later < nkv else k_next)
                s = s_ref[n, 0:width, :]
                m_prev = m_ref[n]
                m_new = jnp.maximum(m_prev, jnp.max(s, axis=0, keepdims=True))
                alpha = jnp.exp2(m_prev - m_new)
                p = jnp.exp2(s - m_new)
                l_ref[n] = alpha * l_ref[n] + jnp.sum(p, axis=0, keepdims=True)
                acc_ref[n] = alpha * acc_ref[n] + lax.dot_general(
                    v_ref[pl.ds(k0, width), n * hd:(n + 1) * hd], p.astype(BF16), tn, preferred_element_type=F32)
                m_ref[n] = m_new
            return carry

        lax.fori_loop(0, n_tiles, body, 0)

    key_tiles(n_far, 0, cfg.tk_far, False, QK_AHEAD_FAR)
    key_tiles(n_near, far_end, TK_NEAR, True, QK_AHEAD_NEAR)
    for n in range(nkv):
        out_t = (acc_ref[n] / l_ref[n]).astype(BF16)
        for g in range(grp):
            h = n * grp + g
            o_ref[:, h * hd:(h + 1) * hd] = lax.dot_general(
                out_t[:, g * tq:(g + 1) * tq], eye_bf, tn, preferred_element_type=F32).astype(o_ref.dtype)


def _dsa(cfg, proj, t_seq, past, k_all, v_all, kidx_all, rel_bias):
    m = proj.shape[0]
    n_seq = m // t_seq
    tq = _tile(t_seq, cfg.tq)
    nq = t_seq // tq
    n_keys = past + t_seq
    lp = k_all.shape[1]
    assert tq % cfg.chunk == 0 and past % LANES == 0 and lp % TK_IDX == 0 and lp >= n_keys
    assert cfg.tk_far % TK_NEAR == 0 and NEAR_BACK == cfg.tk_far and NEAR_COLS == NEAR_BACK + LANES and tq <= LANES
    n_select = min(cfg.top_k_max, n_keys // 4)
    grp = cfg.n_heads // cfg.n_kv
    hq, hidx = cfg.hq, cfg.hidx
    assert cfg.off["q"] % hq == 0 and cfg.off["qi"] % hidx == 0
    assert cfg.head_dim == LANES and LANES % cfg.idx_dim == 0 and cfg.n_idx_heads % (LANES // cfg.idx_dim) == 0
    bias = _near_bias(cfg, rel_bias, tq)
    whole = lambda shape: pl.BlockSpec(shape, lambda b, i: (b,) + (0,) * (len(shape) - 1), pipeline_mode=pl.Buffered(1))
    return pl.pallas_call(
        functools.partial(_dsa_body, cfg, tq, past, n_select),
        grid=(n_seq, nq),
        in_specs=[pl.BlockSpec((tq, hq), lambda b, i: (b * nq + i, cfg.off["q"] // hq)),
                  pl.BlockSpec((tq, hidx), lambda b, i: (b * nq + i, cfg.off["qi"] // hidx)),
                  pl.BlockSpec((tq, LANES), lambda b, i: (b * nq + i, cfg.off["kiw"] // LANES)),
                  whole((None, lp, cfg.hkv)), whole((None, lp, cfg.hkv)), whole((None, lp, cfg.idx_dim)),
                  pl.BlockSpec(bias.shape, lambda b, i: (0, 0, 0), pipeline_mode=pl.Buffered(1))],
        out_specs=pl.BlockSpec((tq, hq), lambda b, i: (b * nq + i, 0)),
        out_shape=jax.ShapeDtypeStruct((m, hq), BF16),
        scratch_shapes=[pltpu.VMEM((lp, tq), I32),
                        pltpu.VMEM((cfg.n_kv, cfg.head_dim, grp * tq), BF16),
                        pltpu.VMEM((cfg.idx_dim, cfg.n_idx_heads * tq), BF16),
                        pltpu.VMEM((cfg.n_idx_heads, tq), F32),
                        pltpu.VMEM((cfg.n_kv, 1, grp * tq), F32), pltpu.VMEM((cfg.n_kv, 1, grp * tq), F32),
                        pltpu.VMEM((cfg.n_kv, cfg.head_dim, grp * tq), F32),
                        pltpu.VMEM((cfg.n_kv, cfg.tk_far, grp * tq), F32),
                        pltpu.VMEM((1, tq), I32)],
        compiler_params=_cparams(("parallel", "arbitrary")),
        name="dsa_attention",
    )(proj, proj, proj, k_all, v_all, kidx_all, bias)


def _pack_w_in(cfg, w_in):
    offs = np.cumsum(np.array(cfg.in_sizes))[:-1].tolist()
    q, k, v, qi, ki, wi, z, xbc, dt, ga, gs = jnp.split(w_in.T, offs, axis=0)
    row_pad = lambda a: jnp.pad(a, ((0, LANES - a.shape[0]), (0, 0)))
    packed = jnp.concatenate([q, k, v, qi, z, xbc, ga, gs, row_pad(jnp.concatenate([ki, wi], axis=0)), row_pad(dt)],
                             axis=0)
    return jnp.pad(packed, ((0, cfg.n_packed - packed.shape[0]), (0, 0))).astype(BF16)


def _pad_keys(a, lp):
    return jnp.pad(a, ((0, 0), (0, lp - a.shape[1]), (0, 0))).astype(BF16)


def _trunk_layer(cfg, x, mod, past_k, past_v, past_ik, ssm0, ssd_conv0, ffn_conv0, rel_bias, wts):
    bsz, t, d = x.shape
    m = bsz * t
    past = past_k.shape[1]
    x2d = x.reshape(m, d)
    off = cfg.off

    u = _norm_mod(cfg, x2d, wts["norm_mix_g"], mod, t, 0, 1)
    proj = _matmul_nt(u, wts["w_in_t"], cfg.tm, cfg.tn_in, F32, "in_proj")
    k_new = proj[:, off["k"]:off["k"] + cfg.hkv].reshape(bsz, t, cfg.hkv)
    v_new = proj[:, off["v"]:off["v"] + cfg.hkv].reshape(bsz, t, cfg.hkv)
    ki_new = proj[:, off["kiw"]:off["kiw"] + cfg.idx_dim].reshape(bsz, t, cfg.idx_dim)
    lp = -(-(past + t) // TK_IDX) * TK_IDX
    k_all = _pad_keys(jnp.concatenate([past_k.reshape(bsz, past, cfg.hkv), k_new], axis=1), lp)
    v_all = _pad_keys(jnp.concatenate([past_v.reshape(bsz, past, cfg.hkv), v_new], axis=1), lp)
    ki_all = _pad_keys(jnp.concatenate([past_ik, ki_new], axis=1), lp)
    attn = _dsa(cfg, proj, t, past, k_all, v_all, ki_all, rel_bias)

    state0_t = jnp.transpose(ssm0.astype(F32), (0, 3, 1, 2)).reshape(bsz, cfg.d_state, cfg.d_ssd)
    ssd_out, state_t, ssd_conv_new = _ssd(cfg, proj, t, ssd_conv0, state0_t, wts["ssd_conv_w"], wts["ssd_conv_b"],
                                          wts["dt_bias"], wts["a_log"], wts["d_skip"], wts["ssd_norm_g"])
    h_new = jnp.transpose(state_t.reshape(bsz, cfg.d_state, cfg.ssd_heads, cfg.ssd_head_dim), (0, 2, 3, 1))

    merged = _merge(cfg, attn, ssd_out, wts["w_attn_o"], wts["w_ssd_o"], proj)
    x1 = _gated_residual(merged, wts["w_out"], x2d, mod, t, 2, cfg.tm, cfg.tn, "out_proj_residual")

    u2 = _norm_mod(cfg, x1, wts["norm_ffn_g"], mod, t, 3, 4)
    h, ffn_conv_new = _ffn_up(cfg, u2, wts["w_up"], wts["ffn_conv_w"], wts["ffn_conv_b"], ffn_conv0, t)
    x2 = _gated_residual(h, wts["w_down"], x1, mod, t, 5, cfg.tm_down, cfg.tn_down, "down_proj_residual")
    states = (k_new.reshape(bsz, t, cfg.n_kv, cfg.head_dim), v_new.reshape(bsz, t, cfg.n_kv, cfg.head_dim), ki_new,
              h_new.astype(ssm0.dtype), ssd_conv_new, ffn_conv_new)
    return x2.reshape(bsz, t, d), states


def _forward(cfg, x_prompt, x_sample, c_prompt, c_sample, cache_k, cache_v, cache_idx_k, state_ssm, state_ssd_conv,
             state_ffn_conv, rel_bias, w_ada, b_ada, norm_mix_g, w_in, ssd_conv_w, ssd_conv_b, dt_bias, a_log, d_skip,
             ssd_norm_g, w_attn_o, w_ssd_o, w_out, norm_ffn_g, w_up, ffn_conv_w, ffn_conv_b, w_down, final_norm_g):
    depth = w_in.shape[0]
    bp, tp, d = x_prompt.shape
    bs, ts, _ = x_sample.shape
    dt_ = x_prompt.dtype
    hp, hs = x_prompt, x_sample
    c_all = jnp.concatenate([c_prompt, c_sample], axis=0)
    c_all = jnp.pad(c_all, ((0, -(bp + bs) % 8), (0, 0)))
    prompt_states, sample_states = [], []
    for l in range(depth):
        mod = _modulation(cfg, c_all, w_ada[l], b_ada[l]).reshape(c_all.shape[0], 6, d)
        wts = dict(norm_mix_g=norm_mix_g[l], w_in_t=_pack_w_in(cfg, w_in[l]), ssd_conv_w=ssd_conv_w[l],
                   ssd_conv_b=ssd_conv_b[l], dt_bias=dt_bias[l], a_log=a_log[l], d_skip=d_skip[l],
                   ssd_norm_g=ssd_norm_g[l], w_attn_o=w_attn_o[l], w_ssd_o=w_ssd_o[l],
                   w_out=w_out[l], norm_ffn_g=norm_ffn_g[l], w_up=w_up[l],
                   ffn_conv_w=ffn_conv_w[l], ffn_conv_b=ffn_conv_b[l], w_down=w_down[l].astype(BF16))
        hp, st_p = _trunk_layer(cfg, hp, mod[:bp],
                                jnp.zeros((bp, 0, cfg.n_kv, cfg.head_dim), dt_),
                                jnp.zeros((bp, 0, cfg.n_kv, cfg.head_dim), dt_),
                                jnp.zeros((bp, 0, cfg.idx_dim), dt_),
                                jnp.zeros((bp, cfg.ssd_heads, cfg.ssd_head_dim, cfg.d_state), state_ssm.dtype),
                                jnp.zeros((bp, cfg.ssd_conv - 1, cfg.conv_dim), dt_),
                                jnp.zeros((bp, cfg.ffn_conv - 1, 2 * cfg.d_ff), dt_),
                                rel_bias, wts)
        hs, st_s = _trunk_layer(cfg, hs, mod[bp:bp + bs], cache_k[l], cache_v[l], cache_idx_k[l], state_ssm[l],
                                state_ssd_conv[l], state_ffn_conv[l], rel_bias, wts)
        prompt_states.append(st_p)
        sample_states.append(st_s)
    y_prompt = _final_norm(cfg, hp.reshape(bp * tp, d), final_norm_g).reshape(bp, tp, d)
    y_sample = _final_norm(cfg, hs.reshape(bs * ts, d), final_norm_g).reshape(bs, ts, d)
    stack = lambda states, i: jnp.stack([s[i] for s in states], axis=0)
    return (y_prompt, y_sample) + tuple(stack(prompt_states, i) for i in range(6)) + tuple(
        stack(sample_states, i) for i in range(6))


def kernel(x_prompt, x_sample, c_prompt, c_sample, cache_k, cache_v, cache_idx_k, state_ssm, state_ssd_conv,
           state_ffn_conv, rel_bias, w_ada, b_ada, norm_mix_g, w_in, ssd_conv_w, ssd_conv_b, dt_bias, a_log, d_skip,
           ssd_norm_g, w_attn_o, w_ssd_o, w_out, norm_ffn_g, w_up, ffn_conv_w, ffn_conv_b, w_down, final_norm_g):
    return _forward(Cfg(), x_prompt, x_sample, c_prompt, c_sample, cache_k, cache_v, cache_idx_k, state_ssm,
                    state_ssd_conv, state_ffn_conv, rel_bias, w_ada, b_ada, norm_mix_g, w_in, ssd_conv_w, ssd_conv_b,
                    dt_bias, a_log, d_skip, ssd_norm_g, w_attn_o, w_ssd_o, w_out, norm_ffn_g, w_up, ffn_conv_w,
                    ffn_conv_b, w_down, final_norm_g)
```

```python
import functools
import math

import numpy as np
import jax
import jax.numpy as jnp
from jax import lax
from jax.experimental import pallas as pl
from jax.experimental.pallas import tpu as pltpu

F32 = jnp.float32
BF16 = jnp.bfloat16
I32 = jnp.int32

LANES = 128
SUBLANES = 8
V7X_VMEM_BYTES = 64 * 1024 * 1024
VMEM_LIMIT = V7X_VMEM_BYTES - 8 * 1024 * 1024
INT_MIN = -(2 ** 31)
NEG_BIG = -1e30
M_INIT = -1e29
LOG2E = math.log2(math.e)
HIGHEST = lax.Precision.HIGHEST


class Cfg:
    def __init__(self, **kw):
        self.d_model = 4096
        self.chunk = 64
        self.n_heads = 16
        self.n_kv = 4
        self.head_dim = 128
        self.n_idx_heads = 16
        self.idx_dim = 64
        self.top_k_max = 256
        self.n_buckets = 32
        self.max_distance = 128
        self.ssd_heads = 32
        self.ssd_head_dim = 64
        self.ssd_groups = 4
        self.d_state = 128
        self.ssd_conv = 4
        self.d_ff = 11008
        self.ffn_conv = 3
        self.eps = 1e-6
        self.tm = 1024
        self.tn_in = 512
        self.tn = 512
        self.tn_ff = 256
        self.tm_down = 512
        self.tn_down = 512
        self.tr = 512
        self.tq = 128
        self.tk_far = 512
        self.ssd_chunk = 128
        self.tn_mod = 512
        for k, v in kw.items():
            assert hasattr(self, k), k
            setattr(self, k, v)
        self.d_ssd = self.ssd_heads * self.ssd_head_dim
        self.conv_dim = self.d_ssd + 2 * self.ssd_groups * self.d_state
        self.hq = self.n_heads * self.head_dim
        self.hkv = self.n_kv * self.head_dim
        self.hidx = self.n_idx_heads * self.idx_dim
        self.in_sizes = (self.hq, self.hkv, self.hkv, self.hidx, self.idx_dim, self.n_idx_heads, self.d_ssd,
                         self.conv_dim, self.ssd_heads, self.d_model, self.d_model)
        src = dict(zip(("q", "k", "v", "qi", "ki", "wi", "z", "xbc", "dt", "ga", "gs"),
                       np.cumsum((0,) + self.in_sizes[:-1]).tolist()))
        segs = [("q", self.hq), ("k", self.hkv), ("v", self.hkv), ("qi", self.hidx), ("z", self.d_ssd),
                ("xbc", self.conv_dim), ("ga", self.d_model), ("gs", self.d_model)]
        off, self.off, shifts = 0, {}, []
        for name, width in segs:
            assert width % LANES == 0
            self.off[name] = off
            shifts.append((off, src[name] - off))
            off += width
        self.n_main = off
        assert off % self.tn_in == 0
        self.shift_steps = []
        for start, shift in shifts:
            if not self.shift_steps or shift != self.shift_steps[-1][1]:
                assert start % self.tn_in == 0 and shift % SUBLANES == 0
                self.shift_steps.append((start // self.tn_in, shift))
        self.src_kiw, self.src_dt = src["ki"], src["dt"]
        self.off_small = {"kiw": 0, "dt": LANES}
        assert self.idx_dim + self.n_idx_heads <= LANES and self.ssd_heads <= LANES


def _silu(x):
    half = 0.5 * x
    return half + half * jnp.tanh(half)


def _cparams(sem):
    return pltpu.CompilerParams(dimension_semantics=sem, vmem_limit_bytes=VMEM_LIMIT)


def _tile(n, pref):
    t = min(n, pref)
    assert n % t == 0, (n, pref)
    return t


def _seq_map(t_seq, tm):
    if t_seq >= tm:
        assert t_seq % tm == 0
        per = t_seq // tm
        return tm, 1, (lambda i: i // per)
    assert tm % t_seq == 0
    return t_seq, tm // t_seq, (lambda i: i)


def _mod_body(c_ref, w_ref, b_ref, o_ref):
    c = c_ref[...]
    a = (c * jax.nn.sigmoid(c)).astype(BF16)
    o_ref[...] = jnp.dot(a, w_ref[...].astype(BF16), preferred_element_type=F32) + b_ref[...]


def _modulation(cfg, c, w_ada, b_ada):
    rows, d = c.shape
    n = w_ada.shape[1]
    tn = _tile(n, cfg.tn_mod)
    return pl.pallas_call(
        _mod_body,
        grid=(n // tn,),
        in_specs=[pl.BlockSpec((rows, d), lambda j: (0, 0)),
                  pl.BlockSpec((d, tn), lambda j: (0, j)),
                  pl.BlockSpec((1, tn), lambda j: (0, j))],
        out_specs=pl.BlockSpec((rows, tn), lambda j: (0, j)),
        out_shape=jax.ShapeDtypeStruct((rows, n), F32),
        compiler_params=_cparams(("parallel",)),
        name="adaln_mod",
    )(c, w_ada, b_ada.reshape(1, n))


def _norm_mod_body(eps, sh_row, sc_row, x_ref, g_ref, mod_ref, o_ref):
    x = x_ref[...]
    y = x * lax.rsqrt(jnp.mean(x * x, axis=-1, keepdims=True) + eps) * g_ref[...]
    y = y * (1.0 + mod_ref[sc_row:sc_row + 1, :]) + mod_ref[sh_row:sh_row + 1, :]
    o_ref[...] = y.astype(o_ref.dtype)


def _norm_mod(cfg, x, g, mod, t_seq, sh_row, sc_row):
    m, d = x.shape
    tr = _tile(t_seq, cfg.tr)
    per = t_seq // tr
    return pl.pallas_call(
        functools.partial(_norm_mod_body, cfg.eps, sh_row, sc_row),
        grid=(m // tr,),
        in_specs=[pl.BlockSpec((tr, d), lambda i: (i, 0)),
                  pl.BlockSpec((1, d), lambda i: (0, 0)),
                  pl.BlockSpec((None, 6, d), lambda i: (i // per, 0, 0))],
        out_specs=pl.BlockSpec((tr, d), lambda i: (i, 0)),
        out_shape=jax.ShapeDtypeStruct((m, d), BF16),
        compiler_params=_cparams(("parallel",)),
        name="rmsnorm_mod",
    )(x, g.reshape(1, d), mod)


def _norm_body(eps, x_ref, g_ref, o_ref):
    x = x_ref[...]
    o_ref[...] = x * lax.rsqrt(jnp.mean(x * x, axis=-1, keepdims=True) + eps) * g_ref[...]


def _final_norm(cfg, x, g):
    m, d = x.shape
    tr = _tile(m, cfg.tr)
    return pl.pallas_call(
        functools.partial(_norm_body, cfg.eps),
        grid=(m // tr,),
        in_specs=[pl.BlockSpec((tr, d), lambda i: (i, 0)), pl.BlockSpec((1, d), lambda i: (0, 0))],
        out_specs=pl.BlockSpec((tr, d), lambda i: (i, 0)),
        out_shape=jax.ShapeDtypeStruct((m, d), F32),
        compiler_params=_cparams(("parallel",)),
        name="final_rmsnorm",
    )(x, g.reshape(1, d))


def _mm_nt_body(a_ref, wt_ref, o_ref):
    o_ref[...] = lax.dot_general(a_ref[...], wt_ref[...], (((1,), (1,)), ((), ())),
                                 preferred_element_type=F32).astype(o_ref.dtype)


def _matmul_nt(a, w_t, tm, tn, out_dtype, name):
    m, k = a.shape
    n = w_t.shape[0]
    tm, tn = _tile(m, tm), _tile(n, tn)
    return pl.pallas_call(
        _mm_nt_body,
        grid=(m // tm, n // tn),
        in_specs=[pl.BlockSpec((tm, k), lambda i, j: (i, 0)), pl.BlockSpec((tn, k), lambda i, j: (j, 0))],
        out_specs=pl.BlockSpec((tm, tn), lambda i, j: (i, j)),
        out_shape=jax.ShapeDtypeStruct((m, n), out_dtype),
        compiler_params=_cparams(("parallel", "parallel")),
        name=name,
    )(a, w_t)


def _in_proj_body(a_ref, wt_ref, o_ref):
    o_ref[...] = lax.dot_general(a_ref[...], wt_ref[...].astype(a_ref.dtype), (((1,), (1,)), ((), ())),
                                 preferred_element_type=F32)


def _in_proj(cfg, a, w_t):
    m, k = a.shape
    tm, tn = _tile(m, cfg.tm), cfg.tn_in

    def first_row(j):
        groups, prev = j * (tn // SUBLANES), 0
        for tile0, shift in cfg.shift_steps:
            groups, prev = groups + jnp.where(j >= tile0, (shift - prev) // SUBLANES, 0), shift
        return groups * SUBLANES

    return pl.pallas_call(
        _in_proj_body,
        grid=(m // tm, cfg.n_main // tn),
        in_specs=[pl.BlockSpec((tm, k), lambda i, j: (i, 0)),
                  pl.BlockSpec((pl.Element(tn), pl.Element(k)), lambda i, j: (first_row(j), 0))],
        out_specs=pl.BlockSpec((tm, tn), lambda i, j: (i, j)),
        out_shape=jax.ShapeDtypeStruct((m, cfg.n_main), F32),
        compiler_params=_cparams(("parallel", "parallel")),
        name="in_proj",
    )(a, w_t)


def _merge_body(attn_ref, ssd_ref, wa_ref, ws_ref, ga_ref, gs_ref, o_ref, wa_scr, ws_scr):
    @pl.when(pl.program_id(1) == 0)
    def _():
        wa_scr[...] = wa_ref[...].astype(wa_scr.dtype)
        ws_scr[...] = ws_ref[...].astype(ws_scr.dtype)

    a = jnp.dot(attn_ref[...], wa_scr[...], preferred_element_type=F32)
    s = jnp.dot(ssd_ref[...], ws_scr[...], preferred_element_type=F32)
    o_ref[...] = (jax.nn.sigmoid(ga_ref[...]) * a + jax.nn.sigmoid(gs_ref[...]) * s).astype(o_ref.dtype)


def _merge(cfg, attn, ssd, w_attn_o, w_ssd_o, proj):
    m = attn.shape[0]
    d = cfg.d_model
    tm = _tile(m, cfg.tm)
    tn = math.gcd(math.gcd(cfg.off["ga"], cfg.off["gs"]), _tile(d, cfg.tn))
    ga0, gs0 = cfg.off["ga"] // tn, cfg.off["gs"] // tn
    return pl.pallas_call(
        _merge_body,
        grid=(d // tn, m // tm),
        in_specs=[pl.BlockSpec((tm, attn.shape[1]), lambda j, i: (i, 0)),
                  pl.BlockSpec((tm, ssd.shape[1]), lambda j, i: (i, 0)),
                  pl.BlockSpec((attn.shape[1], tn), lambda j, i: (0, j)),
                  pl.BlockSpec((ssd.shape[1], tn), lambda j, i: (0, j)),
                  pl.BlockSpec((tm, tn), lambda j, i: (i, ga0 + j)),
                  pl.BlockSpec((tm, tn), lambda j, i: (i, gs0 + j))],
        out_specs=pl.BlockSpec((tm, tn), lambda j, i: (i, j)),
        out_shape=jax.ShapeDtypeStruct((m, d), BF16),
        scratch_shapes=[pltpu.VMEM((attn.shape[1], tn), attn.dtype), pltpu.VMEM((ssd.shape[1], tn), ssd.dtype)],
        compiler_params=_cparams(("arbitrary", "arbitrary")),
        name="branch_merge",
    )(attn, ssd, w_attn_o, w_ssd_o, proj, proj)


def _resid_body(rows, groups, gate_row, a_ref, w_ref, x_ref, mod_ref, o_ref, *w_scr):
    if w_scr:
        @pl.when(pl.program_id(1) == 0)
        def _():
            w_scr[0][...] = w_ref[...].astype(w_scr[0].dtype)
        w = w_scr[0][...]
    else:
        w = w_ref[...]
    acc = jnp.dot(a_ref[...], w, preferred_element_type=F32)
    for g in range(groups):
        sl = slice(g * rows, (g + 1) * rows)
        o_ref[sl, :] = x_ref[sl, :] + mod_ref[g, gate_row:gate_row + 1, :] * acc[sl, :]


def _gated_residual(a, w, x, mod, t_seq, gate_row, tm, tn, name):
    m, k = a.shape
    n = w.shape[1]
    tm, tn = _tile(m, tm), _tile(n, tn)
    rows, groups, seq_of = _seq_map(t_seq, tm)
    cast = w.dtype != a.dtype
    ij = (lambda f: (lambda j, i: f(i, j))) if cast else (lambda f: f)
    return pl.pallas_call(
        functools.partial(_resid_body, rows, groups, gate_row),
        grid=(n // tn, m // tm) if cast else (m // tm, n // tn),
        in_specs=[pl.BlockSpec((tm, k), ij(lambda i, j: (i, 0))),
                  pl.BlockSpec((k, tn), ij(lambda i, j: (0, j))),
                  pl.BlockSpec((tm, tn), ij(lambda i, j: (i, j))),
                  pl.BlockSpec((groups, 6, tn), ij(lambda i, j: (seq_of(i), 0, j)))],
        out_specs=pl.BlockSpec((tm, tn), ij(lambda i, j: (i, j))),
        out_shape=jax.ShapeDtypeStruct((m, n), F32),
        scratch_shapes=[pltpu.VMEM((k, tn), a.dtype)] if cast else [],
        compiler_params=_cparams(("arbitrary", "arbitrary") if cast else ("parallel", "parallel")),
        name=name,
    )(a, w, x, mod)


def _shifted_rows(x, prev, n):
    rid = lax.broadcasted_iota(I32, (SUBLANES, x.shape[1]), 0)
    out = []
    for k in range(1, n + 1):
        rolled = pltpu.roll(x, k, 0)
        head = rolled[0:SUBLANES, :]
        for r in range(k):
            head = jnp.where(rid == r, prev[n - k + r:n - k + r + 1, :], head)
        out.append(jnp.concatenate([head, rolled[SUBLANES:, :]], axis=0))
    return out


def _up_body(rows, groups, tiles_per_seq, a_ref, wg_ref, wv_ref, cwg_ref, cwv_ref, cbg_ref, cbv_ref, hg_ref, hv_ref,
             h_ref, ng_ref, nv_ref, up_ref, carry_ref, w_scr):
    i = pl.program_id(1)

    @pl.when(i == 0)
    def _():
        w_scr[0] = wg_ref[...].astype(w_scr.dtype)
        w_scr[1] = wv_ref[...].astype(w_scr.dtype)

    a = a_ref[...]
    stride = rows + SUBLANES
    hc = []
    for half, (cw_ref, cb_ref, hist_ref, new_ref) in enumerate(
            ((cwg_ref, cbg_ref, hg_ref, ng_ref), (cwv_ref, cbv_ref, hv_ref, nv_ref))):
        up = jnp.dot(a, w_scr[half], preferred_element_type=F32)
        outs = []
        for g in range(groups):
            base = g * stride + SUBLANES
            prev = hist_ref[g]
            if groups == 1 and tiles_per_seq > 1:
                prev = jnp.where(i % tiles_per_seq == 0, prev, carry_ref[half])
            up_ref[half, base - 2:base, :] = prev
            up_ref[half, base:base + rows, :] = up[g * rows:(g + 1) * rows, :]
            last = up[(g + 1) * rows - 2:(g + 1) * rows, :]
            new_ref[g] = last
            if groups == 1 and tiles_per_seq > 1:
                carry_ref[half] = last
            taps = [up_ref[half, base - k:base - k + rows, :] for k in (2, 1, 0)]
            outs.append(cw_ref[0:1, :] * taps[0] + cw_ref[1:2, :] * taps[1] + cw_ref[2:3, :] * taps[2] + cb_ref[...])
        hc.append(outs)
    for g in range(groups):
        gate, val = hc[0][g], hc[1][g]
        h_ref[g * rows:(g + 1) * rows, :] = (_silu(gate) * val).astype(h_ref.dtype)


def _ffn_up(cfg, u2, w_up, conv_w, conv_b, hist, t_seq):
    m, d = u2.shape
    nf = cfg.d_ff
    tm, tn = _tile(m, cfg.tm), _tile(nf, cfg.tn_ff)
    rows, groups, seq_of = _seq_map(t_seq, tm)
    nj = nf // tn
    wspec = lambda half: pl.BlockSpec((d, tn), lambda j, i: (0, j + half * nj))
    cspec = lambda r, half: pl.BlockSpec((r, tn), lambda j, i: (0, j + half * nj))
    hspec = lambda half: pl.BlockSpec((groups, 2, tn), lambda j, i: (seq_of(i), 0, j + half * nj))
    ospec = pl.BlockSpec((groups, 2, tn), lambda j, i: (i, 0, j))
    h, new_g, new_v = pl.pallas_call(
        functools.partial(_up_body, rows, groups, max(1, t_seq // tm)),
        grid=(nj, m // tm),
        in_specs=[pl.BlockSpec((tm, d), lambda j, i: (i, 0)), wspec(0), wspec(1),
                  cspec(3, 0), cspec(3, 1), cspec(1, 0), cspec(1, 1), hspec(0), hspec(1)],
        out_specs=[pl.BlockSpec((tm, tn), lambda j, i: (i, j)), ospec, ospec],
        out_shape=[jax.ShapeDtypeStruct((m, nf), BF16),
                   jax.ShapeDtypeStruct((m // rows, 2, nf), F32),
                   jax.ShapeDtypeStruct((m // rows, 2, nf), F32)],
        scratch_shapes=[pltpu.VMEM((2, groups * (rows + SUBLANES), tn), F32), pltpu.VMEM((2, 2, tn), F32),
                        pltpu.VMEM((2, d, tn), BF16)],
        compiler_params=_cparams(("arbitrary", "arbitrary")),
        name="ffn_up_conv_swiglu",
    )(u2, w_up, w_up, conv_w, conv_w, conv_b.reshape(1, -1), conv_b.reshape(1, -1), hist, hist)
    per_seq = t_seq // rows
    return h, jnp.concatenate([new_g[per_seq - 1::per_seq], new_v[per_seq - 1::per_seq]], axis=-1)


def _ssd_body(cfg, lc, z_ref, xbc_ref, dt_ref, hist_ref, s0_ref, cw_ref, cb_ref, dtb_ref, alog_ref, dskip_ref,
              ng_ref, expand_ref, y_ref, state_ref, cnew_ref, xcarry_ref, ydiag_ref):
    nh, hp, ng, ns = cfg.ssd_heads, cfg.ssd_head_dim, cfg.ssd_groups, cfg.d_state
    ds = cfg.d_ssd
    per_group = nh // ng
    c = pl.program_id(1)

    @pl.when(c == 0)
    def _():
        state_ref[...] = s0_ref[...]
        xcarry_ref[0:3, :] = hist_ref[...]

    x = xbc_ref[...]
    prev = xcarry_ref[0:3, :]
    s1, s2, s3 = _shifted_rows(x, prev, 3)
    xc =cw_ref[0:1, :] * s3 + cw_ref[1:2, :] * s2 + cw_ref[2:3, :] * s1 + cw_ref[3:4, :] * x + cb_ref[...]
    xc = _silu(xc)
    last = x[lc - 3:lc, :]
    xcarry_ref[0:3, :] = last
    cnew_ref[...] = last
    xs = xc[:, :ds]
    bm = xc[:, ds:ds + ng * ns].astype(BF16)
    cm = xc[:, ds + ng * ns:].astype(BF16)

    raw = dt_ref[...] + dtb_ref[...]
    dt = jnp.maximum(raw, 0.0) + jnp.log1p(jnp.exp(-jnp.abs(raw)))
    a = dt * (-jnp.exp(alog_ref[...]))
    ri = lax.broadcasted_iota(I32, (lc, lc), 0)
    ci = lax.broadcasted_iota(I32, (lc, lc), 1)
    causal = ri >= ci
    a_cum = jnp.dot(causal.astype(F32), a, precision=HIGHEST, preferred_element_type=F32)
    eye = (lax.broadcasted_iota(I32, (LANES, LANES), 0) == lax.broadcasted_iota(I32, (LANES, LANES), 1))
    nt = (((1,), (1,)), ((), ()))
    a_cum_t = lax.dot_general(eye.astype(F32), a_cum, nt, precision=HIGHEST, preferred_element_type=F32)
    both = jnp.concatenate([dt, a_cum], axis=0)
    hi = both.astype(BF16)
    rest = both - hi.astype(F32)
    mid = rest.astype(BF16)
    lo = (rest - mid.astype(F32)).astype(BF16)
    wide = jnp.dot(jnp.concatenate([hi, mid, lo], axis=1), expand_ref[...], preferred_element_type=F32)
    dt_x, a_x = wide[:lc, :], wide[lc:, :]
    a_end = a_x[lc - 1:lc, :]
    xd = xs * dt_x

    for g in range(ng):
        cb = lax.dot_general(cm[:, g * ns:(g + 1) * ns], bm[:, g * ns:(g + 1) * ns], nt, preferred_element_type=F32)
        for r in range(per_group):
            h = g * per_group + r
            seg = a_cum[:, h:h + 1] - a_cum_t[h:h + 1, :]
            lmat = jnp.exp(jnp.where(causal, seg, -jnp.inf))
            ydiag_ref[:, h * hp:(h + 1) * hp] = jnp.dot((cb * lmat).astype(BF16), xd[:, h * hp:(h + 1) * hp].astype(BF16),
                                                        preferred_element_type=F32)

    state = state_ref[...]
    xdd = (xd * jnp.exp(a_end - a_x)).astype(BF16)
    w = per_group * hp
    y_off, new_cols = [], []
    for g in range(ng):
        y_off.append(jnp.dot(cm[:, g * ns:(g + 1) * ns], state[:, g * w:(g + 1) * w].astype(BF16),
                             preferred_element_type=F32))
        b_t = lax.dot_general(eye.astype(BF16), bm[:, g * ns:(g + 1) * ns], nt, preferred_element_type=F32).astype(BF16)
        new_cols.append(jnp.dot(b_t, xdd[:, g * w:(g + 1) * w], preferred_element_type=F32))
    state_ref[...] = state * jnp.exp(a_end) + jnp.concatenate(new_cols, axis=1)
    y = ydiag_ref[...] + jnp.concatenate(y_off, axis=1) * jnp.exp(a_x) + dskip_ref[...] * xs

    z = z_ref[...]
    y = y * _silu(z)
    gw = ds // ng
    for g in range(ng):
        yg = y[:, g * gw:(g + 1) * gw]
        yg = yg * lax.rsqrt(jnp.mean(yg * yg, axis=-1, keepdims=True) + cfg.eps)
        y_ref[:, g * gw:(g + 1) * gw] = (yg * ng_ref[:, g * gw:(g + 1) * gw]).astype(y_ref.dtype)


def _ssd(cfg, proj, narrow, t_seq, hist, state0_t, conv_w, conv_b, dt_bias, a_log, d_skip, norm_g):
    m = proj.shape[0]
    n_seq = m // t_seq
    lc = _tile(t_seq, cfg.ssd_chunk)
    nc = t_seq // lc
    ds, cd, ns = cfg.d_ssd, cfg.conv_dim, cfg.d_state
    assert cfg.off["z"] % ds == 0 and cfg.off["xbc"] % cd == 0
    zb, xb, db = cfg.off["z"] // ds, cfg.off["xbc"] // cd, cfg.off_small["dt"] // LANES
    pad = LANES - cfg.ssd_heads
    lane_row = lambda v: jnp.pad(v.astype(F32), (0, pad)).reshape(1, LANES)
    expand = (jnp.arange(LANES)[:, None] == (jnp.arange(ds) // cfg.ssd_head_dim)[None, :]).astype(BF16)
    expand = jnp.concatenate([expand] * 3, axis=0)
    const = lambda shape: pl.BlockSpec(shape, lambda b, c: (0,) * len(shape))
    return pl.pallas_call(
        functools.partial(_ssd_body, cfg, lc),
        grid=(n_seq, nc),
        in_specs=[pl.BlockSpec((lc, ds), lambda b, c: (b * nc + c, zb)),
                  pl.BlockSpec((lc, cd), lambda b, c: (b * nc + c, xb)),
                  pl.BlockSpec((lc, LANES), lambda b, c: (b * nc + c, db)),
                  pl.BlockSpec((None, 3, cd), lambda b, c: (b, 0, 0)),
                  pl.BlockSpec((None, ns, ds), lambda b, c: (b, 0, 0)),
                  const((cfg.ssd_conv, cd)), const((1, cd)), const((1, LANES)), const((1, LANES)), const((1, ds)),
                  const((1, ds)), const((3 * LANES, ds))],
        out_specs=[pl.BlockSpec((lc, ds), lambda b, c: (b * nc + c, 0)),
                   pl.BlockSpec((None, ns, ds), lambda b, c: (b, 0, 0)),
                   pl.BlockSpec((None, 3, cd), lambda b, c: (b, 0, 0))],
        out_shape=[jax.ShapeDtypeStruct((m, ds), BF16),
                   jax.ShapeDtypeStruct((n_seq, ns, ds), F32),
                   jax.ShapeDtypeStruct((n_seq, 3, cd), F32)],
        scratch_shapes=[pltpu.VMEM((8, cd), F32), pltpu.VMEM((lc, ds), F32)],
        compiler_params=_cparams(("parallel", "arbitrary")),
        name="ssd_scan",
    )(proj, proj, narrow, hist, state0_t, conv_w, conv_b.reshape(1, cd), lane_row(dt_bias), lane_row(a_log),
      jnp.repeat(d_skip.astype(F32), cfg.ssd_head_dim).reshape(1, ds), norm_g.reshape(1, ds), expand)


NEAR_COLS = 640
NEAR_BACK = 512
TK_NEAR = 128
TK_IDX = 512
QK_AHEAD_FAR, QK_AHEAD_NEAR = 1, 2
SEARCH_BITS_FIRST = 27


def _t5_bucket(cfg, rel):
    nb = cfg.n_buckets // 2
    max_exact = nb // 2
    ret = jnp.where(rel > 0, nb, 0)
    n = jnp.abs(rel)
    nf = jnp.maximum(n, 1).astype(F32)
    large = max_exact + (jnp.log(nf / max_exact) / math.log(cfg.max_distance / max_exact) * (nb - max_exact)).astype(I32)
    large = jnp.minimum(large, nb - 1)
    return ret + jnp.where(n < max_exact, n, large)


def _near_bias(cfg, rel_bias, tq):
    assert cfg.max_distance <= LANES
    rel = jnp.arange(NEAR_COLS, dtype=I32)[:, None] - NEAR_BACK - jnp.arange(tq, dtype=I32)[None, :]
    far = rel_bias[_t5_bucket(cfg, jnp.asarray(-cfg.max_distance, I32))].astype(F32)
    onehot = (_t5_bucket(cfg, rel)[..., None] == jnp.arange(cfg.n_buckets, dtype=I32)).astype(F32)
    tab = jnp.einsum("ctb,bh->cth", onehot, rel_bias.astype(F32), precision=HIGHEST)
    tab = (tab - far) * LOG2E
    grp = cfg.n_heads // cfg.n_kv
    tab = tab.reshape(NEAR_COLS, tq, cfg.n_kv, grp)
    return jnp.transpose(tab, (2, 0, 3, 1)).reshape(cfg.n_kv, NEAR_COLS, grp * tq)


def _dsa_body(cfg, tq, past, n_select, q_ref, qi_ref, kiw_ref, k_ref, v_ref, kidx_ref, bias_ref, o_ref,
              keys_ref, qs_ref, qis_ref, w_ref, m_ref, l_ref, acc_ref, s_ref, thr_ref):
    nkv, hd, di, nih = cfg.n_kv, cfg.head_dim, cfg.idx_dim, cfg.n_idx_heads
    grp = cfg.n_heads // nkv
    nt = (((1,), (1,)), ((), ()))
    tn = (((0,), (0,)), ((), ()))
    eye = lax.broadcasted_iota(I32, (LANES, LANES), 0) == lax.broadcasted_iota(I32, (LANES, LANES), 1)
    eye_bf = eye.astype(BF16)
    x0 = past + pl.program_id(1) * tq
    k_end = x0 + tq
    n_idx = (k_end + TK_IDX - 1) // TK_IDX

    q = (q_ref[...] * ((hd ** -0.5) * LOG2E)).astype(BF16)
    for n in range(nkv):
        for g in range(grp):
            h = n * grp + g
            qs_ref[n, :, g * tq:(g + 1) * tq] = lax.dot_general(
                eye_bf, q[:, h * hd:(h + 1) * hd], nt, preferred_element_type=F32).astype(BF16)
    qi = qi_ref[...].astype(BF16)
    per_blk = LANES // di
    for j in range(nih // per_blk):
        t_blk = lax.dot_general(eye_bf, qi[:, j * LANES:(j + 1) * LANES], nt, preferred_element_type=F32)
        for r in range(per_blk):
            h = j * per_blk + r
            qis_ref[:, h * tq:(h + 1) * tq] = t_blk[r * di:(r + 1) * di, :].astype(BF16)
    kiw_t = lax.dot_general(eye.astype(F32), kiw_ref[...], nt, precision=HIGHEST, preferred_element_type=F32)
    w_ref[...] = kiw_t[di:di + nih, :] * ((di ** -0.5) * (nih ** -0.5))

    qpos = x0 + lax.broadcasted_iota(I32, (1, tq), 1)
    limit = (jnp.right_shift(qpos, int(math.log2(cfg.chunk))) + 1) * cfg.chunk
    krow = lax.broadcasted_iota(I32, (TK_IDX, tq), 0)

    def idx_tile(t, carry):
        k0 = pl.multiple_of(t * TK_IDX, TK_IDX)
        logits = jnp.dot(kidx_ref[pl.ds(k0, TK_IDX), :], qis_ref[...], preferred_element_type=F32)
        sc = jnp.zeros((TK_IDX, tq), F32)
        for h in range(nih):
            sc = sc + w_ref[h:h + 1, :] * jnp.maximum(logits[:, h * tq:(h + 1) * tq], 0.0)
        bits = pltpu.bitcast(jnp.where(sc == 0.0, 0.0, sc), I32)
        key = jnp.where(bits < 0, bits ^ 0x7FFFFFFF, bits)
        keys_ref[pl.ds(k0, TK_IDX), :] = jnp.where(k0 + krow < limit, key, INT_MIN)
        return carry

    lax.fori_loop(0, n_idx, idx_tile, 0)

    def count(pred):
        def body(t, accs):
            k0 = pl.multiple_of(t * TK_IDX, TK_IDX)
            blk = keys_ref[pl.ds(k0, TK_IDX), :]
            accs = list(accs)
            for r in range(TK_IDX // SUBLANES):
                hit = pred(blk[r * SUBLANES:(r + 1) * SUBLANES, :])
                accs[r % len(accs)] = jnp.where(hit, accs[r % len(accs)] + 1.0, accs[r % len(accs)])
            return tuple(accs)

        accs = lax.fori_loop(0, n_idx, body, (jnp.zeros((SUBLANES, tq), F32),) * 8)
        return jnp.sum(functools.reduce(lambda a, b: a + b, accs), axis=0, keepdims=True)

    settled0 = jnp.where(limit < n_select, 1.0, 0.0)

    def bit_step(s, c):
        thr, settled = c
        bit = 31 - s
        cand = jnp.where(bit == 31, jnp.zeros_like(thr), thr | jnp.left_shift(jnp.int32(1), bit))
        cand8 = jnp.broadcast_to(cand, (SUBLANES, tq))
        kept = count(lambda b: b >= cand8)
        thr = jnp.where(kept >= float(n_select), cand, thr)
        return thr, jnp.maximum(settled, jnp.where(kept == float(n_select), 1.0, 0.0))

    thr, settled = lax.fori_loop(0, SEARCH_BITS_FIRST, bit_step, (jnp.full((1, tq), INT_MIN, I32), settled0))
    thr_ref[...] = thr

    @pl.when(jnp.sum(1.0 - settled) > 0.0)
    def _():
        thr = lax.fori_loop(SEARCH_BITS_FIRST, 32, bit_step, (thr_ref[...], settled))[0]
        thr_ref[...] = thr
        thr = jnp.maximum(thr, INT_MIN + 1)
        thr8 = jnp.broadcast_to(thr, (SUBLANES, tq))
        need = float(n_select) - count(lambda b: b > thr8)
        n_tied = count(lambda b: b == thr8)

        @pl.when(jnp.max(n_tied - need) > 0.0)
        def _():
            tri = (lax.broadcasted_iota(I32, (TK_IDX, TK_IDX), 0) >= lax.broadcasted_iota(I32, (TK_IDX, TK_IDX), 1))
            tri = tri.astype(BF16)

            def drop_late_ties(t, seen):
                k0 = pl.multiple_of(t * TK_IDX, TK_IDX)
                blk = keys_ref[pl.ds(k0, TK_IDX), :]
                tied = blk == thr
                rank = seen + jnp.dot(tri, jnp.where(tied, 1.0, 0.0).astype(BF16), preferred_element_type=F32)
                keys_ref[pl.ds(k0, TK_IDX), :] = jnp.where(tied, jnp.where(rank > need, INT_MIN, blk), blk)
                return rank[TK_IDX - 1:TK_IDX, :]

            lax.fori_loop(0, n_idx, drop_late_ties, jnp.zeros((1, tq), F32))

    thr = jnp.maximum(thr_ref[...], INT_MIN + 1)

    far_end = jnp.maximum(x0 - LANES, 0) // cfg.tk_far * cfg.tk_far
    n_far = far_end // cfg.tk_far
    n_near = ((k_end + TK_NEAR - 1) // TK_NEAR * TK_NEAR - far_end) // TK_NEAR
    m_ref[...] = jnp.full(m_ref.shape, M_INIT, F32)
    l_ref[...] = jnp.zeros(l_ref.shape, F32)
    acc_ref[...] = jnp.zeros(acc_ref.shape, F32)

    def key_tiles(n_tiles, first, width, with_bias, qk_ahead):
        ahead = min(qk_ahead, nkv - 1)

        def logits(n, k0):
            s = jnp.dot(k_ref[pl.ds(k0, width), n * hd:(n + 1) * hd], qs_ref[n], preferred_element_type=F32)
            if with_bias:
                s = s + bias_ref[n, pl.ds(pl.multiple_of(k0 - (x0 - NEAR_BACK), TK_NEAR), width), :]
            sel = keys_ref[pl.ds(k0, width), :] >= thr
            for g in range(grp):
                s_ref[n, 0:width, g * tq:(g + 1) * tq] = jnp.where(sel, s[:, g * tq:(g + 1) * tq], NEG_BIG)

        start = lambda t: pl.multiple_of(first + t * width, TK_NEAR)
        for n in range(ahead):
            logits(n, start(0))

        def body(t, carry):
            k0 = start(t)
            k_next = start(jnp.minimum(t + 1, n_tiles - 1))
            for n in range(nkv):
                later = n + ahead
                logits(later % nkv, k0 if later < nkv else k_next)
                s = s_ref[n, 0:width, :]
                m_prev = m_ref[n]
                m_new = jnp.maximum(m_prev, jnp.max(s, axis=0, keepdims=True))
                alpha = jnp.exp2(m_prev - m_new)
                p = jnp.exp2(s - m_new)
                l_ref[n] = alpha * l_ref[n] + jnp.sum(p, axis=0, keepdims=True)
                acc_ref[n] = alpha * acc_ref[n] + lax.dot_general(
                    v_ref[pl.ds(k0, width), n * hd:(n + 1) * hd], p.astype(BF16), tn, preferred_element_type=F32)
                m_ref[n] = m_new
            return carry

        lax.fori_loop(0, n_tiles, body, 0)

    key_tiles(n_far, 0, cfg.tk_far, False, QK_AHEAD_FAR)
    key_tiles(n_near, far_end, TK_NEAR, True, QK_AHEAD_NEAR)
    for n in range(nkv):
        out_t = (acc_ref[n] / l_ref[n]).astype(BF16)
        for g in range(grp):
            h = n * grp + g
            o_ref[:, h * hd:(h + 1) * hd] = lax.dot_general(
                out_t[:, g * tq:(g + 1) * tq], eye_bf, tn, preferred_element_type=F32).astype(o_ref.dtype)


def _dsa(cfg, proj, narrow, t_seq, past, k_all, v_all, kidx_all, rel_bias):
    m = proj.shape[0]
    n_seq = m // t_seq
    tq = _tile(t_seq, cfg.tq)
    nq = t_seq // tq
    n_keys = past + t_seq
    lp = k_all.shape[1]
    assert tq % cfg.chunk == 0 and past % LANES == 0 and lp % TK_IDX == 0 and lp >= n_keys
    assert cfg.tk_far % TK_NEAR == 0 and NEAR_BACK == cfg.tk_far and NEAR_COLS == NEAR_BACK + LANES and tq <= LANES
    n_select = min(cfg.top_k_max, n_keys // 4)
    grp = cfg.n_heads // cfg.n_kv
    hq, hidx = cfg.hq, cfg.hidx
    assert cfg.off["q"] % hq == 0 and cfg.off["qi"] % hidx == 0
    assert cfg.head_dim == LANES and LANES % cfg.idx_dim == 0 and cfg.n_idx_heads % (LANES // cfg.idx_dim) == 0
    bias = _near_bias(cfg, rel_bias, tq)
    whole = lambda shape: pl.BlockSpec(shape, lambda b, i: (b,) + (0,) * (len(shape) - 1), pipeline_mode=pl.Buffered(1))
    return pl.pallas_call(
        functools.partial(_dsa_body, cfg, tq, past, n_select),
        grid=(n_seq, nq),
        in_specs=[pl.BlockSpec((tq, hq), lambda b, i: (b * nq + i, cfg.off["q"] // hq)),
                  pl.BlockSpec((tq, hidx), lambda b, i: (b * nq + i, cfg.off["qi"] // hidx)),
                  pl.BlockSpec((tq, LANES), lambda b, i: (b * nq + i, cfg.off_small["kiw"] // LANES)),
                  whole((None, lp, cfg.hkv)), whole((None, lp, cfg.hkv)), whole((None, lp, cfg.idx_dim)),
                  pl.BlockSpec(bias.shape, lambda b, i: (0, 0, 0), pipeline_mode=pl.Buffered(1))],
        out_specs=pl.BlockSpec((tq, hq), lambda b, i: (b * nq + i, 0)),
        out_shape=jax.ShapeDtypeStruct((m, hq), BF16),
        scratch_shapes=[pltpu.VMEM((lp, tq), I32),
                        pltpu.VMEM((cfg.n_kv, cfg.head_dim, grp * tq), BF16),
                        pltpu.VMEM((cfg.idx_dim, cfg.n_idx_heads * tq), BF16),
                        pltpu.VMEM((cfg.n_idx_heads, tq), F32),
                        pltpu.VMEM((cfg.n_kv, 1, grp * tq), F32), pltpu.VMEM((cfg.n_kv, 1, grp * tq), F32),
                        pltpu.VMEM((cfg.n_kv, cfg.head_dim, grp * tq), F32),
                        pltpu.VMEM((cfg.n_kv, cfg.tk_far, grp * tq), F32),
                        pltpu.VMEM((1, tq), I32)],
        compiler_params=_cparams(("parallel", "arbitrary")),
        name="dsa_attention",
    )(proj, proj, narrow, k_all, v_all, kidx_all, bias)


def _narrow_w_in(cfg, w_in_t):
    row_pad = lambda a: jnp.pad(a, ((0, LANES - a.shape[0]), (0, 0)))
    kiw = w_in_t[cfg.src_kiw:cfg.src_kiw + cfg.idx_dim + cfg.n_idx_heads]
    dt = w_in_t[cfg.src_dt:cfg.src_dt + cfg.ssd_heads]
    return jnp.concatenate([row_pad(kiw), row_pad(dt)], axis=0).astype(BF16)


def _pad_keys(a, lp):
    return jnp.pad(a, ((0, 0), (0, lp - a.shape[1]), (0, 0))).astype(BF16)


def _trunk_layer(cfg, x, mod, past_k, past_v, past_ik, ssm0, ssd_conv0, ffn_conv0, rel_bias, wts):
    bsz, t, d = x.shape
    m = bsz * t
    past = past_k.shape[1]
    x2d = x.reshape(m, d)
    off = cfg.off

    u = _norm_mod(cfg, x2d, wts["norm_mix_g"], mod, t, 0, 1)
    proj = _in_proj(cfg, u, wts["w_in_t"])
    narrow = _matmul_nt(u, wts["w_in_narrow"], cfg.tm, 2 * LANES, F32, "in_proj_narrow")
    k_new = proj[:, off["k"]:off["k"] + cfg.hkv].reshape(bsz, t, cfg.hkv)
    v_new = proj[:, off["v"]:off["v"] + cfg.hkv].reshape(bsz, t, cfg.hkv)
    ki_new = narrow[:, cfg.off_small["kiw"]:cfg.off_small["kiw"] + cfg.idx_dim].reshape(bsz, t, cfg.idx_dim)
    lp = -(-(past + t) // TK_IDX) * TK_IDX
    k_all = _pad_keys(jnp.concatenate([past_k.reshape(bsz, past, cfg.hkv), k_new], axis=1), lp)
    v_all = _pad_keys(jnp.concatenate([past_v.reshape(bsz, past, cfg.hkv), v_new], axis=1), lp)
    ki_all = _pad_keys(jnp.concatenate([past_ik, ki_new], axis=1), lp)
    attn = _dsa(cfg, proj, narrow, t, past, k_all, v_all, ki_all, rel_bias)

    state0_t = jnp.transpose(ssm0.astype(F32), (0, 3, 1, 2)).reshape(bsz, cfg.d_state, cfg.d_ssd)
    ssd_out, state_t, ssd_conv_new = _ssd(cfg, proj, narrow, t, ssd_conv0, state0_t, wts["ssd_conv_w"], wts["ssd_conv_b"],
                                          wts["dt_bias"], wts["a_log"], wts["d_skip"], wts["ssd_norm_g"])
    h_new = jnp.transpose(state_t.reshape(bsz, cfg.d_state, cfg.ssd_heads, cfg.ssd_head_dim), (0, 2, 3, 1))

    merged = _merge(cfg, attn, ssd_out, wts["w_attn_o"], wts["w_ssd_o"], proj)
    x1 = _gated_residual(merged, wts["w_out"], x2d, mod, t, 2, cfg.tm, cfg.tn, "out_proj_residual")

    u2 = _norm_mod(cfg, x1, wts["norm_ffn_g"], mod, t, 3, 4)
    h, ffn_conv_new = _ffn_up(cfg, u2, wts["w_up"], wts["ffn_conv_w"], wts["ffn_conv_b"], ffn_conv0, t)
    x2 = _gated_residual(h, wts["w_down"], x1, mod, t, 5, cfg.tm_down, cfg.tn_down, "down_proj_residual")
    states = (k_new.reshape(bsz, t, cfg.n_kv, cfg.head_dim), v_new.reshape(bsz, t, cfg.n_kv, cfg.head_dim), ki_new,
              h_new.astype(ssm0.dtype), ssd_conv_new, ffn_conv_new)
    return x2.reshape(bsz, t, d), states


def _forward(cfg, x_prompt, x_sample, c_prompt, c_sample, cache_k, cache_v, cache_idx_k, state_ssm, state_ssd_conv,
             state_ffn_conv, rel_bias, w_ada, b_ada, norm_mix_g, w_in, ssd_conv_w, ssd_conv_b, dt_bias, a_log, d_skip,
             ssd_norm_g, w_attn_o, w_ssd_o, w_out, norm_ffn_g, w_up, ffn_conv_w, ffn_conv_b, w_down, final_norm_g):
    depth = w_in.shape[0]
    bp, tp, d = x_prompt.shape
    bs, ts, _ = x_sample.shape
    dt_ = x_prompt.dtype
    hp, hs = x_prompt, x_sample
    c_all = jnp.concatenate([c_prompt, c_sample], axis=0)
    c_all = jnp.pad(c_all, ((0, -(bp + bs) % 8), (0, 0)))
    prompt_states, sample_states = [], []
    for l in range(depth):
        mod = _modulation(cfg, c_all, w_ada[l], b_ada[l]).reshape(c_all.shape[0], 6, d)
        wts = dict(norm_mix_g=norm_mix_g[l], w_in_t=w_in[l].T, w_in_narrow=_narrow_w_in(cfg, w_in[l].T), ssd_conv_w=ssd_conv_w[l],
                   ssd_conv_b=ssd_conv_b[l], dt_bias=dt_bias[l], a_log=a_log[l], d_skip=d_skip[l],
                   ssd_norm_g=ssd_norm_g[l], w_attn_o=w_attn_o[l], w_ssd_o=w_ssd_o[l],
                   w_out=w_out[l], norm_ffn_g=norm_ffn_g[l], w_up=w_up[l],
                   ffn_conv_w=ffn_conv_w[l], ffn_conv_b=ffn_conv_b[l], w_down=w_down[l].astype(BF16))
        hp, st_p = _trunk_layer(cfg, hp, mod[:bp],
                                jnp.zeros((bp, 0, cfg.n_kv, cfg.head_dim), dt_),
                                jnp.zeros((bp, 0, cfg.n_kv, cfg.head_dim), dt_),
                                jnp.zeros((bp, 0, cfg.idx_dim), dt_),
                                jnp.zeros((bp, cfg.ssd_heads, cfg.ssd_head_dim, cfg.d_state), state_ssm.dtype),
                                jnp.zeros((bp, cfg.ssd_conv - 1, cfg.conv_dim), dt_),
                                jnp.zeros((bp, cfg.ffn_conv - 1, 2 * cfg.d_ff), dt_),
                                rel_bias, wts)
        hs, st_s = _trunk_layer(cfg, hs, mod[bp:bp + bs], cache_k[l], cache_v[l], cache_idx_k[l], state_ssm[l],
                                state_ssd_conv[l], state_ffn_conv[l], rel_bias, wts)
        prompt_states.append(st_p)
        sample_states.append(st_s)
    y_prompt = _final_norm(cfg, hp.reshape(bp * tp, d), final_norm_g).reshape(bp, tp, d)
    y_sample = _final_norm(cfg, hs.reshape(bs * ts, d), final_norm_g).reshape(bs, ts, d)
    stack = lambda states, i: jnp.stack([s[i] for s in states], axis=0)
    return (y_prompt, y_sample) + tuple(stack(prompt_states, i) for i in range(6)) + tuple(
        stack(sample_states, i) for i in range(6))


def kernel(x_prompt, x_sample, c_prompt, c_sample, cache_k, cache_v, cache_idx_k, state_ssm, state_ssd_conv,
           state_ffn_conv, rel_bias, w_ada, b_ada, norm_mix_g, w_in, ssd_conv_w, ssd_conv_b, dt_bias, a_log, d_skip,
           ssd_norm_g, w_attn_o, w_ssd_o, w_out, norm_ffn_g, w_up, ffn_conv_w, ffn_conv_b, w_down, final_norm_g):
    return _forward(Cfg(), x_prompt, x_sample, c_prompt, c_sample, cache_k, cache_v, cache_idx_k, state_ssm,
                    state_ssd_conv, state_ffn_conv, rel_bias, w_ada, b_ada, norm_mix_g, w_in, ssd_conv_w, ssd_conv_b,
                    dt_bias, a_log, d_skip, ssd_norm_g, w_attn_o, w_ssd_o, w_out, norm_ffn_g, w_up, ffn_conv_w,
                    ffn_conv_b, w_down, final_norm_g)
```

```python
import functools
import math

import numpy as np
import jax
import jax.numpy as jnp
from jax import lax
from jax.experimental import pallas as pl
from jax.experimental.pallas import tpu as pltpu

F32 = jnp.float32
BF16 = jnp.bfloat16
I32 = jnp.int32

LANES = 128
SUBLANES = 8
V7X_VMEM_BYTES = 64 * 1024 * 1024
VMEM_LIMIT = V7X_VMEM_BYTES - 8 * 1024 * 1024
INT_MIN = -(2 ** 31)
NEG_BIG = -1e30
M_INIT = -1e29
LOG2E = math.log2(math.e)
HIGHEST = lax.Precision.HIGHEST


class Cfg:
    def __init__(self, **kw):
        self.d_model = 4096
        self.chunk = 64
        self.n_heads = 16
        self.n_kv = 4
        self.head_dim = 128
        self.n_idx_heads = 16
        self.idx_dim = 64
        self.top_k_max = 256
        self.n_buckets = 32
        self.max_distance = 128
        self.ssd_heads = 32
        self.ssd_head_dim = 64
        self.ssd_groups = 4
        self.d_state = 128
        self.ssd_conv = 4
        self.d_ff = 11008
        self.ffn_conv = 3
        self.eps = 1e-6
        self.tm = 1024
        self.tn_in = 768
        self.tn = 512
        self.tn_ff = 256
        self.tm_down = 512
        self.tn_down = 512
        self.tr = 512
        self.tq = 128
        self.tk_far = 512
        self.ssd_chunk = 128
        self.tn_mod = 512
        for k, v in kw.items():
            assert hasattr(self, k), k
            setattr(self, k, v)
        self.d_ssd = self.ssd_heads * self.ssd_head_dim
        self.conv_dim = self.d_ssd + 2 * self.ssd_groups * self.d_state
        self.hq = self.n_heads * self.head_dim
        self.hkv = self.n_kv * self.head_dim
        self.hidx = self.n_idx_heads * self.idx_dim
        self.in_sizes = (self.hq, self.hkv, self.hkv, self.hidx, self.idx_dim, self.n_idx_heads, self.d_ssd,
                         self.conv_dim, self.ssd_heads, self.d_model, self.d_model)
        segs = [("q", self.hq), ("k", self.hkv), ("v", self.hkv), ("qi", self.hidx), ("z", self.d_ssd),
                ("xbc", self.conv_dim), ("ga", self.d_model), ("gs", self.d_model), ("kiw", LANES), ("dt", LANES)]
        off, self.off = 0, {}
        for name, width in segs:
            assert width % LANES == 0
            self.off[name] = off
            off += width
        self.n_packed = -(-off // self.tn_in) * self.tn_in
        assert self.idx_dim + self.n_idx_heads <= LANES and self.ssd_heads <= LANES


def _silu(x):
    half = 0.5 * x
    return half + half * jnp.tanh(half)


def _cparams(sem):
    return pltpu.CompilerParams(dimension_semantics=sem, vmem_limit_bytes=VMEM_LIMIT)


def _tile(n, pref):
    t = min(n, pref)
    assert n % t == 0, (n, pref)
    return t


def _seq_map(t_seq, tm):
    if t_seq >= tm:
        assert t_seq % tm == 0
        per = t_seq // tm
        return tm, 1, (lambda i: i // per)
    assert tm % t_seq == 0
    return t_seq, tm // t_seq, (lambda i: i)


def _mod_body(c_ref, w_ref, b_ref, o_ref):
    c = c_ref[...]
    a = (c * jax.nn.sigmoid(c)).astype(BF16)
    o_ref[...] = jnp.dot(a, w_ref[...].astype(BF16), preferred_element_type=F32) + b_ref[...]


def _modulation(cfg, c, w_ada, b_ada):
    rows, d = c.shape
    n = w_ada.shape[1]
    tn = _tile(n, cfg.tn_mod)
    return pl.pallas_call(
        _mod_body,
        grid=(n // tn,),
        in_specs=[pl.BlockSpec((rows, d), lambda j: (0, 0)),
                  pl.BlockSpec((d, tn), lambda j: (0, j)),
                  pl.BlockSpec((1, tn), lambda j: (0, j))],
        out_specs=pl.BlockSpec((rows, tn), lambda j: (0, j)),
        out_shape=jax.ShapeDtypeStruct((rows, n), F32),
        compiler_params=_cparams(("parallel",)),
        name="adaln_mod",
    )(c, w_ada, b_ada.reshape(1, n))


def _norm_mod_body(eps, sh_row, sc_row, x_ref, g_ref, mod_ref, o_ref):
    x = x_ref[...]
    y = x * lax.rsqrt(jnp.mean(x * x, axis=-1, keepdims=True) + eps) * g_ref[...]
    y = y * (1.0 + mod_ref[sc_row:sc_row + 1, :]) + mod_ref[sh_row:sh_row + 1, :]
    o_ref[...] = y.astype(o_ref.dtype)


def _norm_mod(cfg, x, g, mod, t_seq, sh_row, sc_row):
    m, d = x.shape
    tr = _tile(t_seq, cfg.tr)
    per = t_seq // tr
    return pl.pallas_call(
        functools.partial(_norm_mod_body, cfg.eps, sh_row, sc_row),
        grid=(m // tr,),
        in_specs=[pl.BlockSpec((tr, d), lambda i: (i, 0)),
                  pl.BlockSpec((1, d), lambda i: (0, 0)),
                  pl.BlockSpec((None, 6, d), lambda i: (i // per, 0, 0))],
        out_specs=pl.BlockSpec((tr, d), lambda i: (i, 0)),
        out_shape=jax.ShapeDtypeStruct((m, d), BF16),
        compiler_params=_cparams(("parallel",)),
        name="rmsnorm_mod",
    )(x, g.reshape(1, d), mod)


def _norm_body(eps, x_ref, g_ref, o_ref):
    x = x_ref[...]
    o_ref[...] = x * lax.rsqrt(jnp.mean(x * x, axis=-1, keepdims=True) + eps) * g_ref[...]


def _final_norm(cfg, x, g):
    m, d = x.shape
    tr = _tile(m, cfg.tr)
    return pl.pallas_call(
        functools.partial(_norm_body, cfg.eps),
        grid=(m // tr,),
        in_specs=[pl.BlockSpec((tr, d), lambda i: (i, 0)), pl.BlockSpec((1, d), lambda i: (0, 0))],
        out_specs=pl.BlockSpec((tr, d), lambda i: (i, 0)),
        out_shape=jax.ShapeDtypeStruct((m, d), F32),
        compiler_params=_cparams(("parallel",)),
        name="final_rmsnorm",
    )(x, g.reshape(1, d))


def _mm_nt_body(a_ref, wt_ref, o_ref):
    o_ref[...] = lax.dot_general(a_ref[...], wt_ref[...], (((1,), (1,)), ((), ())),
                                 preferred_element_type=F32).astype(o_ref.dtype)


def _matmul_nt(a, w_t, tm, tn, out_dtype, name):
    m, k = a.shape
    n = w_t.shape[0]
    tm, tn = _tile(m, tm), _tile(n, tn)
    return pl.pallas_call(
        _mm_nt_body,
        grid=(m // tm, n // tn),
        in_specs=[pl.BlockSpec((tm, k), lambda i, j: (i, 0)), pl.BlockSpec((tn, k), lambda i, j: (j, 0))],
        out_specs=pl.BlockSpec((tm, tn), lambda i, j: (i, j)),
        out_shape=jax.ShapeDtypeStruct((m, n), out_dtype),
        compiler_params=_cparams(("parallel", "parallel")),
        name=name,
    )(a, w_t)


def _merge_body(attn_ref, ssd_ref, wa_ref, ws_ref, ga_ref, gs_ref, o_ref, wa_scr, ws_scr):
    @pl.when(pl.program_id(1) == 0)
    def _():
        wa_scr[...] = wa_ref[...].astype(wa_scr.dtype)
        ws_scr[...] = ws_ref[...].astype(ws_scr.dtype)

    a = jnp.dot(attn_ref[...], wa_scr[...], preferred_element_type=F32)
    s = jnp.dot(ssd_ref[...], ws_scr[...], preferred_element_type=F32)
    o_ref[...] = (jax.nn.sigmoid(ga_ref[...]) * a + jax.nn.sigmoid(gs_ref[...]) * s).astype(o_ref.dtype)


def _merge(cfg, attn, ssd, w_attn_o, w_ssd_o, proj):
    m = attn.shape[0]
    d = cfg.d_model
    tm = _tile(m, cfg.tm)
    tn = math.gcd(math.gcd(cfg.off["ga"], cfg.off["gs"]), _tile(d, cfg.tn))
    ga0, gs0 = cfg.off["ga"] // tn, cfg.off["gs"] // tn
    return pl.pallas_call(
        _merge_body,
        grid=(d // tn, m // tm),
        in_specs=[pl.BlockSpec((tm, attn.shape[1]), lambda j, i: (i, 0)),
                  pl.BlockSpec((tm, ssd.shape[1]), lambda j, i: (i, 0)),
                  pl.BlockSpec((attn.shape[1], tn), lambda j, i: (0, j)),
                  pl.BlockSpec((ssd.shape[1], tn), lambda j, i: (0, j)),
                  pl.BlockSpec((tm, tn), lambda j, i: (i, ga0 + j)),
                  pl.BlockSpec((tm, tn), lambda j, i: (i, gs0 + j))],
        out_specs=pl.BlockSpec((tm, tn), lambda j, i: (i, j)),
        out_shape=jax.ShapeDtypeStruct((m, d), BF16),
        scratch_shapes=[pltpu.VMEM((attn.shape[1], tn), attn.dtype), pltpu.VMEM((ssd.shape[1], tn), ssd.dtype)],
        compiler_params=_cparams(("arbitrary", "arbitrary")),
        name="branch_merge",
    )(attn, ssd, w_attn_o, w_ssd_o, proj, proj)


def _resid_body(rows, groups, gate_row, a_ref, w_ref, x_ref, mod_ref, o_ref, *w_scr):
    if w_scr:
        @pl.when(pl.program_id(1) == 0)
        def _():
            w_scr[0][...] = w_ref[...].astype(w_scr[0].dtype)
        w = w_scr[0][...]
    else:
        w = w_ref[...]
    acc = jnp.dot(a_ref[...], w, preferred_element_type=F32)
    for g in range(groups):
        sl = slice(g * rows, (g + 1) * rows)
        o_ref[sl, :] = x_ref[sl, :] + mod_ref[g, gate_row:gate_row + 1, :] * acc[sl, :]


def _gated_residual(a, w, x, mod, t_seq, gate_row, tm, tn, name):
    m, k = a.shape
    n = w.shape[1]
    tm, tn = _tile(m, tm), _tile(n, tn)
    rows, groups, seq_of = _seq_map(t_seq, tm)
    cast = w.dtype != a.dtype
    ij = (lambda f: (lambda j, i: f(i, j))) if cast else (lambda f: f)
    return pl.pallas_call(
        functools.partial(_resid_body, rows, groups, gate_row),
        grid=(n // tn, m // tm) if cast else (m // tm, n // tn),
        in_specs=[pl.BlockSpec((tm, k), ij(lambda i, j: (i, 0))),
                  pl.BlockSpec((k, tn), ij(lambda i, j: (0, j))),
                  pl.BlockSpec((tm, tn), ij(lambda i, j: (i, j))),
                  pl.BlockSpec((groups, 6, tn), ij(lambda i, j: (seq_of(i), 0, j)))],
        out_specs=pl.BlockSpec((tm, tn), ij(lambda i, j: (i, j))),
        out_shape=jax.ShapeDtypeStruct((m, n), F32),
        scratch_shapes=[pltpu.VMEM((k, tn), a.dtype)] if cast else [],
        compiler_params=_cparams(("arbitrary", "arbitrary") if cast else ("parallel", "parallel")),
        name=name,
    )(a, w, x, mod)


def _shifted_rows(x, prev, n):
    rid = lax.broadcasted_iota(I32, (SUBLANES, x.shape[1]), 0)
    out = []
    for k in range(1, n + 1):
        rolled = pltpu.roll(x, k, 0)
        head = rolled[0:SUBLANES, :]
        for r in range(k):
            head = jnp.where(rid == r, prev[n - k + r:n - k + r + 1, :], head)
        out.append(jnp.concatenate([head, rolled[SUBLANES:, :]], axis=0))
    return out


def _up_body(rows, groups, tiles_per_seq, a_ref, wg_ref, wv_ref, cwg_ref, cwv_ref, cbg_ref, cbv_ref, hg_ref, hv_ref,
             h_ref, ng_ref, nv_ref, up_ref, carry_ref, w_scr):
    i = pl.program_id(1)

    @pl.when(i == 0)
    def _():
        w_scr[0] = wg_ref[...].astype(w_scr.dtype)
        w_scr[1] = wv_ref[...].astype(w_scr.dtype)

    a = a_ref[...]
    stride = rows + SUBLANES
    hc = []
    for half, (cw_ref, cb_ref, hist_ref, new_ref) in enumerate(
            ((cwg_ref, cbg_ref, hg_ref, ng_ref), (cwv_ref, cbv_ref, hv_ref, nv_ref))):
        up = jnp.dot(a, w_scr[half], preferred_element_type=F32)
        outs = []
        for g in range(groups):
            base = g * stride + SUBLANES
            prev = hist_ref[g]
            if groups == 1 and tiles_per_seq > 1:
                prev = jnp.where(i % tiles_per_seq == 0, prev, carry_ref[half])
            up_ref[half, base - 2:base, :] = prev
            up_ref[half, base:base + rows, :] = up[g * rows:(g + 1) * rows, :]
            last = up[(g + 1) * rows - 2:(g + 1) * rows, :]
            new_ref[g] = last
            if groups == 1 and tiles_per_seq > 1:
                carry_ref[half] = last
            taps = [up_ref[half, base - k:base - k + rows, :] for k in (2, 1, 0)]
            outs.append(cw_ref[0:1, :] * taps[0] + cw_ref[1:2, :] * taps[1] + cw_ref[2:3, :] * taps[2] + cb_ref[...])
        hc.append(outs)
    for g in range(groups):
        gate, val = hc[0][g], hc[1][g]
        h_ref[g * rows:(g + 1) * rows, :] = (_silu(gate) * val).astype(h_ref.dtype)


def _ffn_up(cfg, u2, w_up, conv_w, conv_b, hist, t_seq):
    m, d = u2.shape
    nf = cfg.d_ff
    tm, tn = _tile(m, cfg.tm), _tile(nf, cfg.tn_ff)
    rows, groups, seq_of = _seq_map(t_seq, tm)
    nj = nf // tn
    wspec = lambda half: pl.BlockSpec((d, tn), lambda j, i: (0, j + half * nj))
    cspec = lambda r, half: pl.BlockSpec((r, tn), lambda j, i: (0, j + half * nj))
    hspec = lambda half: pl.BlockSpec((groups, 2, tn), lambda j, i: (seq_of(i), 0, j + half * nj))
    ospec = pl.BlockSpec((groups, 2, tn), lambda j, i: (i, 0, j))
    h, new_g, new_v = pl.pallas_call(
        functools.partial(_up_body, rows, groups, max(1, t_seq // tm)),
        grid=(nj, m // tm),
        in_specs=[pl.BlockSpec((tm, d), lambda j, i: (i, 0)), wspec(0), wspec(1),
                  cspec(3, 0), cspec(3, 1), cspec(1, 0), cspec(1, 1), hspec(0), hspec(1)],
        out_specs=[pl.BlockSpec((tm, tn), lambda j, i: (i, j)), ospec, ospec],
        out_shape=[jax.ShapeDtypeStruct((m, nf), BF16),
                   jax.ShapeDtypeStruct((m // rows, 2, nf), F32),
                   jax.ShapeDtypeStruct((m // rows, 2, nf), F32)],
        scratch_shapes=[pltpu.VMEM((2, groups * (rows + SUBLANES), tn), F32), pltpu.VMEM((2, 2, tn), F32),
                        pltpu.VMEM((2, d, tn), BF16)],
        compiler_params=_cparams(("arbitrary", "arbitrary")),
        name="ffn_up_conv_swiglu",
    )(u2, w_up, w_up, conv_w, conv_w, conv_b.reshape(1, -1), conv_b.reshape(1, -1), hist, hist)
    per_seq = t_seq // rows
    return h, jnp.concatenate([new_g[per_seq - 1::per_seq], new_v[per_seq - 1::per_seq]], axis=-1)


def _ssd_body(cfg, lc, z_ref, xbc_ref, dt_ref, hist_ref, s0_ref, cw_ref, cb_ref, dtb_ref, alog_ref, dskip_ref,
              ng_ref, expand_ref, y_ref, state_ref, cnew_ref, xcarry_ref, ydiag_ref):
    nh, hp, ng, ns = cfg.ssd_heads, cfg.ssd_head_dim, cfg.ssd_groups, cfg.d_state
    ds = cfg.d_ssd
    per_group = nh // ng
    c = pl.program_id(1)

    @pl.when(c == 0)
    def _():
        state_ref[...] = s0_ref[...]
        xcarry_ref[0:3, :] = hist_ref[...]

    x = xbc_ref[...]
    prev = xcarry_ref[0:3, :]
    s1, s2, s3 = _shifted_rows(x, prev, 3)
    xc =cw_ref[0:1, :] * s3 + cw_ref[1:2, :] * s2 + cw_ref[2:3, :] * s1 + cw_ref[3:4, :] * x + cb_ref[...]
    xc = _silu(xc)
    last = x[lc - 3:lc, :]
    xcarry_ref[0:3, :] = last
    cnew_ref[...] = last
    xs = xc[:, :ds]
    bm = xc[:, ds:ds + ng * ns].astype(BF16)
    cm = xc[:, ds + ng * ns:].astype(BF16)

    raw = dt_ref[...] + dtb_ref[...]
    dt = jnp.maximum(raw, 0.0) + jnp.log1p(jnp.exp(-jnp.abs(raw)))
    a = dt * (-jnp.exp(alog_ref[...]))
    ri = lax.broadcasted_iota(I32, (lc, lc), 0)
    ci = lax.broadcasted_iota(I32, (lc, lc), 1)
    causal = ri >= ci
    a_cum = jnp.dot(causal.astype(F32), a, precision=HIGHEST, preferred_element_type=F32)
    eye = (lax.broadcasted_iota(I32, (LANES, LANES), 0) == lax.broadcasted_iota(I32, (LANES, LANES), 1))
    nt = (((1,), (1,)), ((), ()))
    a_cum_t = lax.dot_general(eye.astype(F32), a_cum, nt, precision=HIGHEST, preferred_element_type=F32)
    both = jnp.concatenate([dt, a_cum], axis=0)
    hi = both.astype(BF16)
    rest = both - hi.astype(F32)
    mid = rest.astype(BF16)
    lo = (rest - mid.astype(F32)).astype(BF16)
    wide = jnp.dot(jnp.concatenate([hi, mid, lo], axis=1), expand_ref[...], preferred_element_type=F32)
    dt_x, a_x = wide[:lc, :], wide[lc:, :]
    a_end = a_x[lc - 1:lc, :]
    xd = xs * dt_x

    for g in range(ng):
        cb = lax.dot_general(cm[:, g * ns:(g + 1) * ns], bm[:, g * ns:(g + 1) * ns], nt, preferred_element_type=F32)
        for r in range(per_group):
            h = g * per_group + r
            seg = a_cum[:, h:h + 1] - a_cum_t[h:h + 1, :]
            lmat = jnp.exp(jnp.where(causal, seg, -jnp.inf))
            ydiag_ref[:, h * hp:(h + 1) * hp] = jnp.dot((cb * lmat).astype(BF16), xd[:, h * hp:(h + 1) * hp].astype(BF16),
                                                        preferred_element_type=F32)

    state = state_ref[...]
    xdd = (xd * jnp.exp(a_end - a_x)).astype(BF16)
    w = per_group * hp
    y_off, new_cols = [], []
    for g in range(ng):
        y_off.append(jnp.dot(cm[:, g * ns:(g + 1) * ns], state[:, g * w:(g + 1) * w].astype(BF16),
                             preferred_element_type=F32))
        b_t = lax.dot_general(eye.astype(BF16), bm[:, g * ns:(g + 1) * ns], nt, preferred_element_type=F32).astype(BF16)
        new_cols.append(jnp.dot(b_t, xdd[:, g * w:(g + 1) * w], preferred_element_type=F32))
    state_ref[...] = state * jnp.exp(a_end) + jnp.concatenate(new_cols, axis=1)
    y = ydiag_ref[...] + jnp.concatenate(y_off, axis=1) * jnp.exp(a_x) + dskip_ref[...] * xs

    z = z_ref[...]
    y = y * _silu(z)
    gw = ds // ng
    for g in range(ng):
        yg = y[:, g * gw:(g + 1) * gw]
        yg = yg * lax.rsqrt(jnp.mean(yg * yg, axis=-1, keepdims=True) + cfg.eps)
        y_ref[:, g * gw:(g + 1) * gw] = (yg * ng_ref[:, g * gw:(g + 1) * gw]).astype(y_ref.dtype)


def _ssd(cfg, proj, t_seq, hist, state0_t, conv_w, conv_b, dt_bias, a_log, d_skip, norm_g):
    m = proj.shape[0]
    n_seq = m // t_seq
    lc = _tile(t_seq, cfg.ssd_chunk)
    nc = t_seq // lc
    ds, cd, ns = cfg.d_ssd, cfg.conv_dim, cfg.d_state
    assert cfg.off["z"] % ds == 0 and cfg.off["xbc"] % cd == 0
    zb, xb, db = cfg.off["z"] // ds, cfg.off["xbc"] // cd, cfg.off["dt"] // LANES
    pad = LANES - cfg.ssd_heads
    lane_row = lambda v: jnp.pad(v.astype(F32), (0, pad)).reshape(1, LANES)
    expand = (jnp.arange(LANES)[:, None] == (jnp.arange(ds) // cfg.ssd_head_dim)[None, :]).astype(BF16)
    expand = jnp.concatenate([expand] * 3, axis=0)
    const = lambda shape: pl.BlockSpec(shape, lambda b, c: (0,) * len(shape))
    return pl.pallas_call(
        functools.partial(_ssd_body, cfg, lc),
        grid=(n_seq, nc),
        in_specs=[pl.BlockSpec((lc, ds), lambda b, c: (b * nc + c, zb)),
                  pl.BlockSpec((lc, cd), lambda b, c: (b * nc + c, xb)),
                  pl.BlockSpec((lc, LANES), lambda b, c: (b * nc + c, db)),
                  pl.BlockSpec((None, 3, cd), lambda b, c: (b, 0, 0)),
                  pl.BlockSpec((None, ns, ds), lambda b, c: (b, 0, 0)),
                  const((cfg.ssd_conv, cd)), const((1, cd)), const((1, LANES)), const((1, LANES)), const((1, ds)),
                  const((1, ds)), const((3 * LANES, ds))],
        out_specs=[pl.BlockSpec((lc, ds), lambda b, c: (b * nc + c, 0)),
                   pl.BlockSpec((None, ns, ds), lambda b, c: (b, 0, 0)),
                   pl.BlockSpec((None, 3, cd), lambda b, c: (b, 0, 0))],
        out_shape=[jax.ShapeDtypeStruct((m, ds), BF16),
                   jax.ShapeDtypeStruct((n_seq, ns, ds), F32),
                   jax.ShapeDtypeStruct((n_seq, 3, cd), F32)],
        scratch_shapes=[pltpu.VMEM((8, cd), F32), pltpu.VMEM((lc, ds), F32)],
        compiler_params=_cparams(("parallel", "arbitrary")),
        name="ssd_scan",
    )(proj, proj, proj, hist, state0_t, conv_w, conv_b.reshape(1, cd), lane_row(dt_bias), lane_row(a_log),
      jnp.repeat(d_skip.astype(F32), cfg.ssd_head_dim).reshape(1, ds), norm_g.reshape(1, ds), expand)


NEAR_COLS = 640
NEAR_BACK = 512
TK_NEAR = 128
TK_IDX = 512
QK_AHEAD_FAR, QK_AHEAD_NEAR = 1, 3
SEARCH_BITS_FIRST = 27


def _t5_bucket(cfg, rel):
    nb = cfg.n_buckets // 2
    max_exact = nb // 2
    ret = jnp.where(rel > 0, nb, 0)
    n = jnp.abs(rel)
    nf = jnp.maximum(n, 1).astype(F32)
    large = max_exact + (jnp.log(nf / max_exact) / math.log(cfg.max_distance / max_exact) * (nb - max_exact)).astype(I32)
    large = jnp.minimum(large, nb - 1)
    return ret + jnp.where(n < max_exact, n, large)


def _near_bias(cfg, rel_bias, tq):
    assert cfg.max_distance <= LANES
    rel = jnp.arange(NEAR_COLS, dtype=I32)[:, None] - NEAR_BACK - jnp.arange(tq, dtype=I32)[None, :]
    far = rel_bias[_t5_bucket(cfg, jnp.asarray(-cfg.max_distance, I32))].astype(F32)
    onehot = (_t5_bucket(cfg, rel)[..., None] == jnp.arange(cfg.n_buckets, dtype=I32)).astype(F32)
    tab = jnp.einsum("ctb,bh->cth", onehot, rel_bias.astype(F32), precision=HIGHEST)
    tab = (tab - far) * LOG2E
    grp = cfg.n_heads // cfg.n_kv
    tab = tab.reshape(NEAR_COLS, tq, cfg.n_kv, grp)
    return jnp.transpose(tab, (2, 0, 3, 1)).reshape(cfg.n_kv, NEAR_COLS, grp * tq)


def _dsa_body(cfg, tq, past, n_select, q_ref, qi_ref, kiw_ref, k_ref, v_ref, kidx_ref, bias_ref, o_ref,
              keys_ref, qs_ref, qis_ref, w_ref, m_ref, l_ref, acc_ref, s_ref, thr_ref):
    nkv, hd, di, nih = cfg.n_kv, cfg.head_dim, cfg.idx_dim, cfg.n_idx_heads
    grp = cfg.n_heads // nkv
    nt = (((1,), (1,)), ((), ()))
    tn = (((0,), (0,)), ((), ()))
    eye = lax.broadcasted_iota(I32, (LANES, LANES), 0) == lax.broadcasted_iota(I32, (LANES, LANES), 1)
    eye_bf = eye.astype(BF16)
    x0 = past + pl.program_id(1) * tq
    k_end = x0 + tq
    n_idx = (k_end + TK_IDX - 1) // TK_IDX

    q = (q_ref[...] * ((hd ** -0.5) * LOG2E)).astype(BF16)
    for n in range(nkv):
        for g in range(grp):
            h = n * grp + g
            qs_ref[n, :, g * tq:(g + 1) * tq] = lax.dot_general(
                eye_bf, q[:, h * hd:(h + 1) * hd], nt, preferred_element_type=F32).astype(BF16)
    qi = qi_ref[...].astype(BF16)
    per_blk = LANES // di
    for j in range(nih // per_blk):
        t_blk = lax.dot_general(eye_bf, qi[:, j * LANES:(j + 1) * LANES], nt, preferred_element_type=F32)
        for r in range(per_blk):
            h = j * per_blk + r
            qis_ref[:, h * tq:(h + 1) * tq] = t_blk[r * di:(r + 1) * di, :].astype(BF16)
    kiw_t = lax.dot_general(eye.astype(F32), kiw_ref[...], nt, precision=HIGHEST, preferred_element_type=F32)
    w_ref[...] = kiw_t[di:di + nih, :] * ((di ** -0.5) * (nih ** -0.5))

    qpos = x0 + lax.broadcasted_iota(I32, (1, tq), 1)
    limit = (jnp.right_shift(qpos, int(math.log2(cfg.chunk))) + 1) * cfg.chunk
    krow = lax.broadcasted_iota(I32, (TK_IDX, tq), 0)

    def idx_tile(t, carry):
        k0 = pl.multiple_of(t * TK_IDX, TK_IDX)
        logits = jnp.dot(kidx_ref[pl.ds(k0, TK_IDX), :], qis_ref[...], preferred_element_type=F32)
        sc = jnp.zeros((TK_IDX, tq), F32)
        for h in range(nih):
            sc = sc + w_ref[h:h + 1, :] * jnp.maximum(logits[:, h * tq:(h + 1) * tq], 0.0)
        bits = pltpu.bitcast(jnp.where(sc == 0.0, 0.0, sc), I32)
        key = jnp.where(bits < 0, bits ^ 0x7FFFFFFF, bits)
        keys_ref[pl.ds(k0, TK_IDX), :] = jnp.where(k0 + krow < limit, key, INT_MIN)
        return carry

    lax.fori_loop(0, n_idx, idx_tile, 0)

    def count(pred):
        def body(t, accs):
            k0 = pl.multiple_of(t * TK_IDX, TK_IDX)
            blk = keys_ref[pl.ds(k0, TK_IDX), :]
            accs = list(accs)
            for r in range(TK_IDX // SUBLANES):
                hit = pred(blk[r * SUBLANES:(r + 1) * SUBLANES, :])
                accs[r % len(accs)] = jnp.where(hit, accs[r % len(accs)] + 1.0, accs[r % len(accs)])
            return tuple(accs)

        accs = lax.fori_loop(0, n_idx, body, (jnp.zeros((SUBLANES, tq), F32),) * 8)
        return jnp.sum(functools.reduce(lambda a, b: a + b, accs), axis=0, keepdims=True)

    settled0 = jnp.where(limit < n_select, 1.0, 0.0)

    def bit_step(s, c):
        thr, settled = c
        bit = 31 - s
        cand = jnp.where(bit == 31, jnp.zeros_like(thr), thr | jnp.left_shift(jnp.int32(1), bit))
        cand8 = jnp.broadcast_to(cand, (SUBLANES, tq))
        kept = count(lambda b: b >= cand8)
        thr = jnp.where(kept >= float(n_select), cand, thr)
        return thr, jnp.maximum(settled, jnp.where(kept == float(n_select), 1.0, 0.0))

    thr, settled = lax.fori_loop(0, SEARCH_BITS_FIRST, bit_step, (jnp.full((1, tq), INT_MIN, I32), settled0))
    thr_ref[...] = thr

    @pl.when(jnp.sum(1.0 - settled) > 0.0)
    def _():
        thr = lax.fori_loop(SEARCH_BITS_FIRST, 32, bit_step, (thr_ref[...], settled))[0]
        thr_ref[...] = thr
        thr = jnp.maximum(thr, INT_MIN + 1)
        thr8 = jnp.broadcast_to(thr, (SUBLANES, tq))
        need = float(n_select) - count(lambda b: b > thr8)
        n_tied = count(lambda b: b == thr8)

        @pl.when(jnp.max(n_tied - need) > 0.0)
        def _():
            tri = (lax.broadcasted_iota(I32, (TK_IDX, TK_IDX), 0) >= lax.broadcasted_iota(I32, (TK_IDX, TK_IDX), 1))
            tri = tri.astype(BF16)

            def drop_late_ties(t, seen):
                k0 = pl.multiple_of(t * TK_IDX, TK_IDX)
                blk = keys_ref[pl.ds(k0, TK_IDX), :]
                tied = blk == thr
                rank = seen + jnp.dot(tri, jnp.where(tied, 1.0, 0.0).astype(BF16), preferred_element_type=F32)
                keys_ref[pl.ds(k0, TK_IDX), :] = jnp.where(tied, jnp.where(rank > need, INT_MIN, blk), blk)
                return rank[TK_IDX - 1:TK_IDX, :]

            lax.fori_loop(0, n_idx, drop_late_ties, jnp.zeros((1, tq), F32))

    thr = jnp.maximum(thr_ref[...], INT_MIN + 1)

    far_end = jnp.maximum(x0 - LANES, 0) // cfg.tk_far * cfg.tk_far
    n_far = far_end // cfg.tk_far
    n_near = ((k_end + TK_NEAR - 1) // TK_NEAR * TK_NEAR - far_end) // TK_NEAR
    m_ref[...] = jnp.full(m_ref.shape, M_INIT, F32)
    l_ref[...] = jnp.zeros(l_ref.shape, F32)
    acc_ref[...] = jnp.zeros(acc_ref.shape, F32)

    def key_tiles(n_tiles, first, width, with_bias, qk_ahead):
        ahead = min(qk_ahead, nkv - 1)

        def logits(n, k0):
            s = jnp.dot(k_ref[pl.ds(k0, width), n * hd:(n + 1) * hd], qs_ref[n], preferred_element_type=F32)
            if with_bias:
                s = s + bias_ref[n, pl.ds(pl.multiple_of(k0 - (x0 - NEAR_BACK), TK_NEAR), width), :]
            sel = keys_ref[pl.ds(k0, width), :] >= thr
            for g in range(grp):
                s_ref[n, 0:width, g * tq:(g + 1) * tq] = jnp.where(sel, s[:, g * tq:(g + 1) * tq], NEG_BIG)

        start = lambda t: pl.multiple_of(first + t * width, TK_NEAR)
        for n in range(ahead):
            logits(n, start(0))

        def body(t, carry):
            k0 = start(t)
            k_next = start(jnp.minimum(t + 1, n_tiles - 1))
            for n in range(nkv):
                later = n + ahead
                logits(later % nkv, k0 if later < nkv else k_next)
                ps, alphas = [], []
                for g in range(grp):
                    cols = slice(g * tq, (g + 1) * tq)
                    s = s_ref[n, 0:width, cols]
                    m_prev = m_ref[n, :, cols]
                    m_new = jnp.maximum(m_prev, jnp.max(s, axis=0, keepdims=True))
                    alpha = jnp.exp2(m_prev - m_new)
                    p = jnp.exp2(s - m_new)
                    l_ref[n, :, cols] = alpha * l_ref[n, :, cols] + jnp.sum(p, axis=0, keepdims=True)
                    m_ref[n, :, cols] = m_new
                    ps.append(p.astype(BF16))
                    alphas.append(alpha)
                acc_ref[n] = jnp.concatenate(alphas, axis=1) * acc_ref[n] + lax.dot_general(
                    v_ref[pl.ds(k0, width), n * hd:(n + 1) * hd], jnp.concatenate(ps, axis=1), tn,
                    preferred_element_type=F32)
            return carry

        lax.fori_loop(0, n_tiles, body, 0)

    key_tiles(n_far, 0, cfg.tk_far, False, QK_AHEAD_FAR)
    key_tiles(n_near, far_end, TK_NEAR, True, QK_AHEAD_NEAR)
    for n in range(nkv):
        out_t = (acc_ref[n] / l_ref[n]).astype(BF16)
        for g in range(grp):
            h = n * grp + g
            o_ref[:, h * hd:(h + 1) * hd] = lax.dot_general(
                out_t[:, g * tq:(g + 1) * tq], eye_bf, tn, preferred_element_type=F32).astype(o_ref.dtype)


def _dsa(cfg, proj, t_seq, past, k_all, v_all, kidx_all, rel_bias):
    m = proj.shape[0]
    n_seq = m // t_seq
    tq = _tile(t_seq, cfg.tq)
    nq = t_seq // tq
    n_keys = past + t_seq
    lp = k_all.shape[1]
    assert tq % cfg.chunk == 0 and past % LANES == 0 and lp % TK_IDX == 0 and lp >= n_keys
    assert cfg.tk_far % TK_NEAR == 0 and NEAR_BACK == cfg.tk_far and NEAR_COLS == NEAR_BACK + LANES and tq <= LANES
    n_select = min(cfg.top_k_max, n_keys // 4)
    grp = cfg.n_heads // cfg.n_kv
    hq, hidx = cfg.hq, cfg.hidx
    assert cfg.off["q"] % hq == 0 and cfg.off["qi"] % hidx == 0
    assert cfg.head_dim == LANES and LANES % cfg.idx_dim == 0 and cfg.n_idx_heads % (LANES // cfg.idx_dim) == 0
    bias = _near_bias(cfg, rel_bias, tq)
    whole = lambda shape: pl.BlockSpec(shape, lambda b, i: (b,) + (0,) * (len(shape) - 1), pipeline_mode=pl.Buffered(1))
    return pl.pallas_call(
        functools.partial(_dsa_body, cfg, tq, past, n_select),
        grid=(n_seq, nq),
        in_specs=[pl.BlockSpec((tq, hq), lambda b, i: (b * nq + i, cfg.off["q"] // hq)),
                  pl.BlockSpec((tq, hidx), lambda b, i: (b * nq + i, cfg.off["qi"] // hidx)),
                  pl.BlockSpec((tq, LANES), lambda b, i: (b * nq + i, cfg.off["kiw"] // LANES)),
                  whole((None, lp, cfg.hkv)), whole((None, lp, cfg.hkv)), whole((None, lp, cfg.idx_dim)),
                  pl.BlockSpec(bias.shape, lambda b, i: (0, 0, 0), pipeline_mode=pl.Buffered(1))],
        out_specs=pl.BlockSpec((tq, hq), lambda b, i: (b * nq + i, 0)),
        out_shape=jax.ShapeDtypeStruct((m, hq), BF16),
        scratch_shapes=[pltpu.VMEM((lp, tq), I32),
                        pltpu.VMEM((cfg.n_kv, cfg.head_dim, grp * tq), BF16),
                        pltpu.VMEM((cfg.idx_dim, cfg.n_idx_heads * tq), BF16),
                        pltpu.VMEM((cfg.n_idx_heads, tq), F32),
                        pltpu.VMEM((cfg.n_kv, 1, grp * tq), F32), pltpu.VMEM((cfg.n_kv, 1, grp * tq), F32),
                        pltpu.VMEM((cfg.n_kv, cfg.head_dim, grp * tq), F32),
                        pltpu.VMEM((cfg.n_kv, cfg.tk_far, grp * tq), F32),
                        pltpu.VMEM((1, tq), I32)],
        compiler_params=_cparams(("parallel", "arbitrary")),
        name="dsa_attention",
    )(proj, proj, proj, k_all, v_all, kidx_all, bias)


def _pack_w_in(cfg, w_in):
    offs = np.cumsum(np.array(cfg.in_sizes))[:-1].tolist()
    q, k, v, qi, ki, wi, z, xbc, dt, ga, gs = jnp.split(w_in.T, offs, axis=0)
    row_pad = lambda a: jnp.pad(a, ((0, LANES - a.shape[0]), (0, 0)))
    packed = jnp.concatenate([q, k, v, qi, z, xbc, ga, gs, row_pad(jnp.concatenate([ki, wi], axis=0)), row_pad(dt)],
                             axis=0)
    return jnp.pad(packed, ((0, cfg.n_packed - packed.shape[0]), (0, 0))).astype(BF16)


def _pad_keys(a, lp):
    return jnp.pad(a, ((0, 0), (0, lp - a.shape[1]), (0, 0))).astype(BF16)


def _trunk_layer(cfg, x, mod, past_k, past_v, past_ik, ssm0, ssd_conv0, ffn_conv0, rel_bias, wts):
    bsz, t, d = x.shape
    m = bsz * t
    past = past_k.shape[1]
    x2d = x.reshape(m, d)
    off = cfg.off

    u = _norm_mod(cfg, x2d, wts["norm_mix_g"], mod, t, 0, 1)
    proj = _matmul_nt(u, wts["w_in_t"], cfg.tm, cfg.tn_in, F32, "in_proj")
    k_new = proj[:, off["k"]:off["k"] + cfg.hkv].reshape(bsz, t, cfg.hkv)
    v_new = proj[:, off["v"]:off["v"] + cfg.hkv].reshape(bsz, t, cfg.hkv)
    ki_new = proj[:, off["kiw"]:off["kiw"] + cfg.idx_dim].reshape(bsz, t, cfg.idx_dim)
    lp = -(-(past + t) // TK_IDX) * TK_IDX
    k_all = _pad_keys(jnp.concatenate([past_k.reshape(bsz, past, cfg.hkv), k_new], axis=1), lp)
    v_all = _pad_keys(jnp.concatenate([past_v.reshape(bsz, past, cfg.hkv), v_new], axis=1), lp)
    ki_all = _pad_keys(jnp.concatenate([past_ik, ki_new], axis=1), lp)
    attn = _dsa(cfg, proj, t, past, k_all, v_all, ki_all, rel_bias)

    state0_t = jnp.transpose(ssm0.astype(F32), (0, 3, 1, 2)).reshape(bsz, cfg.d_state, cfg.d_ssd)
    ssd_out, state_t, ssd_conv_new = _ssd(cfg, proj, t, ssd_conv0, state0_t, wts["ssd_conv_w"], wts["ssd_conv_b"],
                                          wts["dt_bias"], wts["a_log"], wts["d_skip"], wts["ssd_norm_g"])
    h_new = jnp.transpose(state_t.reshape(bsz, cfg.d_state, cfg.ssd_heads, cfg.ssd_head_dim), (0, 2, 3, 1))

    merged = _merge(cfg, attn, ssd_out, wts["w_attn_o"], wts["w_ssd_o"], proj)
    x1 = _gated_residual(merged, wts["w_out"], x2d, mod, t, 2, cfg.tm, cfg.tn, "out_proj_residual")

    u2 = _norm_mod(cfg, x1, wts["norm_ffn_g"], mod, t, 3, 4)
    h, ffn_conv_new = _ffn_up(cfg, u2, wts["w_up"], wts["ffn_conv_w"], wts["ffn_conv_b"], ffn_conv0, t)
    x2 = _gated_residual(h, wts["w_down"], x1, mod, t, 5, cfg.tm_down, cfg.tn_down, "down_proj_residual")
    states = (k_new.reshape(bsz, t, cfg.n_kv, cfg.head_dim), v_new.reshape(bsz, t, cfg.n_kv, cfg.head_dim), ki_new,
              h_new.astype(ssm0.dtype), ssd_conv_new, ffn_conv_new)
    return x2.reshape(bsz, t, d), states


def _forward(cfg, x_prompt, x_sample, c_prompt, c_sample, cache_k, cache_v, cache_idx_k, state_ssm, state_ssd_conv,
             state_ffn_conv, rel_bias, w_ada, b_ada, norm_mix_g, w_in, ssd_conv_w, ssd_conv_b, dt_bias, a_log, d_skip,
             ssd_norm_g, w_attn_o, w_ssd_o, w_out, norm_ffn_g, w_up, ffn_conv_w, ffn_conv_b, w_down, final_norm_g):
    depth = w_in.shape[0]
    bp, tp, d = x_prompt.shape
    bs, ts, _ = x_sample.shape
    dt_ = x_prompt.dtype
    hp, hs = x_prompt, x_sample
    c_all = jnp.concatenate([c_prompt, c_sample], axis=0)
    c_all = jnp.pad(c_all, ((0, -(bp + bs) % 8), (0, 0)))
    prompt_states, sample_states = [], []
    for l in range(depth):
        mod = _modulation(cfg, c_all, w_ada[l], b_ada[l]).reshape(c_all.shape[0], 6, d)
        wts = dict(norm_mix_g=norm_mix_g[l], w_in_t=_pack_w_in(cfg, w_in[l]), ssd_conv_w=ssd_conv_w[l],
                   ssd_conv_b=ssd_conv_b[l], dt_bias=dt_bias[l], a_log=a_log[l], d_skip=d_skip[l],
                   ssd_norm_g=ssd_norm_g[l], w_attn_o=w_attn_o[l], w_ssd_o=w_ssd_o[l],
                   w_out=w_out[l], norm_ffn_g=norm_ffn_g[l], w_up=w_up[l],
                   ffn_conv_w=ffn_conv_w[l], ffn_conv_b=ffn_conv_b[l], w_down=w_down[l].astype(BF16))
        hp, st_p = _trunk_layer(cfg, hp, mod[:bp],
                                jnp.zeros((bp, 0, cfg.n_kv, cfg.head_dim), dt_),
                                jnp.zeros((bp, 0, cfg.n_kv, cfg.head_dim), dt_),
                                jnp.zeros((bp, 0, cfg.idx_dim), dt_),
                                jnp.zeros((bp, cfg.ssd_heads, cfg.ssd_head_dim, cfg.d_state), state_ssm.dtype),
                                jnp.zeros((bp, cfg.ssd_conv - 1, cfg.conv_dim), dt_),
                                jnp.zeros((bp, cfg.ffn_conv - 1, 2 * cfg.d_ff), dt_),
                                rel_bias, wts)
        hs, st_s = _trunk_layer(cfg, hs, mod[bp:bp + bs], cache_k[l], cache_v[l], cache_idx_k[l], state_ssm[l],
                                state_ssd_conv[l], state_ffn_conv[l], rel_bias, wts)
        prompt_states.append(st_p)
        sample_states.append(st_s)
    y_prompt = _final_norm(cfg, hp.reshape(bp * tp, d), final_norm_g).reshape(bp, tp, d)
    y_sample = _final_norm(cfg, hs.reshape(bs * ts, d), final_norm_g).reshape(bs, ts, d)
    stack = lambda states, i: jnp.stack([s[i] for s in states], axis=0)
    return (y_prompt, y_sample) + tuple(stack(prompt_states, i) for i in range(6)) + tuple(
        stack(sample_states, i) for i in range(6))


def kernel(x_prompt, x_sample, c_prompt, c_sample, cache_k, cache_v, cache_idx_k, state_ssm, state_ssd_conv,
           state_ffn_conv, rel_bias, w_ada, b_ada, norm_mix_g, w_in, ssd_conv_w, ssd_conv_b, dt_bias, a_log, d_skip,
           ssd_norm_g, w_attn_o, w_ssd_o, w_out, norm_ffn_g, w_up, ffn_conv_w, ffn_conv_b, w_down, final_norm_g):
    return _forward(Cfg(), x_prompt, x_sample, c_prompt, c_sample, cache_k, cache_v, cache_idx_k, state_ssm,
                    state_ssd_conv, state_ffn_conv, rel_bias, w_ada, b_ada, norm_mix_g, w_in, ssd_conv_w, ssd_conv_b,
                    dt_bias, a_log, d_skip, ssd_norm_g, w_attn_o, w_ssd_o, w_out, norm_ffn_g, w_up, ffn_conv_w,
                    ffn_conv_b, w_down, final_norm_g)
```
